```python
import functools
import jax, jax.numpy as jnp
from jax import lax
import numpy as np

D_MODEL = 1024
BATCH = 32
SEQ = 256
DEPTH = 1
DEC_BATCH = 8
DEC_SEQ = 4096
PAST_LEN = 512

GRID_W = 64
D_MIX = D_MODEL
D_REC = D_MIX // 2
D_POOL = D_MIX - D_REC
REC_HEADS = 4
REC_DK = D_REC // REC_HEADS
REC_DV = D_REC // REC_HEADS
CHUNK = 32
POOL_WINDOWS = (2, 4, 8, 16)
POOL_GROUPS = len(POOL_WINDOWS)
POOL_GW = D_POOL // POOL_GROUPS
D_IN = 5 * D_REC + D_POOL
D_FF = -(-8 * D_MODEL // (3 * 256)) * 256
DEEPNORM_ALPHA = (2.0 * DEPTH) ** 0.25
DEEPNORM_BETA = (8.0 * DEPTH) ** -0.25
LN_EPS = 1e-5
RMS_EPS = 1e-6

kernel_name = "hymba_hgrn2_pool_dit_step"


def layer_norm(x, g, b):
    xf = x.astype(jnp.float32)
    mu = jnp.mean(xf, axis=-1, keepdims=True)
    var = jnp.mean(jnp.square(xf - mu), axis=-1, keepdims=True)
    return ((xf - mu) * lax.rsqrt(var + LN_EPS) * g + b).astype(x.dtype)


def lower_bounds(raw):
    return jnp.cumsum(jax.nn.softmax(raw.astype(jnp.float32), axis=0), axis=0)


def hgrn2_chunked(q, k, log_f, v, s0):
    bsz, t, h, _ = q.shape
    dv = v.shape[-1]
    n = t // CHUNK

    def blocks(a):
        return a.astype(jnp.float32).reshape(bsz, n, CHUNK, h, a.shape[-1]).transpose(0, 3, 1, 2, 4)

    qc, kc, gc, vc = blocks(q), blocks(k), blocks(log_f), blocks(v)
    b = jnp.cumsum(gc, axis=3)
    b_last = b[:, :, :, -1:]
    q_dec = qc * jnp.exp(b)
    k_inv = kc * jnp.exp(-b)
    k_end = kc * jnp.exp(b_last - b)
    mask = jnp.tril(jnp.ones((CHUNK, CHUNK), dtype=bool))
    scores = jnp.where(mask, jnp.einsum('bhncd,bhnsd->bhncs', q_dec, k_inv), 0.0)
    o_intra = jnp.einsum('bhncs,bhnse->bhnce', scores, vc)
    u = jnp.einsum('bhnsd,bhnse->bhnde', k_end, vc)
    decay = jnp.exp(b_last[:, :, :, 0])

    def step(s, xs):
        dec_n, u_n = xs
        return dec_n[..., None] * s + u_n, s

    s_final, s_starts = lax.scan(step, s0.astype(jnp.float32),
                                 (jnp.moveaxis(decay, 2, 0), jnp.moveaxis(u, 2, 0)))
    s_starts = jnp.moveaxis(s_starts, 0, 2)
    o_inter = jnp.einsum('bhncd,bhnde->bhnce', q_dec, s_starts)
    o = (o_intra + o_inter).transpose(0, 2, 3, 1, 4).reshape(bsz, t, h, dv)
    return o, s_final


def _window_bounds(n, w):
    idx = jnp.arange(n)
    return jnp.clip(idx - w // 2, 0, n), jnp.clip(idx + w // 2, 0, n)


def pool_1d(u, w):
    t = u.shape[1]
    cs = jnp.pad(jnp.cumsum(u.astype(jnp.float32), axis=1), ((0, 0), (1, 0), (0, 0)))
    lo, hi = _window_bounds(t, w)
    total = jnp.take(cs, hi, axis=1) - jnp.take(cs, lo, axis=1)
    return total / (hi - lo).astype(jnp.float32)[None, :, None]


def pool_2d(u, w, rows):
    bsz, t, ch = u.shape
    g = u.astype(jnp.float32).reshape(bsz, rows, GRID_W, ch)
    sat = jnp.pad(jnp.cumsum(jnp.cumsum(g, axis=1), axis=2), ((0, 0), (1, 0), (1, 0), (0, 0)))
    r0, r1 = _window_bounds(rows, w)
    c0, c1 = _window_bounds(GRID_W, w)

    def corner(ri, ci):
        return jnp.take(jnp.take(sat, ri, axis=1), ci, axis=2)

    total = corner(r1, c1) - corner(r0, c1) - corner(r1, c0) + corner(r0, c0)
    cnt = ((r1 - r0)[:, None] * (c1 - c0)[None, :]).astype(jnp.float32)
    return (total / cnt[None, :, :, None]).reshape(bsz, t, ch)


def multi_scale_pool(u, pool_w, pool_scale, pool_fn):
    outs = []
    for gi, w in enumerate(POOL_WINDOWS):
        ug = u[..., gi * POOL_GW:(gi + 1) * POOL_GW]
        diff = (pool_fn(ug, w) - ug.astype(jnp.float32)).astype(u.dtype)
        outs.append(diff @ pool_w[gi])
    return jnp.concatenate(outs, axis=-1) * pool_scale


def hybrid_mixer(h, s_fwd0, s_bwd0, pool_fn, w_in, lb_fwd, lb_bwd, hgrn_norm_g, pool_w, pool_scale, w_out):
    bsz, t, _ = h.shape
    proj = h @ w_in
    q, z_fwd, z_bwd, i_in, g, u_pool = jnp.split(
        proj, [D_REC, 2 * D_REC, 3 * D_REC, 4 * D_REC, 5 * D_REC], axis=-1)
    heads_k = (bsz, t, REC_HEADS, REC_DK)
    q = jax.nn.silu(q).reshape(heads_k)
    v = i_in.reshape(bsz, t, REC_HEADS, REC_DV)

    def gates(z, lb):
        f = lb.reshape(REC_HEADS, REC_DK) + (1.0 - lb.reshape(REC_HEADS, REC_DK)) * jax.nn.sigmoid(
            z.astype(jnp.float32).reshape(heads_k))
        return 1.0 - f, jnp.log(f)

    k_f, log_f_f = gates(z_fwd, lb_fwd)
    k_b, log_f_b = gates(z_bwd, lb_bwd)
    o_f, s_f = hgrn2_chunked(q, k_f, log_f_f, v, s_fwd0)
    o_b_rev, s_b = hgrn2_chunked(q[:, ::-1], k_b[:, ::-1], log_f_b[:, ::-1], v[:, ::-1], s_bwd0)
    o = o_f + o_b_rev[:, ::-1]
    o = o * lax.rsqrt(jnp.mean(jnp.square(o), axis=-1, keepdims=True) + RMS_EPS) * hgrn_norm_g
    y_rec = o.reshape(bsz, t, D_REC).astype(h.dtype) * jax.nn.silu(g)
    y_pool = multi_scale_pool(u_pool, pool_w, pool_scale, pool_fn)
    y = jnp.concatenate([y_rec, y_pool.astype(h.dtype)], axis=-1) @ w_out
    return y, s_f, s_b


def trunk_layer(x, mod, s_fwd0, s_bwd0, pool_fn, w_in, lb_fwd, lb_bwd, hgrn_norm_g, pool_w, pool_scale,
                w_out, ln1_g, ln1_b, w_ffn_gate, w_ffn_up, w_ffn_down, ln2_g, ln2_b):
    shift1, scale1, gate1, shift2, scale2, gate2 = jnp.split(mod, 6, axis=-1)
    h = x * (1.0 + scale1) + shift1
    y, s_f, s_b = hybrid_mixer(h, s_fwd0, s_bwd0, pool_fn, w_in, lb_fwd, lb_bwd, hgrn_norm_g,
                               pool_w, pool_scale, w_out)
    x = layer_norm(DEEPNORM_ALPHA * x + gate1 * y, ln1_g, ln1_b)
    h = x * (1.0 + scale2) + shift2
    ffn = (jax.nn.silu(h @ w_ffn_gate) * (h @ w_ffn_up)) @ w_ffn_down
    x = layer_norm(DEEPNORM_ALPHA * x + gate2 * ffn, ln2_g, ln2_b)
    return x, s_f, s_b


def setup_inputs(seed: int = 0) -> dict:
    key = jax.random.key(seed)
    ks = jax.random.split(key, 22)
    nrm = jax.random.normal
    f32 = jnp.float32
    state_shape = (DEC_BATCH, DEPTH, REC_HEADS, REC_DK, REC_DV)
    return {
        'x_prompt': nrm(ks[0], (BATCH, SEQ, D_MODEL), f32),
        'x_sample': nrm(ks[1], (DEC_BATCH, DEC_SEQ, D_MODEL), f32),
        'state_hgrn_fwd': 0.5 * nrm(ks[2], state_shape, f32),
        'state_hgrn_bwd': 0.5 * nrm(ks[3], state_shape, f32),
        'c': nrm(ks[4], (DEC_BATCH, D_MODEL), f32),
        'c_ctx': nrm(ks[5], (D_MODEL,), f32),
        'w_mod': 0.5 * D_MODEL ** -0.5 * nrm(ks[6], (DEPTH, D_MODEL, 6 * D_MODEL), f32),
        'b_mod': 0.01 * nrm(ks[7], (DEPTH, 6 * D_MODEL), f32),
        'w_in': D_MODEL ** -0.5 * nrm(ks[8], (DEPTH, D_MODEL, D_IN), f32),
        'lb_fwd_raw': 0.1 * nrm(ks[9], (DEPTH + 1, D_REC), f32),
        'lb_bwd_raw': 0.1 * nrm(ks[10], (DEPTH + 1, D_REC), f32),
        'hgrn_norm_g': 1.0 + 0.01 * nrm(ks[11], (DEPTH, REC_DV), f32),
        'pool_w': POOL_GW ** -0.5 * nrm(ks[12], (DEPTH, POOL_GROUPS, POOL_GW, POOL_GW), f32),
        'pool_scale': 1.0 + 0.01 * nrm(ks[13], (DEPTH, D_POOL), f32),
        'w_out': DEEPNORM_BETA * D_MIX ** -0.5 * nrm(ks[14], (DEPTH, D_MIX, D_MODEL), f32),
        'ln1_g': 1.0 + 0.01 * nrm(ks[15], (DEPTH, D_MODEL), f32),
        'ln1_b': 0.01 * nrm(ks[16], (DEPTH, D_MODEL), f32),
        'w_ffn_gate': D_MODEL ** -0.5 * nrm(ks[17], (DEPTH, D_MODEL, D_FF), f32),
        'w_ffn_up': D_MODEL ** -0.5 * nrm(ks[18], (DEPTH, D_MODEL, D_FF), f32),
        'w_ffn_down': DEEPNORM_BETA * D_FF ** -0.5 * nrm(ks[19], (DEPTH, D_FF, D_MODEL), f32),
        'ln2_g': 1.0 + 0.01 * nrm(ks[20], (DEPTH, D_MODEL), f32),
        'ln2_b': 0.01 * nrm(ks[21], (DEPTH, D_MODEL), f32),
    }


def reference(x_prompt, x_sample, state_hgrn_fwd, state_hgrn_bwd, c, c_ctx, w_mod, b_mod, w_in,
              lb_fwd_raw, lb_bwd_raw, hgrn_norm_g, pool_w, pool_scale, w_out, ln1_g, ln1_b,
              w_ffn_gate, w_ffn_up, w_ffn_down, ln2_g, ln2_b):
    rows = x_sample.shape[1] // GRID_W
    pool_lat = functools.partial(pool_2d, rows=rows)
    pool_ctx = pool_1d
    lb_fwd_all = lower_bounds(lb_fwd_raw)
    lb_bwd_all = lower_bounds(lb_bwd_raw)
    zero_state = jnp.zeros((x_prompt.shape[0], REC_HEADS, REC_DK, REC_DV), jnp.float32)
    xp, xs = x_prompt, x_sample
    new_fwd, new_bwd = [], []
    for l in range(DEPTH):
        layer = functools.partial(
            trunk_layer, w_in=w_in[l], lb_fwd=lb_fwd_all[l], lb_bwd=lb_bwd_all[l],
            hgrn_norm_g=hgrn_norm_g[l], pool_w=pool_w[l], pool_scale=pool_scale[l], w_out=w_out[l],
            ln1_g=ln1_g[l], ln1_b=ln1_b[l], w_ffn_gate=w_ffn_gate[l], w_ffn_up=w_ffn_up[l],
            w_ffn_down=w_ffn_down[l], ln2_g=ln2_g[l], ln2_b=ln2_b[l])
        mod_ctx = (jax.nn.silu(c_ctx) @ w_mod[l] + b_mod[l])[None, None, :]
        mod_lat = (jax.nn.silu(c) @ w_mod[l] + b_mod[l])[:, None, :]
        xp, s_f, s_b = layer(xp, mod_ctx, zero_state, zero_state, pool_ctx)
        new_fwd.append(s_f)
        new_bwd.append(s_b)
        xs, _, _ = layer(xs, mod_lat, state_hgrn_fwd[:, l], state_hgrn_bwd[:, l], pool_lat)
    new_state_hgrn_fwd = jnp.stack(new_fwd, axis=1)
    new_state_hgrn_bwd = jnp.stack(new_bwd, axis=1)
    return (xp, xs, new_state_hgrn_fwd, new_state_hgrn_bwd)
```

```python
import functools

import numpy as np
import jax
import jax.numpy as jnp
from jax import lax
from jax.experimental import pallas as pl
from jax.experimental.pallas import tpu as pltpu

F32 = jnp.float32
BF16 = jnp.bfloat16

D_MODEL = 1024
D_REC = 512
D_POOL = 512
HEADS = 4
D_HEAD = 128
D_IN = 5 * D_REC + D_POOL
D_FF = 2816
GRID_W = 64
POOL_WINDOWS = (2, 4, 8, 16)
POOL_GW = 128
ALPHA = 2.0 ** 0.25
LN_EPS = 1e-5
RMS_EPS = 1e-6

CHUNK = 64
MID_F = CHUNK // 2 - 1
MID_B = CHUNK // 2
TM_PROJ = 512
TM_TAIL = 256
N_FF_SPLIT = 2
P_COLS = 7 * D_REC
E_COLS = 4 * D_REC
VMEM_LIMIT = 56 * 1024 * 1024

_ARB = "arbitrary"


def _params(n_axes):
    return pltpu.CompilerParams(
        dimension_semantics=(_ARB,) * n_axes, vmem_limit_bytes=VMEM_LIMIT)


def _const_spec(shape):
    nd = len(shape)
    return pl.BlockSpec(shape, lambda *_: (0,) * nd, pipeline_mode=pl.Buffered(1))


def _mod_kernel(c_ref, w_ref, b_ref, o_ref):
    c = c_ref[...]
    a = c * jax.nn.sigmoid(c)
    o_ref[...] = jnp.dot(a, w_ref[...], precision=lax.Precision.HIGHEST,
                         preferred_element_type=F32) + b_ref[...]


def _mod_call(c_all, w_mod, b_mod):
    rows = c_all.shape[0]
    n_out = w_mod.shape[1]
    blk = 1024
    return pl.pallas_call(
        _mod_kernel,
        grid=(n_out // blk,),
        in_specs=[
            pl.BlockSpec((rows, D_MODEL), lambda j: (0, 0)),
            pl.BlockSpec((D_MODEL, blk), lambda j: (0, j)),
            pl.BlockSpec((1, blk), lambda j: (0, j)),
        ],
        out_specs=pl.BlockSpec((rows, blk), lambda j: (0, j)),
        out_shape=jax.ShapeDtypeStruct((rows, n_out), F32),
        compiler_params=_params(1),
        name="mod",
    )(c_all, w_mod, b_mod)


def _lower_bound(raw_ref):
    r0 = raw_ref[0:1, :]
    r1 = raw_ref[1:2, :]
    mx = jnp.maximum(r0, r1)
    e0 = jnp.exp(r0 - mx)
    e1 = jnp.exp(r1 - mx)
    return e0 / (e0 + e1)


def _split_bf16(a):
    hi = a.astype(BF16)
    lo = (a - hi.astype(F32)).astype(BF16)
    return hi, lo


def _proj_kernel(x_ref, sh_ref, sc_ref, w_ref, lbf_ref, lbb_ref, p_ref, e_ref,
                 proj_scr, *, cps):
    tm = x_ref.shape[0]
    h = (x_ref[...] * (1.0 + sc_ref[0]) + sh_ref[0]).astype(BF16)
    proj_scr[...] = jnp.dot(h, w_ref[...], preferred_element_type=F32)

    lb_f = _lower_bound(lbf_ref)
    lb_b = _lower_bound(lbb_ref)
    ri = lax.broadcasted_iota(jnp.int32, (CHUNK, CHUNK), 0)
    ci = lax.broadcasted_iota(jnp.int32, (CHUNK, CHUNK), 1)
    tri_pre = jnp.where(ci <= ri, 1.0, 0.0).astype(BF16)
    tri_suf = jnp.where(ci >= ri, 1.0, 0.0).astype(BF16)

    def chunk_sum(tri, a):
        hi, lo = _split_bf16(a)
        return (jnp.dot(tri, hi, preferred_element_type=F32)
                + jnp.dot(tri, lo, preferred_element_type=F32))

    for c in range(tm // CHUNK):
        rows = slice(c * CHUNK, (c + 1) * CHUNK)
        q = proj_scr[rows, 0:D_REC]
        qs = q * jax.nn.sigmoid(q)

        def direction(z, lb, tri, mid, end):
            f = lb + (1.0 - lb) * jax.nn.sigmoid(z)
            k = 1.0 - f
            b = chunk_sum(tri, jnp.log(f))
            m = b[mid:mid + 1, :]
            d = b - m
            qd = qs * jnp.exp(d)
            ki = k * jnp.exp(-d)
            em = jnp.exp(m)
            r = jnp.exp(b[end:end + 1, :] - m)
            return qd, ki, em, r

        qd_f, ki_f, em_f, r_f = direction(
            proj_scr[rows, D_REC:2 * D_REC], lb_f, tri_pre, MID_F, CHUNK - 1)
        qd_b, ki_b, em_b, r_b = direction(
            proj_scr[rows, 2 * D_REC:3 * D_REC], lb_b, tri_suf, MID_B, 0)
        g = proj_scr[rows, 4 * D_REC:5 * D_REC]

        p_ref[rows, 0 * D_REC:1 * D_REC] = qd_f.astype(BF16)
        p_ref[rows, 1 * D_REC:2 * D_REC] = ki_f.astype(BF16)
        p_ref[rows, 2 * D_REC:3 * D_REC] = qd_b.astype(BF16)
        p_ref[rows, 3 * D_REC:4 * D_REC] = ki_b.astype(BF16)
        p_ref[rows, 4 * D_REC:5 * D_REC] = proj_scr[rows, 3 * D_REC:4 * D_REC].astype(BF16)
        p_ref[rows, 5 * D_REC:6 * D_REC] = (g * jax.nn.sigmoid(g)).astype(BF16)
        p_ref[rows, 6 * D_REC:7 * D_REC] = proj_scr[rows, 5 * D_REC:6 * D_REC].astype(BF16)

        s, cc = c // cps, c % cps
        e_ref[s, cc:cc + 1, 0 * D_REC:1 * D_REC] = em_f
        e_ref[s, cc:cc + 1, 1 * D_REC:2 * D_REC] = r_f
        e_ref[s, cc:cc + 1, 2 * D_REC:3 * D_REC] = em_b
        e_ref[s, cc:cc + 1, 3 * D_REC:4 * D_REC] = r_b


def _proj_call(x2, mod3, mod_row, w_in, lbf_raw, lbb_raw, n_seq, seq_len):
    n_tok = x2.shape[0]
    tm = TM_PROJ
    cpt = tm // CHUNK
    cps = min(cpt, seq_len // CHUNK)
    spt = cpt // cps
    tps = max(1, seq_len // tm)
    n_ck = seq_len // CHUNK

    if spt > 1:
        e_spec = pl.BlockSpec((spt, cps, E_COLS), lambda i: (i, 0, 0))
    else:
        e_spec = pl.BlockSpec((1, cps, E_COLS), lambda i: (i // tps, i % tps, 0))

    return pl.pallas_call(
        functools.partial(_proj_kernel, cps=cps),
        grid=(n_tok // tm,),
        in_specs=[
            pl.BlockSpec((tm, D_MODEL), lambda i: (i, 0)),
            pl.BlockSpec((1, 1, D_MODEL), lambda i: (mod_row(i, tps), 0, 0)),
            pl.BlockSpec((1, 1, D_MODEL), lambda i: (mod_row(i, tps), 0, 1)),
            _const_spec((D_MODEL, D_IN)),
            _const_spec((2, D_REC)),
            _const_spec((2, D_REC)),
        ],
        out_specs=[
            pl.BlockSpec((tm, P_COLS), lambda i: (i, 0)),
            e_spec,
        ],
        out_shape=[
            jax.ShapeDtypeStruct((n_tok, P_COLS), BF16),
            jax.ShapeDtypeStruct((n_seq, n_ck, E_COLS), F32),
        ],
        scratch_shapes=[pltpu.VMEM((tm, D_IN), F32)],
        compiler_params=_params(1),
        name="proj",
    )(x2, mod3, mod3, w_in, lbf_raw, lbb_raw)


def _scan_kernel(*refs, nck, has_init, emit_final):
    it = iter(refs)
    kif_ref, vf_ref, ef_ref, kib_ref, vb_ref, eb_ref = (next(it) for _ in range(6))
    if has_init:
        s0f_ref, s0b_ref = next(it), next(it)
    smf_ref, smb_ref = next(it), next(it)
    if emit_final:
        finf_ref, finb_ref = next(it), next(it)
    sf_scr, sb_scr = next(it), next(it)

    j = pl.program_id(1)

    @pl.when(j == 0)
    def _():
        for h in range(HEADS):
            if has_init:
                sf_scr[h] = s0f_ref[0, h].T
                sb_scr[h] = s0b_ref[0, h].T
            else:
                sf_scr[h] = jnp.zeros((D_HEAD, D_HEAD), F32)
                sb_scr[h] = jnp.zeros((D_HEAD, D_HEAD), F32)

    def step(ki_ref, v_ref, e_ref, sm_ref, s_scr, c):
        rows = slice(c * CHUNK, (c + 1) * CHUNK)
        for h in range(HEADS):
            cols = slice(h * D_HEAD, (h + 1) * D_HEAD)
            em = e_ref[0, c:c + 1, h * D_HEAD:(h + 1) * D_HEAD]
            r = e_ref[0, c:c + 1, D_REC + h * D_HEAD:D_REC + (h + 1) * D_HEAD]
            s = s_scr[h]
            sm_ref[0, c, h] = (s * em).astype(BF16)
            u_t = lax.dot_general(v_ref[rows, cols], ki_ref[rows, cols],
                                  (((0,), (0,)), ((), ())),
                                  preferred_element_type=F32)
            s_scr[h] = s * (em * r) + u_t * r

    for c in range(nck):
        step(kif_ref, vf_ref, ef_ref, smf_ref, sf_scr, c)
        step(kib_ref, vb_ref, eb_ref, smb_ref, sb_scr, nck - 1 - c)

    if emit_final:
        @pl.when(j == pl.num_programs(1) - 1)
        def _():
            for h in range(HEADS):
                finf_ref[0, h] = sf_scr[h].T
                finb_ref[0, h] = sb_scr[h].T


def _scan_call(p, e, n_seq, seq_len, s0f=None, s0b=None, emit_final=False):
    tb = min(seq_len, 512)
    nt = seq_len // tb
    nck = tb // CHUNK
    n_ck = seq_len // CHUNK
    has_init = s0f is not None
    st_blk = (1, HEADS, D_HEAD, D_HEAD)

    in_specs = [
        pl.BlockSpec((tb, D_REC), lambda b, j: (b * nt + j, 1)),
        pl.BlockSpec((tb, D_REC), lambda b, j: (b * nt + j, 4)),
        pl.BlockSpec((1, nck, 2 * D_REC), lambda b, j: (b, j, 0)),
        pl.BlockSpec((tb, D_REC), lambda b, j: (b * nt + nt - 1 - j, 3)),
        pl.BlockSpec((tb, D_REC), lambda b, j: (b * nt + nt - 1 - j, 4)),
        pl.BlockSpec((1, nck, 2 * D_REC), lambda b, j: (b, nt - 1 - j, 1)),
    ]
    args = [p, p, e, p, p, e]
    if has_init:
        in_specs += [pl.BlockSpec(st_blk, lambda b, j: (b, 0, 0, 0))] * 2
        args += [s0f, s0b]
    sm_shape = jax.ShapeDtypeStruct((n_seq, n_ck, HEADS, D_HEAD, D_HEAD), BF16)
    sm_blk = (1, nck, HEADS, D_HEAD, D_HEAD)
    out_specs = [
        pl.BlockSpec(sm_blk, lambda b, j: (b, j, 0, 0, 0)),
        pl.BlockSpec(sm_blk, lambda b, j: (b, nt - 1 - j, 0, 0, 0)),
    ]
    out_shape = [sm_shape, sm_shape]
    if emit_final:
        out_specs += [pl.BlockSpec(st_blk, lambda b, j: (b, 0, 0, 0))] * 2
        out_shape += [jax.ShapeDtypeStruct((n_seq, HEADS, D_HEAD, D_HEAD), F32)] * 2

    return pl.pallas_call(
        functools.partial(_scan_kernel, nck=nck, has_init=has_init,
                          emit_final=emit_final),
        grid=(n_seq, nt),
        in_specs=in_specs,
        out_specs=out_specs,
        out_shape=out_shape,
        scratch_shapes=[pltpu.VMEM((HEADS, D_HEAD, D_HEAD), F32)] * 2,
        compiler_params=_params(2),
        name="scan",
    )(*args)


def _window(n, w):
    idx = np.arange(n)
    return np.clip(idx - w // 2, 0, n), np.clip(idx + w // 2, 0, n)


def _band(n, w):
    lo, hi = _window(n, w)
    j = np.arange(n)[None, :]
    return ((j >= lo[:, None]) & (j < hi[:, None])).astype(np.float32), (hi - lo)


def _pool_seq_kernel(u_ref, a_ref, inv_ref, pw_ref, ps_ref, o_ref, *, n_seq_blk, seq_len):
    for s in range(n_seq_blk):
        rows = slice(s * seq_len, (s + 1) * seq_len)
        for gi in range(len(POOL_WINDOWS)):
            cols = slice(gi * POOL_GW, (gi + 1) * POOL_GW)
            ug = u_ref[rows, cols]
            tot = jnp.dot(a_ref[gi], ug, preferred_element_type=F32)
            diff = (tot * inv_ref[gi] - ug.astype(F32)).astype(BF16)
            y = jnp.dot(diff, pw_ref[gi], preferred_element_type=F32)
            o_ref[rows, cols] = (y * ps_ref[:, cols]).astype(BF16)


def _pool_grid_kernel(u_ref, a_ref, invc_ref, pw_ref, ps_ref, o_ref, col_scr, *,
                      n_rows):
    blk = a_ref.shape[1]
    rpb = blk // GRID_W
    for gi, w in enumerate(POOL_WINDOWS):
        cols = slice(gi * POOL_GW, (gi + 1) * POOL_GW)
        for rb in range(n_rows // rpb):
            rows = slice(rb * blk, (rb + 1) * blk)
            tot = jnp.dot(a_ref[gi], u_ref[rows, cols], preferred_element_type=F32)
            col_scr[rows, :] = tot * invc_ref[gi]
        lo, hi = _window(n_rows, w)
        for r in range(n_rows):
            acc = col_scr[int(lo[r]) * GRID_W:(int(lo[r]) + 1) * GRID_W, :]
            for rr in range(int(lo[r]) + 1, int(hi[r])):
                acc = acc + col_scr[rr * GRID_W:(rr + 1) * GRID_W, :]
            rows = slice(r * GRID_W, (r + 1) * GRID_W)
            mean = acc * (1.0 / float(hi[r] - lo[r]))
            diff = (mean - u_ref[rows, cols].astype(F32)).astype(BF16)
            y = jnp.dot(diff, pw_ref[gi], preferred_element_type=F32)
            o_ref[rows, cols] = (y * ps_ref[:, cols]).astype(BF16)


def _pool_call(p, pool_w, pool_scale, n_seq, seq_len, two_d):
    n_tok = p.shape[0]
    ng = len(POOL_WINDOWS)
    if two_d:
        n_rows = seq_len // GRID_W
        rpb = 4
        blk = rpb * GRID_W
        bands = [_band(GRID_W, w) for w in POOL_WINDOWS]
        a = np.stack([np.kron(np.eye(rpb, dtype=np.float32), b[0]) for b in bands])
        inv = np.stack([np.broadcast_to(np.tile(1.0 / b[1], rpb)[:, None], (blk, POOL_GW))
                        for b in bands]).astype(np.float32)
        kern = functools.partial(_pool_grid_kernel, n_rows=n_rows)
        tb = seq_len
        scratch = [pltpu.VMEM((seq_len, POOL_GW), F32)]
    else:
        blk = seq_len
        bands = [_band(seq_len, w) for w in POOL_WINDOWS]
        a = np.stack([b[0] for b in bands])
        inv = np.stack([np.broadcast_to((1.0 / b[1])[:, None], (blk, POOL_GW))
                        for b in bands]).astype(np.float32)
        n_seq_blk = 8
        kern = functools.partial(_pool_seq_kernel, n_seq_blk=n_seq_blk, seq_len=seq_len)
        tb = n_seq_blk * seq_len
        scratch = []
    return pl.pallas_call(
        kern,
        grid=(n_tok // tb,),
        in_specs=[
            pl.BlockSpec((tb, D_POOL), lambda i: (i, 6)),
            _const_spec((ng, blk, blk)),
            _const_spec((ng, blk, POOL_GW)),
            _const_spec((ng, POOL_GW, POOL_GW)),
            _const_spec((1, D_POOL)),
        ],
        out_specs=pl.BlockSpec((tb, D_POOL), lambda i: (i, 0)),
        out_shape=jax.ShapeDtypeStruct((n_tok, D_POOL), BF16),
        scratch_shapes=scratch,
        compiler_params=_params(1),
        name="pool",
    )(p, jnp.asarray(a, BF16), jnp.asarray(inv, F32), pool_w, pool_scale)


def _layer_norm(u, g, b):
    mu = jnp.mean(u, axis=-1, keepdims=True)
    d = u - mu
    var = jnp.mean(d * d, axis=-1, keepdims=True)
    return d * lax.rsqrt(var + LN_EPS) * g + b


def _tail_kernel(x_ref, p_ref, smf_ref, smb_ref, yp_ref, g1_ref, sh2_ref, sc2_ref,
                 g2_ref, gn_ref, wo_ref, l1g_ref, l1b_ref, wg_ref, wu_ref, wd_ref,
                 l2g_ref, l2b_ref, o_ref, yr_scr, *, cps):
    tm = x_ref.shape[0]
    ri = lax.broadcasted_iota(jnp.int32, (CHUNK, CHUNK), 0)
    ci = lax.broadcasted_iota(jnp.int32, (CHUNK, CHUNK), 1)
    lower = ci <= ri
    upper = ci >= ri
    nt_dims = (((1,), (1,)), ((), ()))
    gn = gn_ref[...]

    for c in range(tm // CHUNK):
        rows = slice(c * CHUNK, (c + 1) * CHUNK)
        s_i, c_i = c // cps, c % cps
        for h in range(HEADS):
            def col(k):
                return slice(k * D_REC + h * D_HEAD, k * D_REC + (h + 1) * D_HEAD)
            qd_f = p_ref[rows, col(0)]
            ki_f = p_ref[rows, col(1)]
            qd_b = p_ref[rows, col(2)]
            ki_b = p_ref[rows, col(3)]
            v = p_ref[rows, col(4)]
            gs = p_ref[rows, col(5)]
            sc_f = lax.dot_general(qd_f, ki_f, nt_dims, preferred_element_type=F32)
            sc_b = lax.dot_general(qd_b, ki_b, nt_dims, preferred_element_type=F32)
            sc = (jnp.where(lower, sc_f, 0.0) + jnp.where(upper, sc_b, 0.0)).astype(BF16)
            o = jnp.dot(sc, v, preferred_element_type=F32)
            o = o + lax.dot_general(qd_f, smf_ref[s_i, c_i, h], nt_dims,
                                    preferred_element_type=F32)
            o = o + lax.dot_general(qd_b, smb_ref[s_i, c_i, h], nt_dims,
                                    preferred_element_type=F32)
            ms = jnp.mean(o * o, axis=-1, keepdims=True)
            on = o * lax.rsqrt(ms + RMS_EPS) * gn
            yr_scr[rows, h * D_HEAD:(h + 1) * D_HEAD] = (on * gs.astype(F32)).astype(BF16)

    y = jnp.dot(yr_scr[...], wo_ref[0:D_REC, :], preferred_element_type=F32)
    y = y + jnp.dot(yp_ref[...], wo_ref[D_REC:2 * D_REC, :], preferred_element_type=F32)
    x1 = _layer_norm(ALPHA * x_ref[...] + g1_ref[0] * y, l1g_ref[...], l1b_ref[...])
    h2 = (x1 * (1.0 + sc2_ref[0]) + sh2_ref[0]).astype(BF16)
    fs = D_FF // N_FF_SPLIT
    ffn = None
    for k in range(N_FF_SPLIT):
        a = jnp.dot(h2, wg_ref[:, k * fs:(k + 1) * fs], preferred_element_type=F32)
        b = jnp.dot(h2, wu_ref[:, k * fs:(k + 1) * fs], preferred_element_type=F32)
        act = (a * jax.nn.sigmoid(a) * b).astype(BF16)
        part = jnp.dot(act, wd_ref[k * fs:(k + 1) * fs, :], preferred_element_type=F32)
        ffn = part if ffn is None else ffn + part
    o_ref[...] = _layer_norm(ALPHA * x1 + g2_ref[0] * ffn, l2g_ref[...], l2b_ref[...])


def _tail_call(x2, p, smf, smb, yp, mod3, mod_row, gn, wo, l1g, l1b, wg, wu, wd,
               l2g, l2b, seq_len):
    n_tok = x2.shape[0]
    tm = min(TM_TAIL, seq_len)
    cps = tm // CHUNK
    tps = seq_len // tm
    sm_blk = (1, cps, HEADS, D_HEAD, D_HEAD)

    def mod_spec(col):
        return pl.BlockSpec((1, 1, D_MODEL), lambda i: (mod_row(i, tps), 0, col))

    sm_spec = pl.BlockSpec(sm_blk, lambda i: (i // tps, i % tps, 0, 0, 0))
    return pl.pallas_call(
        functools.partial(_tail_kernel, cps=cps),
        grid=(n_tok // tm,),
        in_specs=[
            pl.BlockSpec((tm, D_MODEL), lambda i: (i, 0)),
            pl.BlockSpec((tm, 6 * D_REC), lambda i: (i, 0)),
            sm_spec, sm_spec,
            pl.BlockSpec((tm, D_POOL), lambda i: (i, 0)),
            mod_spec(2), mod_spec(3), mod_spec(4), mod_spec(5),
            _const_spec((1, D_HEAD)),
            _const_spec((D_MODEL, D_MODEL)),
            _const_spec((1, D_MODEL)), _const_spec((1, D_MODEL)),
            _const_spec((D_MODEL, D_FF)), _const_spec((D_MODEL, D_FF)),
            _const_spec((D_FF, D_MODEL)),
            _const_spec((1, D_MODEL)), _const_spec((1, D_MODEL)),
        ],
        out_specs=pl.BlockSpec((tm, D_MODEL), lambda i: (i, 0)),
        out_shape=jax.ShapeDtypeStruct((n_tok, D_MODEL), F32),
        scratch_shapes=[pltpu.VMEM((tm, D_REC), BF16)],
        compiler_params=_params(1),
        name="tail",
    )(x2, p, smf, smb, yp, mod3, mod3, mod3, mod3, gn, wo, l1g, l1b, wg, wu, wd,
      l2g, l2b)


def kernel(x_prompt, x_sample, state_hgrn_fwd, state_hgrn_bwd, c, c_ctx, w_mod, b_mod,
           w_in, lb_fwd_raw, lb_bwd_raw, hgrn_norm_g, pool_w, pool_scale, w_out, ln1_g,
           ln1_b, w_ffn_gate, w_ffn_up, w_ffn_down, ln2_g, ln2_b):
    n_ctx, t_ctx, _ = x_prompt.shape
    n_lat, t_lat, _ = x_sample.shape
    ctx_row = n_lat

    c_all = jnp.concatenate(
        [c, c_ctx[None, :], jnp.zeros((16 - n_lat - 1, D_MODEL), F32)], axis=0)
    mod = _mod_call(c_all, w_mod[0], b_mod)
    mod3 = mod.reshape(16, 1, 6 * D_MODEL)

    w_in_b = w_in[0].astype(BF16)
    wo_b = w_out[0].astype(BF16)
    wg_b = w_ffn_gate[0].astype(BF16)
    wu_b = w_ffn_up[0].astype(BF16)
    wd_b = w_ffn_down[0].astype(BF16)
    pw_b = pool_w[0].astype(BF16)

    def run(x, n_seq, seq_len, mod_row, two_d, s0f, s0b, emit_final):
        x2 = x.reshape(n_seq * seq_len, D_MODEL)
        p, e = _proj_call(x2, mod3, mod_row, w_in_b, lb_fwd_raw, lb_bwd_raw, n_seq, seq_len)
        scan = _scan_call(p, e, n_seq, seq_len, s0f, s0b, emit_final)
        yp = _pool_call(p, pw_b, pool_scale, n_seq, seq_len, two_d)
        y = _tail_call(x2, p, scan[0], scan[1], yp, mod3, mod_row, hgrn_norm_g, wo_b,
                       ln1_g, ln1_b, wg_b, wu_b, wd_b, ln2_g, ln2_b, seq_len)
        return y.reshape(n_seq, seq_len, D_MODEL), scan[2:]

    y_ctx, fin = run(x_prompt, n_ctx, t_ctx, lambda i, tps: ctx_row, False,
                     None, None, True)
    y_lat, _ = run(x_sample, n_lat, t_lat, lambda i, tps: i // tps, True,
                   state_hgrn_fwd[:, 0], state_hgrn_bwd[:, 0], False)
    return (y_ctx, y_lat, fin[0][:, None], fin[1][:, None])
```

```python
import functools

import numpy as np
import jax
import jax.numpy as jnp
from jax import lax
from jax.experimental import pallas as pl
from jax.experimental.pallas import tpu as pltpu

F32 = jnp.float32
BF16 = jnp.bfloat16

D_MODEL = 1024
D_REC = 512
D_POOL = 512
HEADS = 4
D_HEAD = 128
D_IN = 5 * D_REC + D_POOL
D_FF = 2816
GRID_W = 64
POOL_WINDOWS = (2, 4, 8, 16)
POOL_GW = 128
ALPHA = 2.0 ** 0.25
LN_EPS = 1e-5
RMS_EPS = 1e-6

CHUNK = 64
MID_F = CHUNK // 2 - 1
MID_B = CHUNK // 2
TM_PROJ = 512
TM_TAIL = 256
SLICE = 256
ROWS = 64
P_COLS = 7 * D_REC
E_COLS = 4 * D_REC
VMEM_LIMIT = 56 * 1024 * 1024

_ARB = "arbitrary"


def _params(n_axes):
    return pltpu.CompilerParams(
        dimension_semantics=(_ARB,) * n_axes, vmem_limit_bytes=VMEM_LIMIT)


def _const_spec(shape):
    nd = len(shape)
    return pl.BlockSpec(shape, lambda *_: (0,) * nd, pipeline_mode=pl.Buffered(1))


def _mod_kernel(c_ref, w_ref, b_ref, o_ref):
    c = c_ref[...]
    a = c * jax.nn.sigmoid(c)
    o_ref[...] = jnp.dot(a, w_ref[...], precision=lax.Precision.HIGHEST,
                         preferred_element_type=F32) + b_ref[...]


def _mod_call(c_all, w_mod, b_mod):
    rows = c_all.shape[0]
    n_out = w_mod.shape[1]
    blk = 1024
    return pl.pallas_call(
        _mod_kernel,
        grid=(n_out // blk,),
        in_specs=[
            pl.BlockSpec((rows, D_MODEL), lambda j: (0, 0)),
            pl.BlockSpec((D_MODEL, blk), lambda j: (0, j)),
            pl.BlockSpec((1, blk), lambda j: (0, j)),
        ],
        out_specs=pl.BlockSpec((rows, blk), lambda j: (0, j)),
        out_shape=jax.ShapeDtypeStruct((rows, n_out), F32),
        compiler_params=_params(1),
        name="mod",
    )(c_all, w_mod, b_mod)


def _lower_bound(raw_ref):
    r0 = raw_ref[0:1, :]
    r1 = raw_ref[1:2, :]
    mx = jnp.maximum(r0, r1)
    e0 = jnp.exp(r0 - mx)
    e1 = jnp.exp(r1 - mx)
    return e0 / (e0 + e1)


def _split_bf16(a):
    hi = a.astype(BF16)
    lo = (a - hi.astype(F32)).astype(BF16)
    return hi, lo


def _proj_kernel(x_ref, sh_ref, sc_ref, w_ref, lbf_ref, lbb_ref, p_ref, e_ref,
                 proj_scr, *, cps):
    tm = x_ref.shape[0]
    h = (x_ref[...] * (1.0 + sc_ref[0]) + sh_ref[0]).astype(BF16)
    proj_scr[...] = jnp.dot(h, w_ref[...], preferred_element_type=F32)

    lb_f = _lower_bound(lbf_ref)
    lb_b = _lower_bound(lbb_ref)
    ri = lax.broadcasted_iota(jnp.int32, (CHUNK, CHUNK), 0)
    ci = lax.broadcasted_iota(jnp.int32, (CHUNK, CHUNK), 1)
    tri_pre = jnp.where(ci <= ri, 1.0, 0.0).astype(BF16)
    tri_suf = jnp.where(ci >= ri, 1.0, 0.0).astype(BF16)

    def chunk_sum(tri, a):
        hi, lo = _split_bf16(a)
        return (jnp.dot(tri, hi, preferred_element_type=F32)
                + jnp.dot(tri, lo, preferred_element_type=F32))

    for c in range(tm // CHUNK):
        rows = slice(c * CHUNK, (c + 1) * CHUNK)
        q = proj_scr[rows, 0:D_REC]
        qs = q * jax.nn.sigmoid(q)

        def direction(z, lb, tri, mid, end):
            f = lb + (1.0 - lb) * jax.nn.sigmoid(z)
            k = 1.0 - f
            b = chunk_sum(tri, jnp.log(f))
            m = b[mid:mid + 1, :]
            d = b - m
            qd = qs * jnp.exp(d)
            ki = k * jnp.exp(-d)
            em = jnp.exp(m)
            r = jnp.exp(b[end:end + 1, :] - m)
            return qd, ki, em, r

        qd_f, ki_f, em_f, r_f = direction(
            proj_scr[rows, D_REC:2 * D_REC], lb_f, tri_pre, MID_F, CHUNK - 1)
        qd_b, ki_b, em_b, r_b = direction(
            proj_scr[rows, 2 * D_REC:3 * D_REC], lb_b, tri_suf, MID_B, 0)
        g = proj_scr[rows, 4 * D_REC:5 * D_REC]

        p_ref[rows, 0 * D_REC:1 * D_REC] = qd_f.astype(BF16)
        p_ref[rows, 1 * D_REC:2 * D_REC] = ki_f.astype(BF16)
        p_ref[rows, 2 * D_REC:3 * D_REC] = qd_b.astype(BF16)
        p_ref[rows, 3 * D_REC:4 * D_REC] = ki_b.astype(BF16)
        p_ref[rows, 4 * D_REC:5 * D_REC] = proj_scr[rows, 3 * D_REC:4 * D_REC].astype(BF16)
        p_ref[rows, 5 * D_REC:6 * D_REC] = (g * jax.nn.sigmoid(g)).astype(BF16)
        p_ref[rows, 6 * D_REC:7 * D_REC] = proj_scr[rows, 5 * D_REC:6 * D_REC].astype(BF16)

        s, cc = c // cps, c % cps
        e_ref[s, cc:cc + 1, 0 * D_REC:1 * D_REC] = em_f
        e_ref[s, cc:cc + 1, 1 * D_REC:2 * D_REC] = r_f
        e_ref[s, cc:cc + 1, 2 * D_REC:3 * D_REC] = em_b
        e_ref[s, cc:cc + 1, 3 * D_REC:4 * D_REC] = r_b


def _proj_call(x2, mod3, mod_row, w_in, lbf_raw, lbb_raw, n_seq, seq_len):
    n_tok = x2.shape[0]
    tm = TM_PROJ
    cpt = tm // CHUNK
    cps = min(cpt, seq_len // CHUNK)
    spt = cpt // cps
    tps = max(1, seq_len // tm)
    n_ck = seq_len // CHUNK

    if spt > 1:
        e_spec = pl.BlockSpec((spt, cps, E_COLS), lambda i: (i, 0, 0))
    else:
        e_spec = pl.BlockSpec((1, cps, E_COLS), lambda i: (i // tps, i % tps, 0))

    return pl.pallas_call(
        functools.partial(_proj_kernel, cps=cps),
        grid=(n_tok // tm,),
        in_specs=[
            pl.BlockSpec((tm, D_MODEL), lambda i: (i, 0)),
            pl.BlockSpec((1, 1, D_MODEL), lambda i: (mod_row(i, tps), 0, 0)),
            pl.BlockSpec((1, 1, D_MODEL), lambda i: (mod_row(i, tps), 0, 1)),
            _const_spec((D_MODEL, D_IN)),
            _const_spec((2, D_REC)),
            _const_spec((2, D_REC)),
        ],
        out_specs=[
            pl.BlockSpec((tm, P_COLS), lambda i: (i, 0)),
            e_spec,
        ],
        out_shape=[
            jax.ShapeDtypeStruct((n_tok, P_COLS), BF16),
            jax.ShapeDtypeStruct((n_seq, n_ck, E_COLS), F32),
        ],
        scratch_shapes=[pltpu.VMEM((tm, D_IN), F32)],
        compiler_params=_params(1),
        name="proj",
    )(x2, mod3, mod3, w_in, lbf_raw, lbb_raw)


def _scan_kernel(*refs, nck, has_init, emit_final):
    it = iter(refs)
    kif_ref, vf_ref, ef_ref, kib_ref, vb_ref, eb_ref = (next(it) for _ in range(6))
    if has_init:
        s0f_ref, s0b_ref = next(it), next(it)
    smf_ref, smb_ref = next(it), next(it)
    if emit_final:
        finf_ref, finb_ref = next(it), next(it)
    sf_scr, sb_scr = next(it), next(it)

    j = pl.program_id(1)

    @pl.when(j == 0)
    def _():
        for h in range(HEADS):
            if has_init:
                sf_scr[h] = s0f_ref[0, h].T
                sb_scr[h] = s0b_ref[0, h].T
            else:
                sf_scr[h] = jnp.zeros((D_HEAD, D_HEAD), F32)
                sb_scr[h] = jnp.zeros((D_HEAD, D_HEAD), F32)

    def step(ki_ref, v_ref, e_ref, sm_ref, s_scr, c):
        rows = slice(c * CHUNK, (c + 1) * CHUNK)
        for h in range(HEADS):
            cols = slice(h * D_HEAD, (h + 1) * D_HEAD)
            em = e_ref[0, c:c + 1, h * D_HEAD:(h + 1) * D_HEAD]
            r = e_ref[0, c:c + 1, D_REC + h * D_HEAD:D_REC + (h + 1) * D_HEAD]
            s = s_scr[h]
            sm_ref[0, c, h] = (s * em).astype(BF16)
            u_t = lax.dot_general(v_ref[rows, cols], ki_ref[rows, cols],
                                  (((0,), (0,)), ((), ())),
                                  preferred_element_type=F32)
            s_scr[h] = s * (em * r) + u_t * r

    for c in range(nck):
        step(kif_ref, vf_ref, ef_ref, smf_ref, sf_scr, c)
        step(kib_ref, vb_ref, eb_ref, smb_ref, sb_scr, nck - 1 - c)

    if emit_final:
        @pl.when(j == pl.num_programs(1) - 1)
        def _():
            for h in range(HEADS):
                finf_ref[0, h] = sf_scr[h].T
                finb_ref[0, h] = sb_scr[h].T


def _scan_call(p, e, n_seq, seq_len, s0f=None, s0b=None, emit_final=False):
    tb = min(seq_len, 512)
    nt = seq_len // tb
    nck = tb // CHUNK
    n_ck = seq_len // CHUNK
    has_init = s0f is not None
    st_blk = (1, HEADS, D_HEAD, D_HEAD)

    in_specs = [
        pl.BlockSpec((tb, D_REC), lambda b, j: (b * nt + j, 1)),
        pl.BlockSpec((tb, D_REC), lambda b, j: (b * nt + j, 4)),
        pl.BlockSpec((1, nck, 2 * D_REC), lambda b, j: (b, j, 0)),
        pl.BlockSpec((tb, D_REC), lambda b, j: (b * nt + nt - 1 - j, 3)),
        pl.BlockSpec((tb, D_REC), lambda b, j: (b * nt + nt - 1 - j, 4)),
        pl.BlockSpec((1, nck, 2 * D_REC), lambda b, j: (b, nt - 1 - j, 1)),
    ]
    args = [p, p, e, p, p, e]
    if has_init:
        in_specs += [pl.BlockSpec(st_blk, lambda b, j: (b, 0, 0, 0))] * 2
        args += [s0f, s0b]
    sm_shape = jax.ShapeDtypeStruct((n_seq, n_ck, HEADS, D_HEAD, D_HEAD), BF16)
    sm_blk = (1, nck, HEADS, D_HEAD, D_HEAD)
    out_specs = [
        pl.BlockSpec(sm_blk, lambda b, j: (b, j, 0, 0, 0)),
        pl.BlockSpec(sm_blk, lambda b, j: (b, nt - 1 - j, 0, 0, 0)),
    ]
    out_shape = [sm_shape, sm_shape]
    if emit_final:
        out_specs += [pl.BlockSpec(st_blk, lambda b, j: (b, 0, 0, 0))] * 2
        out_shape += [jax.ShapeDtypeStruct((n_seq, HEADS, D_HEAD, D_HEAD), F32)] * 2

    return pl.pallas_call(
        functools.partial(_scan_kernel, nck=nck, has_init=has_init,
                          emit_final=emit_final),
        grid=(n_seq, nt),
        in_specs=in_specs,
        out_specs=out_specs,
        out_shape=out_shape,
        scratch_shapes=[pltpu.VMEM((HEADS, D_HEAD, D_HEAD), F32)] * 2,
        compiler_params=_params(2),
        name="scan",
    )(*args)


def _window(n, w):
    idx = np.arange(n)
    return np.clip(idx - w // 2, 0, n), np.clip(idx + w // 2, 0, n)


def _band(n, w):
    lo, hi = _window(n, w)
    j = np.arange(n)[None, :]
    return ((j >= lo[:, None]) & (j < hi[:, None])).astype(np.float32), (hi - lo)


def _pool_seq_kernel(u_ref, a_ref, inv_ref, pw_ref, ps_ref, o_ref, *, n_seq_blk, seq_len):
    for s in range(n_seq_blk):
        rows = slice(s * seq_len, (s + 1) * seq_len)
        for gi in range(len(POOL_WINDOWS)):
            cols = slice(gi * POOL_GW, (gi + 1) * POOL_GW)
            ug = u_ref[rows, cols]
            tot = jnp.dot(a_ref[gi], ug, preferred_element_type=F32)
            diff = (tot * inv_ref[gi] - ug.astype(F32)).astype(BF16)
            y = jnp.dot(diff, pw_ref[gi], preferred_element_type=F32)
            o_ref[rows, cols] = (y * ps_ref[:, cols]).astype(BF16)


def _pool_grid_kernel(u_ref, a_ref, invc_ref, pw_ref, ps_ref, o_ref, col_scr, *,
                      n_rows):
    blk = a_ref.shape[1]
    rpb = blk // GRID_W
    for gi, w in enumerate(POOL_WINDOWS):
        cols = slice(gi * POOL_GW, (gi + 1) * POOL_GW)
        for rb in range(n_rows // rpb):
            rows = slice(rb * blk, (rb + 1) * blk)
            tot = jnp.dot(a_ref[gi], u_ref[rows, cols], preferred_element_type=F32)
            col_scr[rows, :] = tot * invc_ref[gi]
        lo, hi = _window(n_rows, w)
        for r in range(n_rows):
            acc = col_scr[int(lo[r]) * GRID_W:(int(lo[r]) + 1) * GRID_W, :]
            for rr in range(int(lo[r]) + 1, int(hi[r])):
                acc = acc + col_scr[rr * GRID_W:(rr + 1) * GRID_W, :]
            rows = slice(r * GRID_W, (r + 1) * GRID_W)
            mean = acc * (1.0 / float(hi[r] - lo[r]))
            diff = (mean - u_ref[rows, cols].astype(F32)).astype(BF16)
            y = jnp.dot(diff, pw_ref[gi], preferred_element_type=F32)
            o_ref[rows, cols] = (y * ps_ref[:, cols]).astype(BF16)


def _pool_call(p, pool_w, pool_scale, n_seq, seq_len, two_d):
    n_tok = p.shape[0]
    ng = len(POOL_WINDOWS)
    if two_d:
        n_rows = seq_len // GRID_W
        rpb = 4
        blk = rpb * GRID_W
        bands = [_band(GRID_W, w) for w in POOL_WINDOWS]
        a = np.stack([np.kron(np.eye(rpb, dtype=np.float32), b[0]) for b in bands])
        inv = np.stack([np.broadcast_to(np.tile(1.0 / b[1], rpb)[:, None], (blk, POOL_GW))
                        for b in bands]).astype(np.float32)
        kern = functools.partial(_pool_grid_kernel, n_rows=n_rows)
        tb = seq_len
        scratch = [pltpu.VMEM((seq_len, POOL_GW), F32)]
    else:
        blk = seq_len
        bands = [_band(seq_len, w) for w in POOL_WINDOWS]
        a = np.stack([b[0] for b in bands])
        inv = np.stack([np.broadcast_to((1.0 / b[1])[:, None], (blk, POOL_GW))
                        for b in bands]).astype(np.float32)
        n_seq_blk = 8
        kern = functools.partial(_pool_seq_kernel, n_seq_blk=n_seq_blk, seq_len=seq_len)
        tb = n_seq_blk * seq_len
        scratch = []
    return pl.pallas_call(
        kern,
        grid=(n_tok // tb,),
        in_specs=[
            pl.BlockSpec((tb, D_POOL), lambda i: (i, 6)),
            _const_spec((ng, blk, blk)),
            _const_spec((ng, blk, POOL_GW)),
            _const_spec((ng, POOL_GW, POOL_GW)),
            _const_spec((1, D_POOL)),
        ],
        out_specs=pl.BlockSpec((tb, D_POOL), lambda i: (i, 0)),
        out_shape=jax.ShapeDtypeStruct((n_tok, D_POOL), BF16),
        scratch_shapes=scratch,
        compiler_params=_params(1),
        name="pool",
    )(p, jnp.asarray(a, BF16), jnp.asarray(inv, F32), pool_w, pool_scale)


def _layer_norm(u, g, b):
    mu = jnp.mean(u, axis=-1, keepdims=True)
    d = u - mu
    var = jnp.mean(d * d, axis=-1, keepdims=True)
    return d * lax.rsqrt(var + LN_EPS) * g + b


def _tail_kernel(x_ref, p_ref, smf_ref, smb_ref, yp_ref, g1_ref, sh2_ref, sc2_ref,
                 g2_ref, gn_ref, wo_ref, l1g_ref, l1b_ref, wgu_ref, wd_ref,
                 l2g_ref, l2b_ref, o_ref, yr_scr, u1_scr, x1_scr, h2_scr, act_scr,
                 u2_scr, *, cps):
    tm = x_ref.shape[0]
    ri = lax.broadcasted_iota(jnp.int32, (CHUNK, CHUNK), 0)
    ci = lax.broadcasted_iota(jnp.int32, (CHUNK, CHUNK), 1)
    lower = ci <= ri
    upper = ci >= ri
    nt_dims = (((1,), (1,)), ((), ()))
    gn = gn_ref[...]

    @pl.when(pl.program_id(0) == 0)
    def _():
        x1_scr[...] = jnp.zeros_like(x1_scr)
        h2_scr[...] = jnp.zeros_like(h2_scr)
        u2_scr[...] = jnp.zeros_like(u2_scr)

    def rec_piece(c, h):
        rows = slice(c * CHUNK, (c + 1) * CHUNK)
        s_i, c_i = c // cps, c % cps

        def col(k):
            return slice(k * D_REC + h * D_HEAD, k * D_REC + (h + 1) * D_HEAD)
        qd_f = p_ref[rows, col(0)]
        ki_f = p_ref[rows, col(1)]
        qd_b = p_ref[rows, col(2)]
        ki_b = p_ref[rows, col(3)]
        v = p_ref[rows, col(4)]
        gs = p_ref[rows, col(5)]
        sc_f = lax.dot_general(qd_f, ki_f, nt_dims, preferred_element_type=F32)
        sc_b = lax.dot_general(qd_b, ki_b, nt_dims, preferred_element_type=F32)
        sc = (jnp.where(lower, sc_f, 0.0) + jnp.where(upper, sc_b, 0.0)).astype(BF16)
        o = jnp.dot(sc, v, preferred_element_type=F32)
        o = o + lax.dot_general(qd_f, smf_ref[s_i, c_i, h], nt_dims,
                                preferred_element_type=F32)
        o = o + lax.dot_general(qd_b, smb_ref[s_i, c_i, h], nt_dims,
                                preferred_element_type=F32)
        ms = jnp.mean(o * o, axis=-1, keepdims=True)
        on = o * lax.rsqrt(ms + RMS_EPS) * gn
        yr_scr[rows, h * D_HEAD:(h + 1) * D_HEAD] = (on * gs.astype(F32)).astype(BF16)

    def outproj_piece(n):
        cs = slice(n * SLICE, (n + 1) * SLICE)
        y = jnp.dot(yr_scr[...], wo_ref[0:D_REC, cs], preferred_element_type=F32)
        y = y + jnp.dot(yp_ref[...], wo_ref[D_REC:2 * D_REC, cs],
                        preferred_element_type=F32)
        u1_scr[:, cs] = ALPHA * x_ref[:, cs] + g1_ref[0][:, cs] * y

    def norm1_piece(m):
        rows = slice(m * ROWS, (m + 1) * ROWS)
        x1 = _layer_norm(u1_scr[rows, :], l1g_ref[...], l1b_ref[...])
        x1_scr[rows, :] = x1
        h2_scr[rows, :] = (x1 * (1.0 + sc2_ref[0]) + sh2_ref[0]).astype(BF16)

    def gate_piece(k):
        cs = slice(k * SLICE, (k + 1) * SLICE)
        ab = jnp.dot(h2_scr[...], wgu_ref[:, 2 * k * SLICE:2 * (k + 1) * SLICE],
                     preferred_element_type=F32)
        a = ab[:, 0:SLICE]
        b = ab[:, SLICE:2 * SLICE]
        act_scr[:, cs] = (a * jax.nn.sigmoid(a) * b).astype(BF16)

    def resid_piece(m):
        rows = slice(m * ROWS, (m + 1) * ROWS)
        u2_scr[rows, :] = ALPHA * x1_scr[rows, :]

    def down_piece(n):
        cs = slice(n * SLICE, (n + 1) * SLICE)
        f = jnp.dot(act_scr[...], wd_ref[:, cs], preferred_element_type=F32)
        u2_scr[:, cs] = u2_scr[:, cs] + g2_ref[0][:, cs] * f

    def norm2_piece(m):
        rows = slice(m * ROWS, (m + 1) * ROWS)
        o_ref[rows, :] = _layer_norm(u2_scr[rows, :], l2g_ref[...], l2b_ref[...])

    n_rb = tm // ROWS
    mixer = [functools.partial(rec_piece, c, h)
             for c in range(tm // CHUNK) for h in range(HEADS)]
    mixer += [functools.partial(outproj_piece, n) for n in range(D_MODEL // SLICE)]
    vec = []
    for m in range(n_rb):
        vec += [functools.partial(norm2_piece, m), functools.partial(resid_piece, m)]
    n_gate = D_FF // SLICE
    for k in range(n_gate):
        gate_piece(k)
        for piece in mixer[k * len(mixer) // n_gate:(k + 1) * len(mixer) // n_gate]:
            piece()
        for piece in vec[k * len(vec) // n_gate:(k + 1) * len(vec) // n_gate]:
            piece()
    n_down = D_MODEL // SLICE
    for n in range(n_down):
        down_piece(n)
        for m in range(n * n_rb // n_down, (n + 1) * n_rb // n_down):
            norm1_piece(m)


def _tail_call(x2, p, smf, smb, yp, mod3, mod_row, gn, wo, l1g, l1b, wgu, wd,
               l2g, l2b, seq_len):
    n_tok = x2.shape[0]
    tm = min(TM_TAIL, seq_len)
    n_tiles = n_tok // tm
    cps = tm // CHUNK
    tps = seq_len // tm
    sm_blk = (1, cps, HEADS, D_HEAD, D_HEAD)

    def mix_tile(s):
        return jnp.minimum(s, n_tiles - 1)

    def ffn_tile(s):
        return jnp.clip(s - 1, 0, n_tiles - 1)

    def out_tile(s):
        return jnp.clip(s - 2, 0, n_tiles - 1)

    def mod_spec(col, tile):
        return pl.BlockSpec((1, 1, D_MODEL), lambda s: (mod_row(tile(s), tps), 0, col))

    sm_spec = pl.BlockSpec(
        sm_blk, lambda s: (mix_tile(s) // tps, mix_tile(s) % tps, 0, 0, 0))
    return pl.pallas_call(
        functools.partial(_tail_kernel, cps=cps),
        grid=(n_tiles + 2,),
        in_specs=[
            pl.BlockSpec((tm, D_MODEL), lambda s: (mix_tile(s), 0)),
            pl.BlockSpec((tm, 6 * D_REC), lambda s: (mix_tile(s), 0)),
            sm_spec, sm_spec,
            pl.BlockSpec((tm, D_POOL), lambda s: (mix_tile(s), 0)),
            mod_spec(2, mix_tile), mod_spec(3, mix_tile), mod_spec(4, mix_tile),
            mod_spec(5, ffn_tile),
            _const_spec((1, D_HEAD)),
            _const_spec((D_MODEL, D_MODEL)),
            _const_spec((1, D_MODEL)), _const_spec((1, D_MODEL)),
            _const_spec((D_MODEL, 2 * D_FF)),
            _const_spec((D_FF, D_MODEL)),
            _const_spec((1, D_MODEL)), _const_spec((1, D_MODEL)),
        ],
        out_specs=pl.BlockSpec((tm, D_MODEL), lambda s: (out_tile(s), 0)),
        out_shape=jax.ShapeDtypeStruct((n_tok, D_MODEL), F32),
        scratch_shapes=[
            pltpu.VMEM((tm, D_REC), BF16),
            pltpu.VMEM((tm, D_MODEL), F32),
            pltpu.VMEM((tm, D_MODEL), F32),
            pltpu.VMEM((tm, D_MODEL), BF16),
            pltpu.VMEM((tm, D_FF), BF16),
            pltpu.VMEM((tm, D_MODEL), F32),
        ],
        compiler_params=_params(1),
        name="tail",
    )(x2, p, smf, smb, yp, mod3, mod3, mod3, mod3, gn, wo, l1g, l1b, wgu, wd,
      l2g, l2b)


def kernel(x_prompt, x_sample, state_hgrn_fwd, state_hgrn_bwd, c, c_ctx, w_mod, b_mod,
           w_in, lb_fwd_raw, lb_bwd_raw, hgrn_norm_g, pool_w, pool_scale, w_out, ln1_g,
           ln1_b, w_ffn_gate, w_ffn_up, w_ffn_down, ln2_g, ln2_b):
    n_ctx, t_ctx, _ = x_prompt.shape
    n_lat, t_lat, _ = x_sample.shape
    ctx_row = n_lat

    c_all = jnp.concatenate(
        [c, c_ctx[None, :], jnp.zeros((16 - n_lat - 1, D_MODEL), F32)], axis=0)
    mod = _mod_call(c_all, w_mod[0], b_mod)
    mod3 = mod.reshape(16, 1, 6 * D_MODEL)

    w_in_b = w_in[0].astype(BF16)
    wo_b = w_out[0].astype(BF16)
    n_sl = D_FF // SLICE
    wgu_b = jnp.concatenate(
        [w_ffn_gate[0].astype(BF16).reshape(D_MODEL, n_sl, SLICE),
         w_ffn_up[0].astype(BF16).reshape(D_MODEL, n_sl, SLICE)],
        axis=2).reshape(D_MODEL, 2 * D_FF)
    wd_b = w_ffn_down[0].astype(BF16)
    pw_b = pool_w[0].astype(BF16)

    def run(x, n_seq, seq_len, mod_row, two_d, s0f, s0b, emit_final):
        x2 = x.reshape(n_seq * seq_len, D_MODEL)
        p, e = _proj_call(x2, mod3, mod_row, w_in_b, lb_fwd_raw, lb_bwd_raw, n_seq, seq_len)
        scan = _scan_call(p, e, n_seq, seq_len, s0f, s0b, emit_final)
        yp = _pool_call(p, pw_b, pool_scale, n_seq, seq_len, two_d)
        y = _tail_call(x2, p, scan[0], scan[1], yp, mod3, mod_row, hgrn_norm_g, wo_b,
                       ln1_g, ln1_b, wgu_b, wd_b, ln2_g, ln2_b, seq_len)
        return y.reshape(n_seq, seq_len, D_MODEL), scan[2:]

    y_ctx, fin = run(x_prompt, n_ctx, t_ctx, lambda i, tps: ctx_row, False,
                     None, None, True)
    y_lat, _ = run(x_sample, n_lat, t_lat, lambda i, tps: i // tps, True,
                   state_hgrn_fwd[:, 0], state_hgrn_bwd[:, 0], False)
    return (y_ctx, y_lat, fin[0][:, None], fin[1][:, None])
```

```python
import functools

import numpy as np
import jax
import jax.numpy as jnp
from jax import lax
from jax.experimental import pallas as pl
from jax.experimental.pallas import tpu as pltpu

F32 = jnp.float32
BF16 = jnp.bfloat16

D_MODEL = 1024
D_REC = 512
D_POOL = 512
HEADS = 4
D_HEAD = 128
D_IN = 5 * D_REC + D_POOL
D_FF = 2816
GRID_W = 64
POOL_WINDOWS = (2, 4, 8, 16)
POOL_GW = 128
ALPHA = 2.0 ** 0.25
LN_EPS = 1e-5
RMS_EPS = 1e-6

CHUNK = 64
MID_F = CHUNK // 2 - 1
MID_B = CHUNK // 2
TM_PROJ = 512
TM_TAIL = 256
N_FF_SPLIT = 2
P_COLS = 7 * D_REC
E_COLS = 4 * D_REC
VMEM_LIMIT = 56 * 1024 * 1024

_ARB = "arbitrary"


def _params(n_axes):
    return pltpu.CompilerParams(
        dimension_semantics=(_ARB,) * n_axes, vmem_limit_bytes=VMEM_LIMIT)


def _const_spec(shape):
    nd = len(shape)
    return pl.BlockSpec(shape, lambda *_: (0,) * nd, pipeline_mode=pl.Buffered(1))


def _mod_kernel(c_ref, w_ref, b_ref, o_ref):
    c = c_ref[...]
    a = c * jax.nn.sigmoid(c)
    o_ref[...] = jnp.dot(a, w_ref[...], precision=lax.Precision.HIGHEST,
                         preferred_element_type=F32) + b_ref[...]


def _mod_call(c_all, w_mod, b_mod):
    rows = c_all.shape[0]
    n_out = w_mod.shape[1]
    blk = 1024
    return pl.pallas_call(
        _mod_kernel,
        grid=(n_out // blk,),
        in_specs=[
            pl.BlockSpec((rows, D_MODEL), lambda j: (0, 0)),
            pl.BlockSpec((D_MODEL, blk), lambda j: (0, j)),
            pl.BlockSpec((1, blk), lambda j: (0, j)),
        ],
        out_specs=pl.BlockSpec((rows, blk), lambda j: (0, j)),
        out_shape=jax.ShapeDtypeStruct((rows, n_out), F32),
        compiler_params=_params(1),
        name="mod",
    )(c_all, w_mod, b_mod)


def _lower_bound(raw_ref):
    r0 = raw_ref[0:1, :]
    r1 = raw_ref[1:2, :]
    mx = jnp.maximum(r0, r1)
    e0 = jnp.exp(r0 - mx)
    e1 = jnp.exp(r1 - mx)
    return e0 / (e0 + e1)


def _split_bf16(a):
    hi = a.astype(BF16)
    lo = (a - hi.astype(F32)).astype(BF16)
    return hi, lo


def _proj_kernel(x_ref, sh_ref, sc_ref, w_ref, lbf_ref, lbb_ref, p_ref, e_ref,
                 proj_scr, *, cps):
    tm = x_ref.shape[0]
    h = (x_ref[...] * (1.0 + sc_ref[0]) + sh_ref[0]).astype(BF16)
    proj_scr[...] = jnp.dot(h, w_ref[...], preferred_element_type=F32)

    lb_f = _lower_bound(lbf_ref)
    lb_b = _lower_bound(lbb_ref)
    ri = lax.broadcasted_iota(jnp.int32, (CHUNK, CHUNK), 0)
    ci = lax.broadcasted_iota(jnp.int32, (CHUNK, CHUNK), 1)
    tri_pre = jnp.where(ci <= ri, 1.0, 0.0).astype(BF16)
    tri_suf = jnp.where(ci >= ri, 1.0, 0.0).astype(BF16)

    def chunk_sum(tri, a):
        hi, lo = _split_bf16(a)
        return (jnp.dot(tri, hi, preferred_element_type=F32)
                + jnp.dot(tri, lo, preferred_element_type=F32))

    for c in range(tm // CHUNK):
        rows = slice(c * CHUNK, (c + 1) * CHUNK)
        q = proj_scr[rows, 0:D_REC]
        qs = q * jax.nn.sigmoid(q)

        def direction(z, lb, tri, mid, end):
            f = lb + (1.0 - lb) * jax.nn.sigmoid(z)
            k = 1.0 - f
            b = chunk_sum(tri, jnp.log(f))
            m = b[mid:mid + 1, :]
            d = b - m
            qd = qs * jnp.exp(d)
            ki = k * jnp.exp(-d)
            em = jnp.exp(m)
            r = jnp.exp(b[end:end + 1, :] - m)
            return qd, ki, em, r

        qd_f, ki_f, em_f, r_f = direction(
            proj_scr[rows, D_REC:2 * D_REC], lb_f, tri_pre, MID_F, CHUNK - 1)
        qd_b, ki_b, em_b, r_b = direction(
            proj_scr[rows, 2 * D_REC:3 * D_REC], lb_b, tri_suf, MID_B, 0)
        g = proj_scr[rows, 4 * D_REC:5 * D_REC]

        p_ref[rows, 0 * D_REC:1 * D_REC] = qd_f.astype(BF16)
        p_ref[rows, 1 * D_REC:2 * D_REC] = ki_f.astype(BF16)
        p_ref[rows, 2 * D_REC:3 * D_REC] = qd_b.astype(BF16)
        p_ref[rows, 3 * D_REC:4 * D_REC] = ki_b.astype(BF16)
        p_ref[rows, 4 * D_REC:5 * D_REC] = proj_scr[rows, 3 * D_REC:4 * D_REC].astype(BF16)
        p_ref[rows, 5 * D_REC:6 * D_REC] = (g * jax.nn.sigmoid(g)).astype(BF16)
        p_ref[rows, 6 * D_REC:7 * D_REC] = proj_scr[rows, 5 * D_REC:6 * D_REC].astype(BF16)

        s, cc = c // cps, c % cps
        e_ref[s, cc:cc + 1, 0 * D_REC:1 * D_REC] = em_f
        e_ref[s, cc:cc + 1, 1 * D_REC:2 * D_REC] = r_f
        e_ref[s, cc:cc + 1, 2 * D_REC:3 * D_REC] = em_b
        e_ref[s, cc:cc + 1, 3 * D_REC:4 * D_REC] = r_b


def _proj_call(x2, mod3, mod_row, w_in, lbf_raw, lbb_raw, n_seq, seq_len):
    n_tok = x2.shape[0]
    tm = TM_PROJ
    cpt = tm // CHUNK
    cps = min(cpt, seq_len // CHUNK)
    spt = cpt // cps
    tps = max(1, seq_len // tm)
    n_ck = seq_len // CHUNK

    if spt > 1:
        e_spec = pl.BlockSpec((spt, cps, E_COLS), lambda i: (i, 0, 0))
    else:
        e_spec = pl.BlockSpec((1, cps, E_COLS), lambda i: (i // tps, i % tps, 0))

    return pl.pallas_call(
        functools.partial(_proj_kernel, cps=cps),
        grid=(n_tok // tm,),
        in_specs=[
            pl.BlockSpec((tm, D_MODEL), lambda i: (i, 0)),
            pl.BlockSpec((1, 1, D_MODEL), lambda i: (mod_row(i, tps), 0, 0)),
            pl.BlockSpec((1, 1, D_MODEL), lambda i: (mod_row(i, tps), 0, 1)),
            _const_spec((D_MODEL, D_IN)),
            _const_spec((2, D_REC)),
            _const_spec((2, D_REC)),
        ],
        out_specs=[
            pl.BlockSpec((tm, P_COLS), lambda i: (i, 0)),
            e_spec,
        ],
        out_shape=[
            jax.ShapeDtypeStruct((n_tok, P_COLS), BF16),
            jax.ShapeDtypeStruct((n_seq, n_ck, E_COLS), F32),
        ],
        scratch_shapes=[pltpu.VMEM((tm, D_IN), F32)],
        compiler_params=_params(1),
        name="proj",
    )(x2, mod3, mod3, w_in, lbf_raw, lbb_raw)


def _scan_kernel(*refs, nck, has_init, emit_final):
    it = iter(refs)
    kif_ref, vf_ref, ef_ref, kib_ref, vb_ref, eb_ref = (next(it) for _ in range(6))
    if has_init:
        s0f_ref, s0b_ref = next(it), next(it)
    smf_ref, smb_ref = next(it), next(it)
    if emit_final:
        finf_ref, finb_ref = next(it), next(it)
    sf_scr, sb_scr = next(it), next(it)

    j = pl.program_id(1)

    @pl.when(j == 0)
    def _():
        for h in range(HEADS):
            if has_init:
                sf_scr[h] = s0f_ref[0, h].T
                sb_scr[h] = s0b_ref[0, h].T
            else:
                sf_scr[h] = jnp.zeros((D_HEAD, D_HEAD), F32)
                sb_scr[h] = jnp.zeros((D_HEAD, D_HEAD), F32)

    def step(ki_ref, v_ref, e_ref, sm_ref, s_scr, c):
        rows = slice(c * CHUNK, (c + 1) * CHUNK)
        for h in range(HEADS):
            cols = slice(h * D_HEAD, (h + 1) * D_HEAD)
            em = e_ref[0, c:c + 1, h * D_HEAD:(h + 1) * D_HEAD]
            r = e_ref[0, c:c + 1, D_REC + h * D_HEAD:D_REC + (h + 1) * D_HEAD]
            s = s_scr[h]
            sm_ref[0, c, h] = (s * em).astype(BF16).T
            u_t = lax.dot_general(v_ref[rows, cols], ki_ref[rows, cols],
                                  (((0,), (0,)), ((), ())),
                                  preferred_element_type=F32)
            s_scr[h] = s * (em * r) + u_t * r

    for c in range(nck):
        step(kif_ref, vf_ref, ef_ref, smf_ref, sf_scr, c)
        step(kib_ref, vb_ref, eb_ref, smb_ref, sb_scr, nck - 1 - c)

    if emit_final:
        @pl.when(j == pl.num_programs(1) - 1)
        def _():
            for h in range(HEADS):
                finf_ref[0, h] = sf_scr[h].T
                finb_ref[0, h] = sb_scr[h].T


def _scan_call(p, e, n_seq, seq_len, s0f=None, s0b=None, emit_final=False):
    tb = min(seq_len, 512)
    nt = seq_len // tb
    nck = tb // CHUNK
    n_ck = seq_len // CHUNK
    has_init = s0f is not None
    st_blk = (1, HEADS, D_HEAD, D_HEAD)

    in_specs = [
        pl.BlockSpec((tb, D_REC), lambda b, j: (b * nt + j, 1)),
        pl.BlockSpec((tb, D_REC), lambda b, j: (b * nt + j, 4)),
        pl.BlockSpec((1, nck, 2 * D_REC), lambda b, j: (b, j, 0)),
        pl.BlockSpec((tb, D_REC), lambda b, j: (b * nt + nt - 1 - j, 3)),
        pl.BlockSpec((tb, D_REC), lambda b, j: (b * nt + nt - 1 - j, 4)),
        pl.BlockSpec((1, nck, 2 * D_REC), lambda b, j: (b, nt - 1 - j, 1)),
    ]
    args = [p, p, e, p, p, e]
    if has_init:
        in_specs += [pl.BlockSpec(st_blk, lambda b, j: (b, 0, 0, 0))] * 2
        args += [s0f, s0b]
    sm_shape = jax.ShapeDtypeStruct((n_seq, n_ck, HEADS, D_HEAD, D_HEAD), BF16)
    sm_blk = (1, nck, HEADS, D_HEAD, D_HEAD)
    out_specs = [
        pl.BlockSpec(sm_blk, lambda b, j: (b, j, 0, 0, 0)),
        pl.BlockSpec(sm_blk, lambda b, j: (b, nt - 1 - j, 0, 0, 0)),
    ]
    out_shape = [sm_shape, sm_shape]
    if emit_final:
        out_specs += [pl.BlockSpec(st_blk, lambda b, j: (b, 0, 0, 0))] * 2
        out_shape += [jax.ShapeDtypeStruct((n_seq, HEADS, D_HEAD, D_HEAD), F32)] * 2

    return pl.pallas_call(
        functools.partial(_scan_kernel, nck=nck, has_init=has_init,
                          emit_final=emit_final),
        grid=(n_seq, nt),
        in_specs=in_specs,
        out_specs=out_specs,
        out_shape=out_shape,
        scratch_shapes=[pltpu.VMEM((HEADS, D_HEAD, D_HEAD), F32)] * 2,
        compiler_params=_params(2),
        name="scan",
    )(*args)


def _window(n, w):
    idx = np.arange(n)
    return np.clip(idx - w // 2, 0, n), np.clip(idx + w // 2, 0, n)


def _band(n, w):
    lo, hi = _window(n, w)
    j = np.arange(n)[None, :]
    return ((j >= lo[:, None]) & (j < hi[:, None])).astype(np.float32), (hi - lo)


def _pool_seq_kernel(u_ref, a_ref, inv_ref, pw_ref, ps_ref, o_ref, *, n_seq_blk, seq_len):
    for s in range(n_seq_blk):
        rows = slice(s * seq_len, (s + 1) * seq_len)
        for gi in range(len(POOL_WINDOWS)):
            cols = slice(gi * POOL_GW, (gi + 1) * POOL_GW)
            ug = u_ref[rows, cols]
            tot = jnp.dot(a_ref[gi], ug, preferred_element_type=F32)
            diff = (tot * inv_ref[gi] - ug.astype(F32)).astype(BF16)
            y = jnp.dot(diff, pw_ref[gi], preferred_element_type=F32)
            o_ref[rows, cols] = (y * ps_ref[:, cols]).astype(BF16)


def _pool_grid_kernel(u_ref, a_ref, invc_ref, pw_ref, ps_ref, o_ref, col_scr, *,
                      n_rows):
    blk = a_ref.shape[1]
    rpb = blk // GRID_W
    for gi, w in enumerate(POOL_WINDOWS):
        cols = slice(gi * POOL_GW, (gi + 1) * POOL_GW)
        for rb in range(n_rows // rpb):
            rows = slice(rb * blk, (rb + 1) * blk)
            tot = jnp.dot(a_ref[gi], u_ref[rows, cols], preferred_element_type=F32)
            col_scr[rows, :] = tot * invc_ref[gi]
        lo, hi = _window(n_rows, w)
        for r in range(n_rows):
            acc = col_scr[int(lo[r]) * GRID_W:(int(lo[r]) + 1) * GRID_W, :]
            for rr in range(int(lo[r]) + 1, int(hi[r])):
                acc = acc + col_scr[rr * GRID_W:(rr + 1) * GRID_W, :]
            rows = slice(r * GRID_W, (r + 1) * GRID_W)
            mean = acc * (1.0 / float(hi[r] - lo[r]))
            diff = (mean - u_ref[rows, cols].astype(F32)).astype(BF16)
            y = jnp.dot(diff, pw_ref[gi], preferred_element_type=F32)
            o_ref[rows, cols] = (y * ps_ref[:, cols]).astype(BF16)


def _pool_call(p, pool_w, pool_scale, n_seq, seq_len, two_d):
    n_tok = p.shape[0]
    ng = len(POOL_WINDOWS)
    if two_d:
        n_rows = seq_len // GRID_W
        rpb = 4
        blk = rpb * GRID_W
        bands = [_band(GRID_W, w) for w in POOL_WINDOWS]
        a = np.stack([np.kron(np.eye(rpb, dtype=np.float32), b[0]) for b in bands])
        inv = np.stack([np.broadcast_to(np.tile(1.0 / b[1], rpb)[:, None], (blk, POOL_GW))
                        for b in bands]).astype(np.float32)
        kern = functools.partial(_pool_grid_kernel, n_rows=n_rows)
        tb = seq_len
        scratch = [pltpu.VMEM((seq_len, POOL_GW), F32)]
    else:
        blk = seq_len
        bands = [_band(seq_len, w) for w in POOL_WINDOWS]
        a = np.stack([b[0] for b in bands])
        inv = np.stack([np.broadcast_to((1.0 / b[1])[:, None], (blk, POOL_GW))
                        for b in bands]).astype(np.float32)
        n_seq_blk = 8
        kern = functools.partial(_pool_seq_kernel, n_seq_blk=n_seq_blk, seq_len=seq_len)
        tb = n_seq_blk * seq_len
        scratch = []
    return pl.pallas_call(
        kern,
        grid=(n_tok // tb,),
        in_specs=[
            pl.BlockSpec((tb, D_POOL), lambda i: (i, 6)),
            _const_spec((ng, blk, blk)),
            _const_spec((ng, blk, POOL_GW)),
            _const_spec((ng, POOL_GW, POOL_GW)),
            _const_spec((1, D_POOL)),
        ],
        out_specs=pl.BlockSpec((tb, D_POOL), lambda i: (i, 0)),
        out_shape=jax.ShapeDtypeStruct((n_tok, D_POOL), BF16),
        scratch_shapes=scratch,
        compiler_params=_params(1),
        name="pool",
    )(p, jnp.asarray(a, BF16), jnp.asarray(inv, F32), pool_w, pool_scale)


def _layer_norm(u, g, b):
    mu = jnp.mean(u, axis=-1, keepdims=True)
    d = u - mu
    var = jnp.mean(d * d, axis=-1, keepdims=True)
    return d * lax.rsqrt(var + LN_EPS) * g + b


def _tail_kernel(x_ref, p_ref, smf_ref, smb_ref, yp_ref, g1_ref, sh2_ref, sc2_ref,
                 g2_ref, gn_ref, wo_ref, l1g_ref, l1b_ref, wg_ref, wu_ref, wd_ref,
                 l2g_ref, l2b_ref, o_ref, yr_scr, *, cps):
    tm = x_ref.shape[0]
    ri = lax.broadcasted_iota(jnp.int32, (CHUNK, CHUNK), 0)
    ci = lax.broadcasted_iota(jnp.int32, (CHUNK, CHUNK), 1)
    lower = ci <= ri
    upper = ci >= ri
    nt_dims = (((1,), (1,)), ((), ()))
    gn = gn_ref[...]

    pieces = [(c, h) for c in range(tm // CHUNK) for h in range(HEADS)]

    def rows_of(c):
        return slice(c * CHUNK, (c + 1) * CHUNK)

    def col(k, h):
        return slice(k * D_REC + h * D_HEAD, k * D_REC + (h + 1) * D_HEAD)

    scores = []
    for c, h in pieces:
        rows = rows_of(c)
        sc_f = lax.dot_general(p_ref[rows, col(0, h)], p_ref[rows, col(1, h)], nt_dims,
                               preferred_element_type=F32)
        sc_b = lax.dot_general(p_ref[rows, col(2, h)], p_ref[rows, col(3, h)], nt_dims,
                               preferred_element_type=F32)
        scores.append((jnp.where(lower, sc_f, 0.0)
                       + jnp.where(upper, sc_b, 0.0)).astype(BF16))
    outs = []
    for (c, h), sc in zip(pieces, scores):
        rows = rows_of(c)
        s_i, c_i = c // cps, c % cps
        s_cat = jnp.concatenate([smf_ref[s_i, c_i, h], smb_ref[s_i, c_i, h]], axis=0)
        q_cat = jnp.concatenate([p_ref[rows, col(0, h)], p_ref[rows, col(2, h)]], axis=1)
        o = jnp.dot(sc, p_ref[rows, col(4, h)], preferred_element_type=F32)
        outs.append(o + jnp.dot(q_cat, s_cat, preferred_element_type=F32))
    for (c, h), o in zip(pieces, outs):
        rows = rows_of(c)
        ms = jnp.mean(o * o, axis=-1, keepdims=True)
        on = o * lax.rsqrt(ms + RMS_EPS) * gn
        gs = p_ref[rows, col(5, h)]
        yr_scr[rows, h * D_HEAD:(h + 1) * D_HEAD] = (on * gs.astype(F32)).astype(BF16)

    y = jnp.dot(yr_scr[...], wo_ref[0:D_REC, :], preferred_element_type=F32)
    y = y + jnp.dot(yp_ref[...], wo_ref[D_REC:2 * D_REC, :], preferred_element_type=F32)
    x1 = _layer_norm(ALPHA * x_ref[...] + g1_ref[0] * y, l1g_ref[...], l1b_ref[...])
    h2 = (x1 * (1.0 + sc2_ref[0]) + sh2_ref[0]).astype(BF16)
    fs = D_FF // N_FF_SPLIT
    ffn = None
    for k in range(N_FF_SPLIT):
        a = jnp.dot(h2, wg_ref[:, k * fs:(k + 1) * fs], preferred_element_type=F32)
        b = jnp.dot(h2, wu_ref[:, k * fs:(k + 1) * fs], preferred_element_type=F32)
        act = (a * jax.nn.sigmoid(a) * b).astype(BF16)
        part = jnp.dot(act, wd_ref[k * fs:(k + 1) * fs, :], preferred_element_type=F32)
        ffn = part if ffn is None else ffn + part
    o_ref[...] = _layer_norm(ALPHA * x1 + g2_ref[0] * ffn, l2g_ref[...], l2b_ref[...])


def _tail_call(x2, p, smf, smb, yp, mod3, mod_row, gn, wo, l1g, l1b, wg, wu, wd,
               l2g, l2b, seq_len):
    n_tok = x2.shape[0]
    tm = min(TM_TAIL, seq_len)
    cps = tm // CHUNK
    tps = seq_len // tm
    sm_blk = (1, cps, HEADS, D_HEAD, D_HEAD)

    def mod_spec(col):
        return pl.BlockSpec((1, 1, D_MODEL), lambda i: (mod_row(i, tps), 0, col))

    sm_spec = pl.BlockSpec(sm_blk, lambda i: (i // tps, i % tps, 0, 0, 0))
    return pl.pallas_call(
        functools.partial(_tail_kernel, cps=cps),
        grid=(n_tok // tm,),
        in_specs=[
            pl.BlockSpec((tm, D_MODEL), lambda i: (i, 0)),
            pl.BlockSpec((tm, 6 * D_REC), lambda i: (i, 0)),
            sm_spec, sm_spec,
            pl.BlockSpec((tm, D_POOL), lambda i: (i, 0)),
            mod_spec(2), mod_spec(3), mod_spec(4), mod_spec(5),
            _const_spec((1, D_HEAD)),
            _const_spec((D_MODEL, D_MODEL)),
            _const_spec((1, D_MODEL)), _const_spec((1, D_MODEL)),
            _const_spec((D_MODEL, D_FF)), _const_spec((D_MODEL, D_FF)),
            _const_spec((D_FF, D_MODEL)),
            _const_spec((1, D_MODEL)), _const_spec((1, D_MODEL)),
        ],
        out_specs=pl.BlockSpec((tm, D_MODEL), lambda i: (i, 0)),
        out_shape=jax.ShapeDtypeStruct((n_tok, D_MODEL), F32),
        scratch_shapes=[pltpu.VMEM((tm, D_REC), BF16)],
        compiler_params=_params(1),
        name="tail",
    )(x2, p, smf, smb, yp, mod3, mod3, mod3, mod3, gn, wo, l1g, l1b, wg, wu, wd,
      l2g, l2b)


def kernel(x_prompt, x_sample, state_hgrn_fwd, state_hgrn_bwd, c, c_ctx, w_mod, b_mod,
           w_in, lb_fwd_raw, lb_bwd_raw, hgrn_norm_g, pool_w, pool_scale, w_out, ln1_g,
           ln1_b, w_ffn_gate, w_ffn_up, w_ffn_down, ln2_g, ln2_b):
    n_ctx, t_ctx, _ = x_prompt.shape
    n_lat, t_lat, _ = x_sample.shape
    ctx_row = n_lat

    c_all = jnp.concatenate(
        [c, c_ctx[None, :], jnp.zeros((16 - n_lat - 1, D_MODEL), F32)], axis=0)
    mod = _mod_call(c_all, w_mod[0], b_mod)
    mod3 = mod.reshape(16, 1, 6 * D_MODEL)

    w_in_b = w_in[0].astype(BF16)
    wo_b = w_out[0].astype(BF16)
    wg_b = w_ffn_gate[0].astype(BF16)
    wu_b = w_ffn_up[0].astype(BF16)
    wd_b = w_ffn_down[0].astype(BF16)
    pw_b = pool_w[0].astype(BF16)

    def run(x, n_seq, seq_len, mod_row, two_d, s0f, s0b, emit_final):
        x2 = x.reshape(n_seq * seq_len, D_MODEL)
        p, e = _proj_call(x2, mod3, mod_row, w_in_b, lb_fwd_raw, lb_bwd_raw, n_seq, seq_len)
        scan = _scan_call(p, e, n_seq, seq_len, s0f, s0b, emit_final)
        yp = _pool_call(p, pw_b, pool_scale, n_seq, seq_len, two_d)
        y = _tail_call(x2, p, scan[0], scan[1], yp, mod3, mod_row, hgrn_norm_g, wo_b,
                       ln1_g, ln1_b, wg_b, wu_b, wd_b, ln2_g, ln2_b, seq_len)
        return y.reshape(n_seq, seq_len, D_MODEL), scan[2:]

    y_ctx, fin = run(x_prompt, n_ctx, t_ctx, lambda i, tps: ctx_row, False,
                     None, None, True)
    y_lat, _ = run(x_sample, n_lat, t_lat, lambda i, tps: i // tps, True,
                   state_hgrn_fwd[:, 0], state_hgrn_bwd[:, 0], False)
    return (y_ctx, y_lat, fin[0][:, None], fin[1][:, None])
```

```python
import functools

import numpy as np
import jax
import jax.numpy as jnp
from jax import lax
from jax.experimental import pallas as pl
from jax.experimental.pallas import tpu as pltpu

F32 = jnp.float32
BF16 = jnp.bfloat16

D_MODEL = 1024
D_REC = 512
D_POOL = 512
HEADS = 4
D_HEAD = 128
D_IN = 5 * D_REC + D_POOL
D_FF = 2816
GRID_W = 64
POOL_WINDOWS = (2, 4, 8, 16)
POOL_GW = 128
ALPHA = 2.0 ** 0.25
LN_EPS = 1e-5
RMS_EPS = 1e-6

CHUNK = 64
MID_F = CHUNK // 2 - 1
MID_B = CHUNK // 2
TM_PROJ = 512
TM_TAIL = 512
ROWS = 64
N_FF_SPLIT = 2
P_COLS = 7 * D_REC
E_COLS = 4 * D_REC
VMEM_LIMIT = 56 * 1024 * 1024

_ARB = "arbitrary"


def _params(n_axes):
    return pltpu.CompilerParams(
        dimension_semantics=(_ARB,) * n_axes, vmem_limit_bytes=VMEM_LIMIT)


def _const_spec(shape):
    nd = len(shape)
    return pl.BlockSpec(shape, lambda *_: (0,) * nd, pipeline_mode=pl.Buffered(1))


def _mod_kernel(c_ref, w_ref, b_ref, o_ref):
    c = c_ref[...]
    a = c * jax.nn.sigmoid(c)
    o_ref[...] = jnp.dot(a, w_ref[...], precision=lax.Precision.HIGHEST,
                         preferred_element_type=F32) + b_ref[...]


def _mod_call(c_all, w_mod, b_mod):
    rows = c_all.shape[0]
    n_out = w_mod.shape[1]
    blk = 1024
    return pl.pallas_call(
        _mod_kernel,
        grid=(n_out // blk,),
        in_specs=[
            pl.BlockSpec((rows, D_MODEL), lambda j: (0, 0)),
            pl.BlockSpec((D_MODEL, blk), lambda j: (0, j)),
            pl.BlockSpec((1, blk), lambda j: (0, j)),
        ],
        out_specs=pl.BlockSpec((rows, blk), lambda j: (0, j)),
        out_shape=jax.ShapeDtypeStruct((rows, n_out), F32),
        compiler_params=_params(1),
        name="mod",
    )(c_all, w_mod, b_mod)


def _lower_bound(raw_ref):
    r0 = raw_ref[0:1, :]
    r1 = raw_ref[1:2, :]
    mx = jnp.maximum(r0, r1)
    e0 = jnp.exp(r0 - mx)
    e1 = jnp.exp(r1 - mx)
    return e0 / (e0 + e1)


def _split_bf16(a):
    hi = a.astype(BF16)
    lo = (a - hi.astype(F32)).astype(BF16)
    return hi, lo


def _proj_kernel(x_ref, sh_ref, sc_ref, w_ref, lbf_ref, lbb_ref, p_ref, e_ref,
                 proj_scr, *, cps):
    tm = x_ref.shape[0]
    h = (x_ref[...] * (1.0 + sc_ref[0]) + sh_ref[0]).astype(BF16)
    proj_scr[...] = jnp.dot(h, w_ref[...], preferred_element_type=F32)

    lb_f = _lower_bound(lbf_ref)
    lb_b = _lower_bound(lbb_ref)
    ri = lax.broadcasted_iota(jnp.int32, (CHUNK, CHUNK), 0)
    ci = lax.broadcasted_iota(jnp.int32, (CHUNK, CHUNK), 1)
    tri_pre = jnp.where(ci <= ri, 1.0, 0.0).astype(BF16)
    tri_suf = jnp.where(ci >= ri, 1.0, 0.0).astype(BF16)

    def chunk_sum(tri, a):
        hi, lo = _split_bf16(a)
        return (jnp.dot(tri, hi, preferred_element_type=F32)
                + jnp.dot(tri, lo, preferred_element_type=F32))

    for c in range(tm // CHUNK):
        rows = slice(c * CHUNK, (c + 1) * CHUNK)
        q = proj_scr[rows, 0:D_REC]
        qs = q * jax.nn.sigmoid(q)

        def direction(z, lb, tri, mid, end):
            f = lb + (1.0 - lb) * jax.nn.sigmoid(z)
            k = 1.0 - f
            b = chunk_sum(tri, jnp.log(f))
            m = b[mid:mid + 1, :]
            d = b - m
            qd = qs * jnp.exp(d)
            ki = k * jnp.exp(-d)
            em = jnp.exp(m)
            r = jnp.exp(b[end:end + 1, :] - m)
            return qd, ki, em, r

        qd_f, ki_f, em_f, r_f = direction(
            proj_scr[rows, D_REC:2 * D_REC], lb_f, tri_pre, MID_F, CHUNK - 1)
        qd_b, ki_b, em_b, r_b = direction(
            proj_scr[rows, 2 * D_REC:3 * D_REC], lb_b, tri_suf, MID_B, 0)
        g = proj_scr[rows, 4 * D_REC:5 * D_REC]

        p_ref[rows, 0 * D_REC:1 * D_REC] = qd_f.astype(BF16)
        p_ref[rows, 1 * D_REC:2 * D_REC] = ki_f.astype(BF16)
        p_ref[rows, 2 * D_REC:3 * D_REC] = qd_b.astype(BF16)
        p_ref[rows, 3 * D_REC:4 * D_REC] = ki_b.astype(BF16)
        p_ref[rows, 4 * D_REC:5 * D_REC] = proj_scr[rows, 3 * D_REC:4 * D_REC].astype(BF16)
        p_ref[rows, 5 * D_REC:6 * D_REC] = (g * jax.nn.sigmoid(g)).astype(BF16)
        p_ref[rows, 6 * D_REC:7 * D_REC] = proj_scr[rows, 5 * D_REC:6 * D_REC].astype(BF16)

        s, cc = c // cps, c % cps
        e_ref[s, cc:cc + 1, 0 * D_REC:1 * D_REC] = em_f
        e_ref[s, cc:cc + 1, 1 * D_REC:2 * D_REC] = r_f
        e_ref[s, cc:cc + 1, 2 * D_REC:3 * D_REC] = em_b
        e_ref[s, cc:cc + 1, 3 * D_REC:4 * D_REC] = r_b


def _proj_call(x2, mod3, mod_row, w_in, lbf_raw, lbb_raw, n_seq, seq_len):
    n_tok = x2.shape[0]
    tm = TM_PROJ
    cpt = tm // CHUNK
    cps = min(cpt, seq_len // CHUNK)
    spt = cpt // cps
    tps = max(1, seq_len // tm)
    n_ck = seq_len // CHUNK

    if spt > 1:
        e_spec = pl.BlockSpec((spt, cps, E_COLS), lambda i: (i, 0, 0))
    else:
        e_spec = pl.BlockSpec((1, cps, E_COLS), lambda i: (i // tps, i % tps, 0))

    return pl.pallas_call(
        functools.partial(_proj_kernel, cps=cps),
        grid=(n_tok // tm,),
        in_specs=[
            pl.BlockSpec((tm, D_MODEL), lambda i: (i, 0)),
            pl.BlockSpec((1, 1, D_MODEL), lambda i: (mod_row(i, tps), 0, 0)),
            pl.BlockSpec((1, 1, D_MODEL), lambda i: (mod_row(i, tps), 0, 1)),
            _const_spec((D_MODEL, D_IN)),
            _const_spec((2, D_REC)),
            _const_spec((2, D_REC)),
        ],
        out_specs=[
            pl.BlockSpec((tm, P_COLS), lambda i: (i, 0)),
            e_spec,
        ],
        out_shape=[
            jax.ShapeDtypeStruct((n_tok, P_COLS), BF16),
            jax.ShapeDtypeStruct((n_seq, n_ck, E_COLS), F32),
        ],
        scratch_shapes=[pltpu.VMEM((tm, D_IN), F32)],
        compiler_params=_params(1),
        name="proj",
    )(x2, mod3, mod3, w_in, lbf_raw, lbb_raw)


def _scan_kernel(*refs, nck, has_init, emit_final):
    it = iter(refs)
    kif_ref, vf_ref, ef_ref, kib_ref, vb_ref, eb_ref = (next(it) for _ in range(6))
    if has_init:
        s0f_ref, s0b_ref = next(it), next(it)
    smf_ref, smb_ref = next(it), next(it)
    if emit_final:
        finf_ref, finb_ref = next(it), next(it)
    sf_scr, sb_scr = next(it), next(it)

    j = pl.program_id(1)

    @pl.when(j == 0)
    def _():
        for h in range(HEADS):
            if has_init:
                sf_scr[h] = s0f_ref[0, h].T
                sb_scr[h] = s0b_ref[0, h].T
            else:
                sf_scr[h] = jnp.zeros((D_HEAD, D_HEAD), F32)
                sb_scr[h] = jnp.zeros((D_HEAD, D_HEAD), F32)

    def step(ki_ref, v_ref, e_ref, sm_ref, s_scr, c):
        rows = slice(c * CHUNK, (c + 1) * CHUNK)
        for h in range(HEADS):
            cols = slice(h * D_HEAD, (h + 1) * D_HEAD)
            em = e_ref[0, c:c + 1, h * D_HEAD:(h + 1) * D_HEAD]
            r = e_ref[0, c:c + 1, D_REC + h * D_HEAD:D_REC + (h + 1) * D_HEAD]
            s = s_scr[h]
            sm_ref[0, c, h] = (s * em).astype(BF16).T
            u_t = lax.dot_general(v_ref[rows, cols], ki_ref[rows, cols],
                                  (((0,), (0,)), ((), ())),
                                  preferred_element_type=F32)
            s_scr[h] = s * (em * r) + u_t * r

    for c in range(nck):
        step(kif_ref, vf_ref, ef_ref, smf_ref, sf_scr, c)
        step(kib_ref, vb_ref, eb_ref, smb_ref, sb_scr, nck - 1 - c)

    if emit_final:
        @pl.when(j == pl.num_programs(1) - 1)
        def _():
            for h in range(HEADS):
                finf_ref[0, h] = sf_scr[h].T
                finb_ref[0, h] = sb_scr[h].T


def _scan_call(p, e, n_seq, seq_len, s0f=None, s0b=None, emit_final=False):
    tb = min(seq_len, 512)
    nt = seq_len // tb
    nck = tb // CHUNK
    n_ck = seq_len // CHUNK
    has_init = s0f is not None
    st_blk = (1, HEADS, D_HEAD, D_HEAD)

    in_specs = [
        pl.BlockSpec((tb, D_REC), lambda b, j: (b * nt + j, 1)),
        pl.BlockSpec((tb, D_REC), lambda b, j: (b * nt + j, 4)),
        pl.BlockSpec((1, nck, 2 * D_REC), lambda b, j: (b, j, 0)),
        pl.BlockSpec((tb, D_REC), lambda b, j: (b * nt + nt - 1 - j, 3)),
        pl.BlockSpec((tb, D_REC), lambda b, j: (b * nt + nt - 1 - j, 4)),
        pl.BlockSpec((1, nck, 2 * D_REC), lambda b, j: (b, nt - 1 - j, 1)),
    ]
    args = [p, p, e, p, p, e]
    if has_init:
        in_specs += [pl.BlockSpec(st_blk, lambda b, j: (b, 0, 0, 0))] * 2
        args += [s0f, s0b]
    sm_shape = jax.ShapeDtypeStruct((n_seq, n_ck, HEADS, D_HEAD, D_HEAD), BF16)
    sm_blk = (1, nck, HEADS, D_HEAD, D_HEAD)
    out_specs = [
        pl.BlockSpec(sm_blk, lambda b, j: (b, j, 0, 0, 0)),
        pl.BlockSpec(sm_blk, lambda b, j: (b, nt - 1 - j, 0, 0, 0)),
    ]
    out_shape = [sm_shape, sm_shape]
    if emit_final:
        out_specs += [pl.BlockSpec(st_blk, lambda b, j: (b, 0, 0, 0))] * 2
        out_shape += [jax.ShapeDtypeStruct((n_seq, HEADS, D_HEAD, D_HEAD), F32)] * 2

    return pl.pallas_call(
        functools.partial(_scan_kernel, nck=nck, has_init=has_init,
                          emit_final=emit_final),
        grid=(n_seq, nt),
        in_specs=in_specs,
        out_specs=out_specs,
        out_shape=out_shape,
        scratch_shapes=[pltpu.VMEM((HEADS, D_HEAD, D_HEAD), F32)] * 2,
        compiler_params=_params(2),
        name="scan",
    )(*args)


def _window(n, w):
    idx = np.arange(n)
    return np.clip(idx - w // 2, 0, n), np.clip(idx + w // 2, 0, n)


def _band(n, w):
    lo, hi = _window(n, w)
    j = np.arange(n)[None, :]
    return ((j >= lo[:, None]) & (j < hi[:, None])).astype(np.float32), (hi - lo)


def _pool_seq_kernel(u_ref, a_ref, inv_ref, pw_ref, ps_ref, o_ref, *, n_seq_blk, seq_len):
    for s in range(n_seq_blk):
        rows = slice(s * seq_len, (s + 1) * seq_len)
        for gi in range(len(POOL_WINDOWS)):
            cols = slice(gi * POOL_GW, (gi + 1) * POOL_GW)
            ug = u_ref[rows, cols]
            tot = jnp.dot(a_ref[gi], ug, preferred_element_type=F32)
            diff = (tot * inv_ref[gi] - ug.astype(F32)).astype(BF16)
            y = jnp.dot(diff, pw_ref[gi], preferred_element_type=F32)
            o_ref[rows, cols] = (y * ps_ref[:, cols]).astype(BF16)


def _pool_grid_kernel(u_ref, a_ref, invc_ref, pw_ref, ps_ref, o_ref, col_scr, *,
                      n_rows):
    blk = a_ref.shape[1]
    rpb = blk // GRID_W
    for gi, w in enumerate(POOL_WINDOWS):
        cols = slice(gi * POOL_GW, (gi + 1) * POOL_GW)
        for rb in range(n_rows // rpb):
            rows = slice(rb * blk, (rb + 1) * blk)
            tot = jnp.dot(a_ref[gi], u_ref[rows, cols], preferred_element_type=F32)
            col_scr[rows, :] = tot * invc_ref[gi]
        lo, hi = _window(n_rows, w)
        for r in range(n_rows):
            acc = col_scr[int(lo[r]) * GRID_W:(int(lo[r]) + 1) * GRID_W, :]
            for rr in range(int(lo[r]) + 1, int(hi[r])):
                acc = acc + col_scr[rr * GRID_W:(rr + 1) * GRID_W, :]
            rows = slice(r * GRID_W, (r + 1) * GRID_W)
            mean = acc * (1.0 / float(hi[r] - lo[r]))
            diff = (mean - u_ref[rows, cols].astype(F32)).astype(BF16)
            y = jnp.dot(diff, pw_ref[gi], preferred_element_type=F32)
            o_ref[rows, cols] = (y * ps_ref[:, cols]).astype(BF16)


def _pool_call(p, pool_w, pool_scale, n_seq, seq_len, two_d):
    n_tok = p.shape[0]
    ng = len(POOL_WINDOWS)
    if two_d:
        n_rows = seq_len // GRID_W
        rpb = 4
        blk = rpb * GRID_W
        bands = [_band(GRID_W, w) for w in POOL_WINDOWS]
        a = np.stack([np.kron(np.eye(rpb, dtype=np.float32), b[0]) for b in bands])
        inv = np.stack([np.broadcast_to(np.tile(1.0 / b[1], rpb)[:, None], (blk, POOL_GW))
                        for b in bands]).astype(np.float32)
        kern = functools.partial(_pool_grid_kernel, n_rows=n_rows)
        tb = seq_len
        scratch = [pltpu.VMEM((seq_len, POOL_GW), F32)]
    else:
        blk = seq_len
        bands = [_band(seq_len, w) for w in POOL_WINDOWS]
        a = np.stack([b[0] for b in bands])
        inv = np.stack([np.broadcast_to((1.0 / b[1])[:, None], (blk, POOL_GW))
                        for b in bands]).astype(np.float32)
        n_seq_blk = 8
        kern = functools.partial(_pool_seq_kernel, n_seq_blk=n_seq_blk, seq_len=seq_len)
        tb = n_seq_blk * seq_len
        scratch = []
    return pl.pallas_call(
        kern,
        grid=(n_tok // tb,),
        in_specs=[
            pl.BlockSpec((tb, D_POOL), lambda i: (i, 6)),
            _const_spec((ng, blk, blk)),
            _const_spec((ng, blk, POOL_GW)),
            _const_spec((ng, POOL_GW, POOL_GW)),
            _const_spec((1, D_POOL)),
        ],
        out_specs=pl.BlockSpec((tb, D_POOL), lambda i: (i, 0)),
        out_shape=jax.ShapeDtypeStruct((n_tok, D_POOL), BF16),
        scratch_shapes=scratch,
        compiler_params=_params(1),
        name="pool",
    )(p, jnp.asarray(a, BF16), jnp.asarray(inv, F32), pool_w, pool_scale)


def _layer_norm(u, g, b):
    mu = jnp.mean(u, axis=-1, keepdims=True)
    d = u - mu
    var = jnp.mean(d * d, axis=-1, keepdims=True)
    return d * lax.rsqrt(var + LN_EPS) * g + b


def _tail_kernel(x_ref, p_ref, smf_ref, smb_ref, yp_ref, g1_ref, sh2_ref, sc2_ref,
                 g2_ref, gn_ref, wo_ref, l1g_ref, l1b_ref, wg_ref, wu_ref, wd_ref,
                 l2g_ref, l2b_ref, o_ref, yr_scr, u_scr, h2_scr, *, cps):
    tm = x_ref.shape[0]
    ri = lax.broadcasted_iota(jnp.int32, (CHUNK, CHUNK), 0)
    ci = lax.broadcasted_iota(jnp.int32, (CHUNK, CHUNK), 1)
    lower = ci <= ri
    upper = ci >= ri
    nt_dims = (((1,), (1,)), ((), ()))
    gn = gn_ref[...]

    th = tm // 2
    half = [slice(0, th), slice(th, tm)]
    n_rb = th // ROWS
    fs = D_FF // N_FF_SPLIT

    def rows_of(c):
        return slice(c * CHUNK, (c + 1) * CHUNK)

    def col(k, h):
        return slice(k * D_REC + h * D_HEAD, k * D_REC + (h + 1) * D_HEAD)

    def mixer_half(i, fill):
        cph = th // CHUNK
        pieces = [(c, h) for c in range(i * cph, (i + 1) * cph) for h in range(HEADS)]
        scores = []
        for c, h in pieces:
            rows = rows_of(c)
            sc_f = lax.dot_general(p_ref[rows, col(0, h)], p_ref[rows, col(1, h)],
                                   nt_dims, preferred_element_type=F32)
            sc_b = lax.dot_general(p_ref[rows, col(2, h)], p_ref[rows, col(3, h)],
                                   nt_dims, preferred_element_type=F32)
            scores.append((jnp.where(lower, sc_f, 0.0)
                           + jnp.where(upper, sc_b, 0.0)).astype(BF16))
        fill(0)
        outs = []
        for (c, h), sc in zip(pieces, scores):
            rows = rows_of(c)
            s_i, c_i = c // cps, c % cps
            s_cat = jnp.concatenate([smf_ref[s_i, c_i, h], smb_ref[s_i, c_i, h]], axis=0)
            q_cat = jnp.concatenate([p_ref[rows, col(0, h)], p_ref[rows, col(2, h)]],
                                    axis=1)
            o = jnp.dot(sc, p_ref[rows, col(4, h)], preferred_element_type=F32)
            outs.append(o + jnp.dot(q_cat, s_cat, preferred_element_type=F32))
        fill(1)
        for (c, h), o in zip(pieces, outs):
            rows = rows_of(c)
            ms = jnp.mean(o * o, axis=-1, keepdims=True)
            on = o * lax.rsqrt(ms + RMS_EPS) * gn
            gs = p_ref[rows, col(5, h)]
            yr_scr[rows, h * D_HEAD:(h + 1) * D_HEAD] = (on * gs.astype(F32)).astype(BF16)

    def out_proj(hr):
        y = jnp.dot(yr_scr[hr, :], wo_ref[0:D_REC, :], preferred_element_type=F32)
        y = y + jnp.dot(yp_ref[hr, :], wo_ref[D_REC:2 * D_REC, :],
                        preferred_element_type=F32)
        return ALPHA * x_ref[hr, :] + g1_ref[0] * y

    def norm1_block(i, m):
        rows = slice(i * th + m * ROWS, i * th + (m + 1) * ROWS)
        x1 = _layer_norm(u_scr[rows, :], l1g_ref[...], l1b_ref[...])
        u_scr[rows, :] = x1
        h2_scr[rows, :] = (x1 * (1.0 + sc2_ref[0]) + sh2_ref[0]).astype(BF16)

    def norm2_block(i, m):
        rows = slice(i * th + m * ROWS, i * th + (m + 1) * ROWS)
        o_ref[rows, :] = _layer_norm(u_scr[rows, :], l2g_ref[...], l2b_ref[...])

    def ffn_half(i, fill):
        hr = half[i]
        h2 = h2_scr[hr, :]
        ffn = None
        j = 0
        for k in range(N_FF_SPLIT):
            a = jnp.dot(h2, wg_ref[:, k * fs:(k + 1) * fs], preferred_element_type=F32)
            fill(j); j += 1
            b = jnp.dot(h2, wu_ref[:, k * fs:(k + 1) * fs], preferred_element_type=F32)
            fill(j); j += 1
            act = (a * jax.nn.sigmoid(a) * b).astype(BF16)
            part = jnp.dot(act, wd_ref[k * fs:(k + 1) * fs, :],
                           preferred_element_type=F32)
            ffn = part if ffn is None else ffn + part
        u_scr[hr, :] = ALPHA * u_scr[hr, :] + g2_ref[0] * ffn

    def norm1_a(j):
        for m in range(j * n_rb // 2, (j + 1) * n_rb // 2):
            norm1_block(0, m)

    mixer_half(0, lambda j: None)
    u_scr[half[0], :] = out_proj(half[0])
    mixer_half(1, norm1_a)
    u_scr[half[1], :] = out_proj(half[1])
    ffn_half(0, lambda j: norm1_block(1, j) if j < n_rb else None)
    ffn_half(1, lambda j: norm2_block(0, j) if j < n_rb else None)
    for m in range(n_rb):
        norm2_block(1, m)


def _tail_call(x2, p, smf, smb, yp, mod3, mod_row, gn, wo, l1g, l1b, wg, wu, wd,
               l2g, l2b, seq_len):
    n_tok = x2.shape[0]
    tm = TM_TAIL
    cpt = tm // CHUNK
    cps = min(cpt, seq_len // CHUNK)
    spt = cpt // cps
    tps = max(1, seq_len // tm)
    sm_blk = (spt, cps, HEADS, D_HEAD, D_HEAD)

    def mod_spec(col):
        return pl.BlockSpec((1, 1, D_MODEL), lambda i: (mod_row(i, tps), 0, col))

    if spt > 1:
        sm_spec = pl.BlockSpec(sm_blk, lambda i: (i, 0, 0, 0, 0))
    else:
        sm_spec = pl.BlockSpec(sm_blk, lambda i: (i // tps, i % tps, 0, 0, 0))
    return pl.pallas_call(
        functools.partial(_tail_kernel, cps=cps),
        grid=(n_tok // tm,),
        in_specs=[
            pl.BlockSpec((tm, D_MODEL), lambda i: (i, 0)),
            pl.BlockSpec((tm, 6 * D_REC), lambda i: (i, 0)),
            sm_spec, sm_spec,
            pl.BlockSpec((tm, D_POOL), lambda i: (i, 0)),
            mod_spec(2), mod_spec(3), mod_spec(4), mod_spec(5),
            _const_spec((1, D_HEAD)),
            _const_spec((D_MODEL, D_MODEL)),
            _const_spec((1, D_MODEL)), _const_spec((1, D_MODEL)),
            _const_spec((D_MODEL, D_FF)), _const_spec((D_MODEL, D_FF)),
            _const_spec((D_FF, D_MODEL)),
            _const_spec((1, D_MODEL)), _const_spec((1, D_MODEL)),
        ],
        out_specs=pl.BlockSpec((tm, D_MODEL), lambda i: (i, 0)),
        out_shape=jax.ShapeDtypeStruct((n_tok, D_MODEL), F32),
        scratch_shapes=[
            pltpu.VMEM((tm, D_REC), BF16),
            pltpu.VMEM((tm, D_MODEL), F32),
            pltpu.VMEM((tm, D_MODEL), BF16),
        ],
        compiler_params=_params(1),
        name="tail",
    )(x2, p, smf, smb, yp, mod3, mod3, mod3, mod3, gn, wo, l1g, l1b, wg, wu, wd,
      l2g, l2b)


def kernel(x_prompt, x_sample, state_hgrn_fwd, state_hgrn_bwd, c, c_ctx, w_mod, b_mod,
           w_in, lb_fwd_raw, lb_bwd_raw, hgrn_norm_g, pool_w, pool_scale, w_out, ln1_g,
           ln1_b, w_ffn_gate, w_ffn_up, w_ffn_down, ln2_g, ln2_b):
    n_ctx, t_ctx, _ = x_prompt.shape
    n_lat, t_lat, _ = x_sample.shape
    ctx_row = n_lat

    c_all = jnp.concatenate(
        [c, c_ctx[None, :], jnp.zeros((16 - n_lat - 1, D_MODEL), F32)], axis=0)
    mod = _mod_call(c_all, w_mod[0], b_mod)
    mod3 = mod.reshape(16, 1, 6 * D_MODEL)

    w_in_b = w_in[0].astype(BF16)
    wo_b = w_out[0].astype(BF16)
    wg_b = w_ffn_gate[0].astype(BF16)
    wu_b = w_ffn_up[0].astype(BF16)
    wd_b = w_ffn_down[0].astype(BF16)
    pw_b = pool_w[0].astype(BF16)

    def run(x, n_seq, seq_len, mod_row, two_d, s0f, s0b, emit_final):
        x2 = x.reshape(n_seq * seq_len, D_MODEL)
        p, e = _proj_call(x2, mod3, mod_row, w_in_b, lb_fwd_raw, lb_bwd_raw, n_seq, seq_len)
        scan = _scan_call(p, e, n_seq, seq_len, s0f, s0b, emit_final)
        yp = _pool_call(p, pw_b, pool_scale, n_seq, seq_len, two_d)
        y = _tail_call(x2, p, scan[0], scan[1], yp, mod3, mod_row, hgrn_norm_g, wo_b,
                       ln1_g, ln1_b, wg_b, wu_b, wd_b, ln2_g, ln2_b, seq_len)
        return y.reshape(n_seq, seq_len, D_MODEL), scan[2:]

    y_ctx, fin = run(x_prompt, n_ctx, t_ctx, lambda i, tps: ctx_row, False,
                     None, None, True)
    y_lat, _ = run(x_sample, n_lat, t_lat, lambda i, tps: i // tps, True,
                   state_hgrn_fwd[:, 0], state_hgrn_bwd[:, 0], False)
    return (y_ctx, y_lat, fin[0][:, None], fin[1][:, None])
```

```python
import functools

import numpy as np
import jax
import jax.numpy as jnp
from jax import lax
from jax.experimental import pallas as pl
from jax.experimental.pallas import tpu as pltpu

F32 = jnp.float32
BF16 = jnp.bfloat16

D_MODEL = 1024
D_REC = 512
D_POOL = 512
HEADS = 4
D_HEAD = 128
D_IN = 5 * D_REC + D_POOL
D_FF = 2816
GRID_W = 64
POOL_WINDOWS = (2, 4, 8, 16)
POOL_GW = 128
ALPHA = 2.0 ** 0.25
LN_EPS = 1e-5
RMS_EPS = 1e-6

CHUNK = 64
MID_F = CHUNK // 2 - 1
MID_B = CHUNK // 2
TM_PROJ = 512
TM_TAIL = 512
ROWS = 64
N_FF_SPLIT = 2
P_COLS = 7 * D_REC
E_COLS = 4 * D_REC
VMEM_LIMIT = 56 * 1024 * 1024

_ARB = "arbitrary"


def _params(n_axes):
    return pltpu.CompilerParams(
        dimension_semantics=(_ARB,) * n_axes, vmem_limit_bytes=VMEM_LIMIT)


def _const_spec(shape):
    nd = len(shape)
    return pl.BlockSpec(shape, lambda *_: (0,) * nd, pipeline_mode=pl.Buffered(1))


def _mod_kernel(c_ref, w_ref, b_ref, o_ref):
    c = c_ref[...]
    a = c * jax.nn.sigmoid(c)
    o_ref[...] = jnp.dot(a, w_ref[...], precision=lax.Precision.HIGHEST,
                         preferred_element_type=F32) + b_ref[...]


def _mod_call(c_all, w_mod, b_mod):
    rows = c_all.shape[0]
    n_out = w_mod.shape[1]
    blk = 1024
    return pl.pallas_call(
        _mod_kernel,
        grid=(n_out // blk,),
        in_specs=[
            pl.BlockSpec((rows, D_MODEL), lambda j: (0, 0)),
            pl.BlockSpec((D_MODEL, blk), lambda j: (0, j)),
            pl.BlockSpec((1, blk), lambda j: (0, j)),
        ],
        out_specs=pl.BlockSpec((rows, blk), lambda j: (0, j)),
        out_shape=jax.ShapeDtypeStruct((rows, n_out), F32),
        compiler_params=_params(1),
        name="mod",
    )(c_all, w_mod, b_mod)


def _lower_bound(raw_ref):
    r0 = raw_ref[0:1, :]
    r1 = raw_ref[1:2, :]
    mx = jnp.maximum(r0, r1)
    e0 = jnp.exp(r0 - mx)
    e1 = jnp.exp(r1 - mx)
    return e0 / (e0 + e1)


def _split_bf16(a):
    hi = a.astype(BF16)
    lo = (a - hi.astype(F32)).astype(BF16)
    return hi, lo


def _proj_kernel(x_ref, sh_ref, sc_ref, w_ref, lbf_ref, lbb_ref, p_ref, e_ref,
                 proj_scr, *, cps):
    tm = x_ref.shape[0]
    h = (x_ref[...] * (1.0 + sc_ref[0]) + sh_ref[0]).astype(BF16)
    proj_scr[...] = jnp.dot(h, w_ref[...], preferred_element_type=F32)

    lb_f = _lower_bound(lbf_ref)
    lb_b = _lower_bound(lbb_ref)
    ri = lax.broadcasted_iota(jnp.int32, (CHUNK, CHUNK), 0)
    ci = lax.broadcasted_iota(jnp.int32, (CHUNK, CHUNK), 1)
    tri_pre = jnp.where(ci <= ri, 1.0, 0.0).astype(BF16)
    tri_suf = jnp.where(ci >= ri, 1.0, 0.0).astype(BF16)

    def chunk_sum(tri, a):
        hi, lo = _split_bf16(a)
        return (jnp.dot(tri, hi, preferred_element_type=F32)
                + jnp.dot(tri, lo, preferred_element_type=F32))

    for c in range(tm // CHUNK):
        rows = slice(c * CHUNK, (c + 1) * CHUNK)
        q = proj_scr[rows, 0:D_REC]
        qs = q * jax.nn.sigmoid(q)

        def direction(z, lb, tri, mid, end):
            f = lb + (1.0 - lb) * jax.nn.sigmoid(z)
            k = 1.0 - f
            b = chunk_sum(tri, jnp.log(f))
            m = b[mid:mid + 1, :]
            d = b - m
            qd = qs * jnp.exp(d)
            ki = k * jnp.exp(-d)
            em = jnp.exp(m)
            r = jnp.exp(b[end:end + 1, :] - m)
            return qd, ki, em, r

        qd_f, ki_f, em_f, r_f = direction(
            proj_scr[rows, D_REC:2 * D_REC], lb_f, tri_pre, MID_F, CHUNK - 1)
        qd_b, ki_b, em_b, r_b = direction(
            proj_scr[rows, 2 * D_REC:3 * D_REC], lb_b, tri_suf, MID_B, 0)
        g = proj_scr[rows, 4 * D_REC:5 * D_REC]

        p_ref[rows, 0 * D_REC:1 * D_REC] = qd_f.astype(BF16)
        p_ref[rows, 1 * D_REC:2 * D_REC] = ki_f.astype(BF16)
        p_ref[rows, 2 * D_REC:3 * D_REC] = qd_b.astype(BF16)
        p_ref[rows, 3 * D_REC:4 * D_REC] = ki_b.astype(BF16)
        p_ref[rows, 4 * D_REC:5 * D_REC] = proj_scr[rows, 3 * D_REC:4 * D_REC].astype(BF16)
        p_ref[rows, 5 * D_REC:6 * D_REC] = (g * jax.nn.sigmoid(g)).astype(BF16)
        p_ref[rows, 6 * D_REC:7 * D_REC] = proj_scr[rows, 5 * D_REC:6 * D_REC].astype(BF16)

        s, cc = c // cps, c % cps
        e_ref[s, cc:cc + 1, 0 * D_REC:1 * D_REC] = em_f
        e_ref[s, cc:cc + 1, 1 * D_REC:2 * D_REC] = r_f
        e_ref[s, cc:cc + 1, 2 * D_REC:3 * D_REC] = em_b
        e_ref[s, cc:cc + 1, 3 * D_REC:4 * D_REC] = r_b


def _proj_call(x2, mod3, mod_row, w_in, lbf_raw, lbb_raw, n_seq, seq_len):
    n_tok = x2.shape[0]
    tm = TM_PROJ
    cpt = tm // CHUNK
    cps = min(cpt, seq_len // CHUNK)
    spt = cpt // cps
    tps = max(1, seq_len // tm)
    n_ck = seq_len // CHUNK

    if spt > 1:
        e_spec = pl.BlockSpec((spt, cps, E_COLS), lambda i: (i, 0, 0))
    else:
        e_spec = pl.BlockSpec((1, cps, E_COLS), lambda i: (i // tps, i % tps, 0))

    return pl.pallas_call(
        functools.partial(_proj_kernel, cps=cps),
        grid=(n_tok // tm,),
        in_specs=[
            pl.BlockSpec((tm, D_MODEL), lambda i: (i, 0)),
            pl.BlockSpec((1, 1, D_MODEL), lambda i: (mod_row(i, tps), 0, 0)),
            pl.BlockSpec((1, 1, D_MODEL), lambda i: (mod_row(i, tps), 0, 1)),
            _const_spec((D_MODEL, D_IN)),
            _const_spec((2, D_REC)),
            _const_spec((2, D_REC)),
        ],
        out_specs=[
            pl.BlockSpec((tm, P_COLS), lambda i: (i, 0)),
            e_spec,
        ],
        out_shape=[
            jax.ShapeDtypeStruct((n_tok, P_COLS), BF16),
            jax.ShapeDtypeStruct((n_seq, n_ck, E_COLS), F32),
        ],
        scratch_shapes=[pltpu.VMEM((tm, D_IN), F32)],
        compiler_params=_params(1),
        name="proj",
    )(x2, mod3, mod3, w_in, lbf_raw, lbb_raw)


def _scan_kernel(*refs, nck, has_init, emit_final):
    it = iter(refs)
    kif_ref, vf_ref, ef_ref, kib_ref, vb_ref, eb_ref = (next(it) for _ in range(6))
    if has_init:
        s0f_ref, s0b_ref = next(it), next(it)
    smf_ref, smb_ref = next(it), next(it)
    if emit_final:
        finf_ref, finb_ref = next(it), next(it)
    sf_scr, sb_scr = next(it), next(it)

    j = pl.program_id(1)

    @pl.when(j == 0)
    def _():
        for h in range(HEADS):
            if has_init:
                sf_scr[h] = s0f_ref[0, h].T
                sb_scr[h] = s0b_ref[0, h].T
            else:
                sf_scr[h] = jnp.zeros((D_HEAD, D_HEAD), F32)
                sb_scr[h] = jnp.zeros((D_HEAD, D_HEAD), F32)

    def step(ki_ref, v_ref, e_ref, sm_ref, s_scr, c):
        rows = slice(c * CHUNK, (c + 1) * CHUNK)
        for h in range(HEADS):
            cols = slice(h * D_HEAD, (h + 1) * D_HEAD)
            em = e_ref[0, c:c + 1, h * D_HEAD:(h + 1) * D_HEAD]
            r = e_ref[0, c:c + 1, D_REC + h * D_HEAD:D_REC + (h + 1) * D_HEAD]
            sm = s_scr[h] * em
            sm_ref[0, c, h] = sm.astype(BF16).T
            u_t = lax.dot_general(v_ref[rows, cols], ki_ref[rows, cols],
                                  (((0,), (0,)), ((), ())),
                                  preferred_element_type=F32)
            s_scr[h] = (sm + u_t) * r

    for c in range(nck):
        step(kif_ref, vf_ref, ef_ref, smf_ref, sf_scr, c)
        step(kib_ref, vb_ref, eb_ref, smb_ref, sb_scr, nck - 1 - c)

    if emit_final:
        @pl.when(j == pl.num_programs(1) - 1)
        def _():
            for h in range(HEADS):
                finf_ref[0, h] = sf_scr[h].T
                finb_ref[0, h] = sb_scr[h].T


def _scan_call(p, e, n_seq, seq_len, s0f=None, s0b=None, emit_final=False):
    tb = min(seq_len, 512)
    nt = seq_len // tb
    nck = tb // CHUNK
    n_ck = seq_len // CHUNK
    has_init = s0f is not None
    st_blk = (1, HEADS, D_HEAD, D_HEAD)

    in_specs = [
        pl.BlockSpec((tb, D_REC), lambda b, j: (b * nt + j, 1)),
        pl.BlockSpec((tb, D_REC), lambda b, j: (b * nt + j, 4)),
        pl.BlockSpec((1, nck, 2 * D_REC), lambda b, j: (b, j, 0)),
        pl.BlockSpec((tb, D_REC), lambda b, j: (b * nt + nt - 1 - j, 3)),
        pl.BlockSpec((tb, D_REC), lambda b, j: (b * nt + nt - 1 - j, 4)),
        pl.BlockSpec((1, nck, 2 * D_REC), lambda b, j: (b, nt - 1 - j, 1)),
    ]
    args = [p, p, e, p, p, e]
    if has_init:
        in_specs += [pl.BlockSpec(st_blk, lambda b, j: (b, 0, 0, 0))] * 2
        args += [s0f, s0b]
    sm_shape = jax.ShapeDtypeStruct((n_seq, n_ck, HEADS, D_HEAD, D_HEAD), BF16)
    sm_blk = (1, nck, HEADS, D_HEAD, D_HEAD)
    out_specs = [
        pl.BlockSpec(sm_blk, lambda b, j: (b, j, 0, 0, 0)),
        pl.BlockSpec(sm_blk, lambda b, j: (b, nt - 1 - j, 0, 0, 0)),
    ]
    out_shape = [sm_shape, sm_shape]
    if emit_final:
        out_specs += [pl.BlockSpec(st_blk, lambda b, j: (b, 0, 0, 0))] * 2
        out_shape += [jax.ShapeDtypeStruct((n_seq, HEADS, D_HEAD, D_HEAD), F32)] * 2

    return pl.pallas_call(
        functools.partial(_scan_kernel, nck=nck, has_init=has_init,
                          emit_final=emit_final),
        grid=(n_seq, nt),
        in_specs=in_specs,
        out_specs=out_specs,
        out_shape=out_shape,
        scratch_shapes=[pltpu.VMEM((HEADS, D_HEAD, D_HEAD), F32)] * 2,
        compiler_params=_params(2),
        name="scan",
    )(*args)


def _window(n, w):
    idx = np.arange(n)
    return np.clip(idx - w // 2, 0, n), np.clip(idx + w // 2, 0, n)


def _band(n, w):
    lo, hi = _window(n, w)
    j = np.arange(n)[None, :]
    return ((j >= lo[:, None]) & (j < hi[:, None])).astype(np.float32), (hi - lo)


def _pool_seq_kernel(u_ref, a_ref, inv_ref, pw_ref, ps_ref, o_ref, diff_scr, *,
                     n_seq_blk, seq_len):
    for gi in range(len(POOL_WINDOWS)):
        cols = slice(gi * POOL_GW, (gi + 1) * POOL_GW)
        for s in range(n_seq_blk):
            rows = slice(s * seq_len, (s + 1) * seq_len)
            ug = u_ref[rows, cols]
            tot = jnp.dot(a_ref[gi], ug, preferred_element_type=F32)
            diff_scr[rows, cols] = (tot * inv_ref[gi] - ug.astype(F32)).astype(BF16)
    for gi in range(len(POOL_WINDOWS)):
        cols = slice(gi * POOL_GW, (gi + 1) * POOL_GW)
        y = jnp.dot(diff_scr[:, cols], pw_ref[gi], preferred_element_type=F32)
        o_ref[:, cols] = (y * ps_ref[:, cols]).astype(BF16)


def _pool_grid_kernel(u_ref, a_ref, invc_ref, pw_ref, ps_ref, o_ref, col_scr, diff_scr,
                      *, n_rows):
    blk = a_ref.shape[1]
    rpb = blk // GRID_W
    for gi, w in enumerate(POOL_WINDOWS):
        cols = slice(gi * POOL_GW, (gi + 1) * POOL_GW)
        for rb in range(n_rows // rpb):
            rows = slice(rb * blk, (rb + 1) * blk)
            tot = jnp.dot(a_ref[gi], u_ref[rows, cols], preferred_element_type=F32)
            col_scr[gi, rows, :] = tot * invc_ref[gi]
    for gi, w in enumerate(POOL_WINDOWS):
        cols = slice(gi * POOL_GW, (gi + 1) * POOL_GW)
        lo, hi = _window(n_rows, w)

        def grid_row(rr):
            return col_scr[gi, rr * GRID_W:(rr + 1) * GRID_W, :]

        acc = None
        for r in range(n_rows):
            if r == 0:
                acc = grid_row(int(lo[0]))
                for rr in range(int(lo[0]) + 1, int(hi[0])):
                    acc = acc + grid_row(rr)
            else:
                for rr in range(int(hi[r - 1]), int(hi[r])):
                    acc = acc + grid_row(rr)
                for rr in range(int(lo[r - 1]), int(lo[r])):
                    acc = acc - grid_row(rr)
            rows = slice(r * GRID_W, (r + 1) * GRID_W)
            mean = acc * (1.0 / float(hi[r] - lo[r]))
            diff_scr[rows, cols] = (mean - u_ref[rows, cols].astype(F32)).astype(BF16)
    for gi in range(len(POOL_WINDOWS)):
        cols = slice(gi * POOL_GW, (gi + 1) * POOL_GW)
        y = jnp.dot(diff_scr[:, cols], pw_ref[gi], preferred_element_type=F32)
        o_ref[:, cols] = (y * ps_ref[:, cols]).astype(BF16)


def _pool_call(p, pool_w, pool_scale, n_seq, seq_len, two_d):
    n_tok = p.shape[0]
    ng = len(POOL_WINDOWS)
    if two_d:
        n_rows = seq_len // GRID_W
        rpb = 4
        blk = rpb * GRID_W
        bands = [_band(GRID_W, w) for w in POOL_WINDOWS]
        a = np.stack([np.kron(np.eye(rpb, dtype=np.float32), b[0]) for b in bands])
        inv = np.stack([np.broadcast_to(np.tile(1.0 / b[1], rpb)[:, None], (blk, POOL_GW))
                        for b in bands]).astype(np.float32)
        kern = functools.partial(_pool_grid_kernel, n_rows=n_rows)
        tb = seq_len
        scratch = [pltpu.VMEM((ng, seq_len, POOL_GW), F32),
                   pltpu.VMEM((seq_len, D_POOL), BF16)]
    else:
        blk = seq_len
        bands = [_band(seq_len, w) for w in POOL_WINDOWS]
        a = np.stack([b[0] for b in bands])
        inv = np.stack([np.broadcast_to((1.0 / b[1])[:, None], (blk, POOL_GW))
                        for b in bands]).astype(np.float32)
        n_seq_blk = 8
        kern = functools.partial(_pool_seq_kernel, n_seq_blk=n_seq_blk, seq_len=seq_len)
        tb = n_seq_blk * seq_len
        scratch = [pltpu.VMEM((tb, D_POOL), BF16)]
    return pl.pallas_call(
        kern,
        grid=(n_tok // tb,),
        in_specs=[
            pl.BlockSpec((tb, D_POOL), lambda i: (i, 6)),
            _const_spec((ng, blk, blk)),
            _const_spec((ng, blk, POOL_GW)),
            _const_spec((ng, POOL_GW, POOL_GW)),
            _const_spec((1, D_POOL)),
        ],
        out_specs=pl.BlockSpec((tb, D_POOL), lambda i: (i, 0)),
        out_shape=jax.ShapeDtypeStruct((n_tok, D_POOL), BF16),
        scratch_shapes=scratch,
        compiler_params=_params(1),
        name="pool",
    )(p, jnp.asarray(a, BF16), jnp.asarray(inv, F32), pool_w, pool_scale)


def _layer_norm(u, g, b):
    mu = jnp.mean(u, axis=-1, keepdims=True)
    d = u - mu
    var = jnp.mean(d * d, axis=-1, keepdims=True)
    return d * lax.rsqrt(var + LN_EPS) * g + b


def _tail_kernel(x_ref, p_ref, smf_ref, smb_ref, yp_ref, g1_ref, sh2_ref, sc2_ref,
                 g2_ref, gn_ref, wo_ref, l1g_ref, l1b_ref, wg_ref, wu_ref, wd_ref,
                 l2g_ref, l2b_ref, o_ref, yr_scr, u_scr, h2_scr, *, cps):
    tm = x_ref.shape[0]
    ri = lax.broadcasted_iota(jnp.int32, (CHUNK, CHUNK), 0)
    ci = lax.broadcasted_iota(jnp.int32, (CHUNK, CHUNK), 1)
    lower = ci <= ri
    upper = ci >= ri
    nt_dims = (((1,), (1,)), ((), ()))
    gn = gn_ref[...]

    th = tm // 2
    half = [slice(0, th), slice(th, tm)]
    n_rb = th // ROWS
    fs = D_FF // N_FF_SPLIT

    def rows_of(c):
        return slice(c * CHUNK, (c + 1) * CHUNK)

    def col(k, h):
        return slice(k * D_REC + h * D_HEAD, k * D_REC + (h + 1) * D_HEAD)

    def mixer_half(i, fill):
        cph = th // CHUNK
        pieces = [(c, h) for c in range(i * cph, (i + 1) * cph) for h in range(HEADS)]
        scores = []
        for c, h in pieces:
            rows = rows_of(c)
            sc_f = lax.dot_general(p_ref[rows, col(0, h)], p_ref[rows, col(1, h)],
                                   nt_dims, preferred_element_type=F32)
            sc_b = lax.dot_general(p_ref[rows, col(2, h)], p_ref[rows, col(3, h)],
                                   nt_dims, preferred_element_type=F32)
            scores.append((jnp.where(lower, sc_f, 0.0)
                           + jnp.where(upper, sc_b, 0.0)).astype(BF16))
        fill(0)
        outs = []
        for (c, h), sc in zip(pieces, scores):
            rows = rows_of(c)
            s_i, c_i = c // cps, c % cps
            s_cat = jnp.concatenate([smf_ref[s_i, c_i, h], smb_ref[s_i, c_i, h]], axis=0)
            q_cat = jnp.concatenate([p_ref[rows, col(0, h)], p_ref[rows, col(2, h)]],
                                    axis=1)
            o = jnp.dot(sc, p_ref[rows, col(4, h)], preferred_element_type=F32)
            outs.append(o + jnp.dot(q_cat, s_cat, preferred_element_type=F32))
        fill(1)
        for (c, h), o in zip(pieces, outs):
            rows = rows_of(c)
            ms = jnp.mean(o * o, axis=-1, keepdims=True)
            on = o * lax.rsqrt(ms + RMS_EPS) * gn
            gs = p_ref[rows, col(5, h)]
            yr_scr[rows, h * D_HEAD:(h + 1) * D_HEAD] = (on * gs.astype(F32)).astype(BF16)

    def out_proj(hr):
        y = jnp.dot(yr_scr[hr, :], wo_ref[0:D_REC, :], preferred_element_type=F32)
        y = y + jnp.dot(yp_ref[hr, :], wo_ref[D_REC:2 * D_REC, :],
                        preferred_element_type=F32)
        return ALPHA * x_ref[hr, :] + g1_ref[0] * y

    def norm1_block(i, m):
        rows = slice(i * th + m * ROWS, i * th + (m + 1) * ROWS)
        x1 = _layer_norm(u_scr[rows, :], l1g_ref[...], l1b_ref[...])
        u_scr[rows, :] = x1
        h2_scr[rows, :] = (x1 * (1.0 + sc2_ref[0]) + sh2_ref[0]).astype(BF16)

    def norm2_block(i, m):
        rows = slice(i * th + m * ROWS, i * th + (m + 1) * ROWS)
        o_ref[rows, :] = _layer_norm(u_scr[rows, :], l2g_ref[...], l2b_ref[...])

    def ffn_half(i, fill):
        hr = half[i]
        h2 = h2_scr[hr, :]
        ffn = None
        j = 0
        for k in range(N_FF_SPLIT):
            a = jnp.dot(h2, wg_ref[:, k * fs:(k + 1) * fs], preferred_element_type=F32)
            fill(j); j += 1
            b = jnp.dot(h2, wu_ref[:, k * fs:(k + 1) * fs], preferred_element_type=F32)
            fill(j); j += 1
            act = (a * jax.nn.sigmoid(a) * b).astype(BF16)
            part = jnp.dot(act, wd_ref[k * fs:(k + 1) * fs, :],
                           preferred_element_type=F32)
            ffn = part if ffn is None else ffn + part
        u_scr[hr, :] = ALPHA * u_scr[hr, :] + g2_ref[0] * ffn

    def norm1_a(j):
        for m in range(j * n_rb // 2, (j + 1) * n_rb // 2):
            norm1_block(0, m)

    mixer_half(0, lambda j: None)
    u_scr[half[0], :] = out_proj(half[0])
    mixer_half(1, norm1_a)
    u_scr[half[1], :] = out_proj(half[1])
    ffn_half(0, lambda j: norm1_block(1, j) if j < n_rb else None)
    ffn_half(1, lambda j: norm2_block(0, j) if j < n_rb else None)
    for m in range(n_rb):
        norm2_block(1, m)


def _tail_call(x2, p, smf, smb, yp, mod3, mod_row, gn, wo, l1g, l1b, wg, wu, wd,
               l2g, l2b, seq_len):
    n_tok = x2.shape[0]
    tm = TM_TAIL
    cpt = tm // CHUNK
    cps = min(cpt, seq_len // CHUNK)
    spt = cpt // cps
    tps = max(1, seq_len // tm)
    sm_blk = (spt, cps, HEADS, D_HEAD, D_HEAD)

    def mod_spec(col):
        return pl.BlockSpec((1, 1, D_MODEL), lambda i: (mod_row(i, tps), 0, col))

    if spt > 1:
        sm_spec = pl.BlockSpec(sm_blk, lambda i: (i, 0, 0, 0, 0))
    else:
        sm_spec = pl.BlockSpec(sm_blk, lambda i: (i // tps, i % tps, 0, 0, 0))
    return pl.pallas_call(
        functools.partial(_tail_kernel, cps=cps),
        grid=(n_tok // tm,),
        in_specs=[
            pl.BlockSpec((tm, D_MODEL), lambda i: (i, 0)),
            pl.BlockSpec((tm, 6 * D_REC), lambda i: (i, 0)),
            sm_spec, sm_spec,
            pl.BlockSpec((tm, D_POOL), lambda i: (i, 0)),
            mod_spec(2), mod_spec(3), mod_spec(4), mod_spec(5),
            _const_spec((1, D_HEAD)),
            _const_spec((D_MODEL, D_MODEL)),
            _const_spec((1, D_MODEL)), _const_spec((1, D_MODEL)),
            _const_spec((D_MODEL, D_FF)), _const_spec((D_MODEL, D_FF)),
            _const_spec((D_FF, D_MODEL)),
            _const_spec((1, D_MODEL)), _const_spec((1, D_MODEL)),
        ],
        out_specs=pl.BlockSpec((tm, D_MODEL), lambda i: (i, 0)),
        out_shape=jax.ShapeDtypeStruct((n_tok, D_MODEL), F32),
        scratch_shapes=[
            pltpu.VMEM((tm, D_REC), BF16),
            pltpu.VMEM((tm, D_MODEL), F32),
            pltpu.VMEM((tm, D_MODEL), BF16),
        ],
        compiler_params=_params(1),
        name="tail",
    )(x2, p, smf, smb, yp, mod3, mod3, mod3, mod3, gn, wo, l1g, l1b, wg, wu, wd,
      l2g, l2b)


def kernel(x_prompt, x_sample, state_hgrn_fwd, state_hgrn_bwd, c, c_ctx, w_mod, b_mod,
           w_in, lb_fwd_raw, lb_bwd_raw, hgrn_norm_g, pool_w, pool_scale, w_out, ln1_g,
           ln1_b, w_ffn_gate, w_ffn_up, w_ffn_down, ln2_g, ln2_b):
    n_ctx, t_ctx, _ = x_prompt.shape
    n_lat, t_lat, _ = x_sample.shape
    ctx_row = n_lat

    c_all = jnp.concatenate(
        [c, c_ctx[None, :], jnp.zeros((16 - n_lat - 1, D_MODEL), F32)], axis=0)
    mod = _mod_call(c_all, w_mod[0], b_mod)
    mod3 = mod.reshape(16, 1, 6 * D_MODEL)

    w_in_b = w_in[0].astype(BF16)
    wo_b = w_out[0].astype(BF16)
    wg_b = w_ffn_gate[0].astype(BF16)
    wu_b = w_ffn_up[0].astype(BF16)
    wd_b = w_ffn_down[0].astype(BF16)
    pw_b = pool_w[0].astype(BF16)

    def run(x, n_seq, seq_len, mod_row, two_d, s0f, s0b, emit_final):
        x2 = x.reshape(n_seq * seq_len, D_MODEL)
        p, e = _proj_call(x2, mod3, mod_row, w_in_b, lb_fwd_raw, lb_bwd_raw, n_seq, seq_len)
        scan = _scan_call(p, e, n_seq, seq_len, s0f, s0b, emit_final)
        yp = _pool_call(p, pw_b, pool_scale, n_seq, seq_len, two_d)
        y = _tail_call(x2, p, scan[0], scan[1], yp, mod3, mod_row, hgrn_norm_g, wo_b,
                       ln1_g, ln1_b, wg_b, wu_b, wd_b, ln2_g, ln2_b, seq_len)
        return y.reshape(n_seq, seq_len, D_MODEL), scan[2:]

    y_ctx, fin = run(x_prompt, n_ctx, t_ctx, lambda i, tps: ctx_row, False,
                     None, None, True)
    y_lat, _ = run(x_sample, n_lat, t_lat, lambda i, tps: i // tps, True,
                   state_hgrn_fwd[:, 0], state_hgrn_bwd[:, 0], False)
    return (y_ctx, y_lat, fin[0][:, None], fin[1][:, None])
```

```python
import functools

import numpy as np
import jax
import jax.numpy as jnp
from jax import lax
from jax.experimental import pallas as pl
from jax.experimental.pallas import tpu as pltpu

F32 = jnp.float32
BF16 = jnp.bfloat16

D_MODEL = 1024
D_REC = 512
D_POOL = 512
HEADS = 4
D_HEAD = 128
D_IN = 5 * D_REC + D_POOL
D_FF = 2816
GRID_W = 64
POOL_WINDOWS = (2, 4, 8, 16)
POOL_GW = 128
ALPHA = 2.0 ** 0.25
LN_EPS = 1e-5
RMS_EPS = 1e-6

CHUNK = 64
MID_F = CHUNK // 2 - 1
MID_B = CHUNK // 2
TM_PROJ = 512
TB_SCAN = 1024
SEQ_PER_SCAN = 4
TM_TAIL = 512
ROWS = 64
N_FF_SPLIT = 2
P_COLS = 7 * D_REC
E_COLS = 4 * D_REC
VMEM_LIMIT = 56 * 1024 * 1024

_ARB = "arbitrary"


def _params(n_axes):
    return pltpu.CompilerParams(
        dimension_semantics=(_ARB,) * n_axes, vmem_limit_bytes=VMEM_LIMIT)


def _const_spec(shape):
    nd = len(shape)
    return pl.BlockSpec(shape, lambda *_: (0,) * nd, pipeline_mode=pl.Buffered(1))


def _mod_kernel(c_ref, w_ref, b_ref, o_ref):
    c = c_ref[...]
    hi, lo = _split_bf16(c * jax.nn.sigmoid(c))
    w = w_ref[...].astype(BF16)
    o_ref[...] = (jnp.dot(hi, w, preferred_element_type=F32)
                  + jnp.dot(lo, w, preferred_element_type=F32) + b_ref[...])


def _mod_call(c_all, w_mod, b_mod):
    rows = c_all.shape[0]
    n_out = w_mod.shape[1]
    blk = 1024
    return pl.pallas_call(
        _mod_kernel,
        grid=(n_out // blk,),
        in_specs=[
            pl.BlockSpec((rows, D_MODEL), lambda j: (0, 0)),
            pl.BlockSpec((D_MODEL, blk), lambda j: (0, j)),
            pl.BlockSpec((1, blk), lambda j: (0, j)),
        ],
        out_specs=pl.BlockSpec((rows, blk), lambda j: (0, j)),
        out_shape=jax.ShapeDtypeStruct((rows, n_out), F32),
        compiler_params=_params(1),
        name="mod",
    )(c_all, w_mod, b_mod)


def _lower_bound(raw_ref):
    r0 = raw_ref[0:1, :]
    r1 = raw_ref[1:2, :]
    mx = jnp.maximum(r0, r1)
    e0 = jnp.exp(r0 - mx)
    e1 = jnp.exp(r1 - mx)
    return e0 / (e0 + e1)


def _split_bf16(a):
    hi = a.astype(BF16)
    lo = (a - hi.astype(F32)).astype(BF16)
    return hi, lo


def _proj_kernel(x_ref, sh_ref, sc_ref, w_ref, lbf_ref, lbb_ref, p_ref, e_ref,
                 proj_scr, *, cps):
    tm = x_ref.shape[0]
    h = (x_ref[...] * (1.0 + sc_ref[0]) + sh_ref[0]).astype(BF16)
    proj_scr[...] = jnp.dot(h, w_ref[...], preferred_element_type=F32)

    lb_f = _lower_bound(lbf_ref)
    lb_b = _lower_bound(lbb_ref)
    ri = lax.broadcasted_iota(jnp.int32, (CHUNK, CHUNK), 0)
    ci = lax.broadcasted_iota(jnp.int32, (CHUNK, CHUNK), 1)
    tri_pre = jnp.where(ci <= ri, 1.0, 0.0).astype(BF16)
    tri_suf = jnp.where(ci >= ri, 1.0, 0.0).astype(BF16)

    def chunk_sum(tri, a):
        hi, lo = _split_bf16(a)
        return (jnp.dot(tri, hi, preferred_element_type=F32)
                + jnp.dot(tri, lo, preferred_element_type=F32))

    for c in range(tm // CHUNK):
        rows = slice(c * CHUNK, (c + 1) * CHUNK)
        q = proj_scr[rows, 0:D_REC]
        qs = q * jax.nn.sigmoid(q)

        def direction(z, lb, tri, mid, end):
            f = lb + (1.0 - lb) * jax.nn.sigmoid(z)
            k = 1.0 - f
            b = chunk_sum(tri, jnp.log(f))
            m = b[mid:mid + 1, :]
            d = b - m
            qd = qs * jnp.exp(d)
            ki = k * jnp.exp(-d)
            em = jnp.exp(m)
            r = jnp.exp(b[end:end + 1, :] - m)
            return qd, ki, em, r

        qd_f, ki_f, em_f, r_f = direction(
            proj_scr[rows, D_REC:2 * D_REC], lb_f, tri_pre, MID_F, CHUNK - 1)
        qd_b, ki_b, em_b, r_b = direction(
            proj_scr[rows, 2 * D_REC:3 * D_REC], lb_b, tri_suf, MID_B, 0)
        g = proj_scr[rows, 4 * D_REC:5 * D_REC]

        p_ref[rows, 0 * D_REC:1 * D_REC] = qd_f.astype(BF16)
        p_ref[rows, 1 * D_REC:2 * D_REC] = ki_f.astype(BF16)
        p_ref[rows, 2 * D_REC:3 * D_REC] = qd_b.astype(BF16)
        p_ref[rows, 3 * D_REC:4 * D_REC] = ki_b.astype(BF16)
        p_ref[rows, 4 * D_REC:5 * D_REC] = proj_scr[rows, 3 * D_REC:4 * D_REC].astype(BF16)
        p_ref[rows, 5 * D_REC:6 * D_REC] = (g * jax.nn.sigmoid(g)).astype(BF16)
        p_ref[rows, 6 * D_REC:7 * D_REC] = proj_scr[rows, 5 * D_REC:6 * D_REC].astype(BF16)

        s, cc = c // cps, c % cps
        e_ref[s, cc:cc + 1, 0 * D_REC:1 * D_REC] = em_f
        e_ref[s, cc:cc + 1, 1 * D_REC:2 * D_REC] = r_f
        e_ref[s, cc:cc + 1, 2 * D_REC:3 * D_REC] = em_b
        e_ref[s, cc:cc + 1, 3 * D_REC:4 * D_REC] = r_b


def _proj_call(x2, mod3, mod_row, w_in, lbf_raw, lbb_raw, n_seq, seq_len):
    n_tok = x2.shape[0]
    tm = TM_PROJ
    cpt = tm // CHUNK
    cps = min(cpt, seq_len // CHUNK)
    spt = cpt // cps
    tps = max(1, seq_len // tm)
    n_ck = seq_len // CHUNK

    if spt > 1:
        e_spec = pl.BlockSpec((spt, cps, E_COLS), lambda i: (i, 0, 0))
    else:
        e_spec = pl.BlockSpec((1, cps, E_COLS), lambda i: (i // tps, i % tps, 0))

    return pl.pallas_call(
        functools.partial(_proj_kernel, cps=cps),
        grid=(n_tok // tm,),
        in_specs=[
            pl.BlockSpec((tm, D_MODEL), lambda i: (i, 0)),
            pl.BlockSpec((1, 1, D_MODEL), lambda i: (mod_row(i, tps), 0, 0)),
            pl.BlockSpec((1, 1, D_MODEL), lambda i: (mod_row(i, tps), 0, 1)),
            _const_spec((D_MODEL, D_IN)),
            _const_spec((2, D_REC)),
            _const_spec((2, D_REC)),
        ],
        out_specs=[
            pl.BlockSpec((tm, P_COLS), lambda i: (i, 0)),
            e_spec,
        ],
        out_shape=[
            jax.ShapeDtypeStruct((n_tok, P_COLS), BF16),
            jax.ShapeDtypeStruct((n_seq, n_ck, E_COLS), F32),
        ],
        scratch_shapes=[pltpu.VMEM((tm, D_IN), F32)],
        compiler_params=_params(1),
        name="proj",
    )(x2, mod3, mod3, w_in, lbf_raw, lbb_raw)


def _scan_kernel(*refs, nck, spb, has_init, emit_final):
    it = iter(refs)
    kif_ref, vf_ref, ef_ref, kib_ref, vb_ref, eb_ref = (next(it) for _ in range(6))
    if has_init:
        s0f_ref, s0b_ref = next(it), next(it)
    smf_ref, smb_ref = next(it), next(it)
    if emit_final:
        finf_ref, finb_ref = next(it), next(it)
    sf_scr, sb_scr, uf_scr, ub_scr = (next(it) for _ in range(4))

    j = pl.program_id(1)
    seq_heads = [(q, h) for q in range(spb) for h in range(HEADS)]

    @pl.when(j == 0)
    def _():
        for q, h in seq_heads:
            if has_init:
                sf_scr[q, h] = s0f_ref[q, h].T
                sb_scr[q, h] = s0b_ref[q, h].T
            else:
                sf_scr[q, h] = jnp.zeros((D_HEAD, D_HEAD), F32)
                sb_scr[q, h] = jnp.zeros((D_HEAD, D_HEAD), F32)

    def products(ki_ref, v_ref, u_scr):
        for q in range(spb):
            for c in range(nck):
                rows = slice((q * nck + c) * CHUNK, (q * nck + c + 1) * CHUNK)
                for h in range(HEADS):
                    cols = slice(h * D_HEAD, (h + 1) * D_HEAD)
                    u_scr[q, c, h] = lax.dot_general(
                        v_ref[rows, cols], ki_ref[rows, cols], (((0,), (0,)), ((), ())),
                        preferred_element_type=F32)

    def step(e_ref, sm_ref, s_scr, u_scr, c):
        for q, h in seq_heads:
            em = e_ref[q, c:c + 1, h * D_HEAD:(h + 1) * D_HEAD]
            r = e_ref[q, c:c + 1, D_REC + h * D_HEAD:D_REC + (h + 1) * D_HEAD]
            sm = s_scr[q, h] * em
            sm_ref[q, c, h] = sm.astype(BF16).T
            s_scr[q, h] = (sm + u_scr[q, c, h]) * r

    products(kif_ref, vf_ref, uf_scr)
    products(kib_ref, vb_ref, ub_scr)
    for c in range(nck):
        step(ef_ref, smf_ref, sf_scr, uf_scr, c)
        step(eb_ref, smb_ref, sb_scr, ub_scr, nck - 1 - c)

    if emit_final:
        @pl.when(j == pl.num_programs(1) - 1)
        def _():
            for q, h in seq_heads:
                finf_ref[q, h] = sf_scr[q, h].T
                finb_ref[q, h] = sb_scr[q, h].T


def _scan_call(p, e, n_seq, seq_len, s0f=None, s0b=None, emit_final=False):
    tb = min(seq_len, TB_SCAN)
    nt = seq_len // tb
    spb = SEQ_PER_SCAN if nt == 1 else 1
    nck = tb // CHUNK
    n_ck = seq_len // CHUNK
    has_init = s0f is not None
    st_blk = (spb, HEADS, D_HEAD, D_HEAD)
    rows = spb * tb

    in_specs = [
        pl.BlockSpec((rows, D_REC), lambda b, j: (b * nt + j, 1)),
        pl.BlockSpec((rows, D_REC), lambda b, j: (b * nt + j, 4)),
        pl.BlockSpec((spb, nck, 2 * D_REC), lambda b, j: (b, j, 0)),
        pl.BlockSpec((rows, D_REC), lambda b, j: (b * nt + nt - 1 - j, 3)),
        pl.BlockSpec((rows, D_REC), lambda b, j: (b * nt + nt - 1 - j, 4)),
        pl.BlockSpec((spb, nck, 2 * D_REC), lambda b, j: (b, nt - 1 - j, 1)),
    ]
    args = [p, p, e, p, p, e]
    if has_init:
        in_specs += [pl.BlockSpec(st_blk, lambda b, j: (b, 0, 0, 0))] * 2
        args += [s0f, s0b]
    sm_shape = jax.ShapeDtypeStruct((n_seq, n_ck, HEADS, D_HEAD, D_HEAD), BF16)
    sm_blk = (spb, nck, HEADS, D_HEAD, D_HEAD)
    out_specs = [
        pl.BlockSpec(sm_blk, lambda b, j: (b, j, 0, 0, 0)),
        pl.BlockSpec(sm_blk, lambda b, j: (b, nt - 1 - j, 0, 0, 0)),
    ]
    out_shape = [sm_shape, sm_shape]
    if emit_final:
        out_specs += [pl.BlockSpec(st_blk, lambda b, j: (b, 0, 0, 0))] * 2
        out_shape += [jax.ShapeDtypeStruct((n_seq, HEADS, D_HEAD, D_HEAD), F32)] * 2

    return pl.pallas_call(
        functools.partial(_scan_kernel, nck=nck, spb=spb, has_init=has_init,
                          emit_final=emit_final),
        grid=(n_seq // spb, nt),
        in_specs=in_specs,
        out_specs=out_specs,
        out_shape=out_shape,
        scratch_shapes=([pltpu.VMEM((spb, HEADS, D_HEAD, D_HEAD), F32)] * 2
                        + [pltpu.VMEM((spb, nck, HEADS, D_HEAD, D_HEAD), F32)] * 2),
        compiler_params=_params(2),
        name="scan",
    )(*args)


def _window(n, w):
    idx = np.arange(n)
    return np.clip(idx - w // 2, 0, n), np.clip(idx + w // 2, 0, n)


def _band(n, w):
    lo, hi = _window(n, w)
    j = np.arange(n)[None, :]
    return ((j >= lo[:, None]) & (j < hi[:, None])).astype(np.float32), (hi - lo)


def _pool_seq_kernel(u_ref, a_ref, inv_ref, pw_ref, ps_ref, o_ref, diff_scr, *,
                     n_seq_blk, seq_len):
    for gi in range(len(POOL_WINDOWS)):
        cols = slice(gi * POOL_GW, (gi + 1) * POOL_GW)
        for s in range(n_seq_blk):
            rows = slice(s * seq_len, (s + 1) * seq_len)
            ug = u_ref[rows, cols]
            tot = jnp.dot(a_ref[gi], ug, preferred_element_type=F32)
            diff_scr[rows, cols] = (tot * inv_ref[gi] - ug.astype(F32)).astype(BF16)
    for gi in range(len(POOL_WINDOWS)):
        cols = slice(gi * POOL_GW, (gi + 1) * POOL_GW)
        y = jnp.dot(diff_scr[:, cols], pw_ref[gi], preferred_element_type=F32)
        o_ref[:, cols] = (y * ps_ref[:, cols]).astype(BF16)


def _pool_grid_kernel(u_ref, a_ref, invc_ref, pw_ref, ps_ref, o_ref, col_scr, diff_scr,
                      *, n_rows):
    blk = a_ref.shape[1]
    rpb = blk // GRID_W
    for gi, w in enumerate(POOL_WINDOWS):
        cols = slice(gi * POOL_GW, (gi + 1) * POOL_GW)
        for rb in range(n_rows // rpb):
            rows = slice(rb * blk, (rb + 1) * blk)
            tot = jnp.dot(a_ref[gi], u_ref[rows, cols], preferred_element_type=F32)
            col_scr[gi, rows, :] = tot * invc_ref[gi]
    for gi, w in enumerate(POOL_WINDOWS):
        cols = slice(gi * POOL_GW, (gi + 1) * POOL_GW)
        lo, hi = _window(n_rows, w)

        def grid_row(rr):
            return col_scr[gi, rr * GRID_W:(rr + 1) * GRID_W, :]

        acc = None
        for r in range(n_rows):
            if r == 0:
                acc = grid_row(int(lo[0]))
                for rr in range(int(lo[0]) + 1, int(hi[0])):
                    acc = acc + grid_row(rr)
            else:
                for rr in range(int(hi[r - 1]), int(hi[r])):
                    acc = acc + grid_row(rr)
                for rr in range(int(lo[r - 1]), int(lo[r])):
                    acc = acc - grid_row(rr)
            rows = slice(r * GRID_W, (r + 1) * GRID_W)
            mean = acc * (1.0 / float(hi[r] - lo[r]))
            diff_scr[rows, cols] = (mean - u_ref[rows, cols].astype(F32)).astype(BF16)
    for gi in range(len(POOL_WINDOWS)):
        cols = slice(gi * POOL_GW, (gi + 1) * POOL_GW)
        y = jnp.dot(diff_scr[:, cols], pw_ref[gi], preferred_element_type=F32)
        o_ref[:, cols] = (y * ps_ref[:, cols]).astype(BF16)


def _pool_call(p, pool_w, pool_scale, n_seq, seq_len, two_d):
    n_tok = p.shape[0]
    ng = len(POOL_WINDOWS)
    if two_d:
        n_rows = seq_len // GRID_W
        rpb = 4
        blk = rpb * GRID_W
        bands = [_band(GRID_W, w) for w in POOL_WINDOWS]
        a = np.stack([np.kron(np.eye(rpb, dtype=np.float32), b[0]) for b in bands])
        inv = np.stack([np.broadcast_to(np.tile(1.0 / b[1], rpb)[:, None], (blk, POOL_GW))
                        for b in bands]).astype(np.float32)
        kern = functools.partial(_pool_grid_kernel, n_rows=n_rows)
        tb = seq_len
        scratch = [pltpu.VMEM((ng, seq_len, POOL_GW), F32),
                   pltpu.VMEM((seq_len, D_POOL), BF16)]
    else:
        blk = seq_len
        bands = [_band(seq_len, w) for w in POOL_WINDOWS]
        a = np.stack([b[0] for b in bands])
        inv = np.stack([np.broadcast_to((1.0 / b[1])[:, None], (blk, POOL_GW))
                        for b in bands]).astype(np.float32)
        n_seq_blk = 8
        kern = functools.partial(_pool_seq_kernel, n_seq_blk=n_seq_blk, seq_len=seq_len)
        tb = n_seq_blk * seq_len
        scratch = [pltpu.VMEM((tb, D_POOL), BF16)]
    return pl.pallas_call(
        kern,
        grid=(n_tok // tb,),
        in_specs=[
            pl.BlockSpec((tb, D_POOL), lambda i: (i, 6)),
            _const_spec((ng, blk, blk)),
            _const_spec((ng, blk, POOL_GW)),
            _const_spec((ng, POOL_GW, POOL_GW)),
            _const_spec((1, D_POOL)),
        ],
        out_specs=pl.BlockSpec((tb, D_POOL), lambda i: (i, 0)),
        out_shape=jax.ShapeDtypeStruct((n_tok, D_POOL), BF16),
        scratch_shapes=scratch,
        compiler_params=_params(1),
        name="pool",
    )(p, jnp.asarray(a, BF16), jnp.asarray(inv, F32), pool_w, pool_scale)


def _layer_norm(u, g, b):
    mu = jnp.mean(u, axis=-1, keepdims=True)
    d = u - mu
    var = jnp.mean(d * d, axis=-1, keepdims=True)
    return d * lax.rsqrt(var + LN_EPS) * g + b


def _tail_kernel(x_ref, p_ref, smf_ref, smb_ref, yp_ref, g1_ref, sh2_ref, sc2_ref,
                 g2_ref, gn_ref, wo_ref, l1g_ref, l1b_ref, wg_ref, wu_ref, wd_ref,
                 l2g_ref, l2b_ref, o_ref, yr_scr, u_scr, h2_scr, *, cps):
    tm = x_ref.shape[0]
    ri = lax.broadcasted_iota(jnp.int32, (CHUNK, CHUNK), 0)
    ci = lax.broadcasted_iota(jnp.int32, (CHUNK, CHUNK), 1)
    lower = ci <= ri
    upper = ci >= ri
    nt_dims = (((1,), (1,)), ((), ()))
    gn = gn_ref[...]

    th = tm // 2
    half = [slice(0, th), slice(th, tm)]
    n_rb = th // ROWS
    fs = D_FF // N_FF_SPLIT

    def rows_of(c):
        return slice(c * CHUNK, (c + 1) * CHUNK)

    def col(k, h):
        return slice(k * D_REC + h * D_HEAD, k * D_REC + (h + 1) * D_HEAD)

    def mixer_half(i, fill):
        cph = th // CHUNK
        pieces = [(c, h) for c in range(i * cph, (i + 1) * cph) for h in range(HEADS)]
        scores = []
        for c, h in pieces:
            rows = rows_of(c)
            sc_f = lax.dot_general(p_ref[rows, col(0, h)], p_ref[rows, col(1, h)],
                                   nt_dims, preferred_element_type=F32)
            sc_b = lax.dot_general(p_ref[rows, col(2, h)], p_ref[rows, col(3, h)],
                                   nt_dims, preferred_element_type=F32)
            scores.append((jnp.where(lower, sc_f, 0.0)
                           + jnp.where(upper, sc_b, 0.0)).astype(BF16))
        fill(0)
        outs = []
        for (c, h), sc in zip(pieces, scores):
            rows = rows_of(c)
            s_i, c_i = c // cps, c % cps
            s_cat = jnp.concatenate([smf_ref[s_i, c_i, h], smb_ref[s_i, c_i, h]], axis=0)
            q_cat = jnp.concatenate([p_ref[rows, col(0, h)], p_ref[rows, col(2, h)]],
                                    axis=1)
            o = jnp.dot(sc, p_ref[rows, col(4, h)], preferred_element_type=F32)
            outs.append(o + jnp.dot(q_cat, s_cat, preferred_element_type=F32))
        fill(1)
        for (c, h), o in zip(pieces, outs):
            rows = rows_of(c)
            ms = jnp.mean(o * o, axis=-1, keepdims=True)
            on = o * lax.rsqrt(ms + RMS_EPS) * gn
            gs = p_ref[rows, col(5, h)]
            yr_scr[rows, h * D_HEAD:(h + 1) * D_HEAD] = (on * gs.astype(F32)).astype(BF16)

    def out_proj(hr):
        y = jnp.dot(yr_scr[hr, :], wo_ref[0:D_REC, :], preferred_element_type=F32)
        y = y + jnp.dot(yp_ref[hr, :], wo_ref[D_REC:2 * D_REC, :],
                        preferred_element_type=F32)
        return ALPHA * x_ref[hr, :] + g1_ref[0] * y

    def norm1_block(i, m):
        rows = slice(i * th + m * ROWS, i * th + (m + 1) * ROWS)
        x1 = _layer_norm(u_scr[rows, :], l1g_ref[...], l1b_ref[...])
        u_scr[rows, :] = x1
        h2_scr[rows, :] = (x1 * (1.0 + sc2_ref[0]) + sh2_ref[0]).astype(BF16)

    def norm2_block(i, m):
        rows = slice(i * th + m * ROWS, i * th + (m + 1) * ROWS)
        o_ref[rows, :] = _layer_norm(u_scr[rows, :], l2g_ref[...], l2b_ref[...])

    def ffn_half(i, fill):
        hr = half[i]
        h2 = h2_scr[hr, :]
        ffn = None
        j = 0
        for k in range(N_FF_SPLIT):
            a = jnp.dot(h2, wg_ref[:, k * fs:(k + 1) * fs], preferred_element_type=F32)
            fill(j); j += 1
            b = jnp.dot(h2, wu_ref[:, k * fs:(k + 1) * fs], preferred_element_type=F32)
            fill(j); j += 1
            act = (a * jax.nn.sigmoid(a) * b).astype(BF16)
            part = jnp.dot(act, wd_ref[k * fs:(k + 1) * fs, :],
                           preferred_element_type=F32)
            ffn = part if ffn is None else ffn + part
        u_scr[hr, :] = ALPHA * u_scr[hr, :] + g2_ref[0] * ffn

    def norm1_a(j):
        for m in range(j * n_rb // 2, (j + 1) * n_rb // 2):
            norm1_block(0, m)

    mixer_half(0, lambda j: None)
    u_scr[half[0], :] = out_proj(half[0])
    mixer_half(1, norm1_a)
    u_scr[half[1], :] = out_proj(half[1])
    ffn_half(0, lambda j: norm1_block(1, j) if j < n_rb else None)
    ffn_half(1, lambda j: norm2_block(0, j) if j < n_rb else None)
    for m in range(n_rb):
        norm2_block(1, m)


def _tail_call(x2, p, smf, smb, yp, mod3, mod_row, gn, wo, l1g, l1b, wg, wu, wd,
               l2g, l2b, seq_len):
    n_tok = x2.shape[0]
    tm = TM_TAIL
    cpt = tm // CHUNK
    cps = min(cpt, seq_len // CHUNK)
    spt = cpt // cps
    tps = max(1, seq_len // tm)
    sm_blk = (spt, cps, HEADS, D_HEAD, D_HEAD)

    def mod_spec(col):
        return pl.BlockSpec((1, 1, D_MODEL), lambda i: (mod_row(i, tps), 0, col))

    if spt > 1:
        sm_spec = pl.BlockSpec(sm_blk, lambda i: (i, 0, 0, 0, 0))
    else:
        sm_spec = pl.BlockSpec(sm_blk, lambda i: (i // tps, i % tps, 0, 0, 0))
    return pl.pallas_call(
        functools.partial(_tail_kernel, cps=cps),
        grid=(n_tok // tm,),
        in_specs=[
            pl.BlockSpec((tm, D_MODEL), lambda i: (i, 0)),
            pl.BlockSpec((tm, 6 * D_REC), lambda i: (i, 0)),
            sm_spec, sm_spec,
            pl.BlockSpec((tm, D_POOL), lambda i: (i, 0)),
            mod_spec(2), mod_spec(3), mod_spec(4), mod_spec(5),
            _const_spec((1, D_HEAD)),
            _const_spec((D_MODEL, D_MODEL)),
            _const_spec((1, D_MODEL)), _const_spec((1, D_MODEL)),
            _const_spec((D_MODEL, D_FF)), _const_spec((D_MODEL, D_FF)),
            _const_spec((D_FF, D_MODEL)),
            _const_spec((1, D_MODEL)), _const_spec((1, D_MODEL)),
        ],
        out_specs=pl.BlockSpec((tm, D_MODEL), lambda i: (i, 0)),
        out_shape=jax.ShapeDtypeStruct((n_tok, D_MODEL), F32),
        scratch_shapes=[
            pltpu.VMEM((tm, D_REC), BF16),
            pltpu.VMEM((tm, D_MODEL), F32),
            pltpu.VMEM((tm, D_MODEL), BF16),
        ],
        compiler_params=_params(1),
        name="tail",
    )(x2, p, smf, smb, yp, mod3, mod3, mod3, mod3, gn, wo, l1g, l1b, wg, wu, wd,
      l2g, l2b)


def kernel(x_prompt, x_sample, state_hgrn_fwd, state_hgrn_bwd, c, c_ctx, w_mod, b_mod,
           w_in, lb_fwd_raw, lb_bwd_raw, hgrn_norm_g, pool_w, pool_scale, w_out, ln1_g,
           ln1_b, w_ffn_gate, w_ffn_up, w_ffn_down, ln2_g, ln2_b):
    n_ctx, t_ctx, _ = x_prompt.shape
    n_lat, t_lat, _ = x_sample.shape
    ctx_row = n_lat

    c_all = jnp.concatenate(
        [c, c_ctx[None, :], jnp.zeros((16 - n_lat - 1, D_MODEL), F32)], axis=0)
    mod = _mod_call(c_all, w_mod[0], b_mod)
    mod3 = mod.reshape(16, 1, 6 * D_MODEL)

    w_in_b = w_in[0].astype(BF16)
    wo_b = w_out[0].astype(BF16)
    wg_b = w_ffn_gate[0].astype(BF16)
    wu_b = w_ffn_up[0].astype(BF16)
    wd_b = w_ffn_down[0].astype(BF16)
    pw_b = pool_w[0].astype(BF16)

    def run(x, n_seq, seq_len, mod_row, two_d, s0f, s0b, emit_final):
        x2 = x.reshape(n_seq * seq_len, D_MODEL)
        p, e = _proj_call(x2, mod3, mod_row, w_in_b, lb_fwd_raw, lb_bwd_raw, n_seq, seq_len)
        scan = _scan_call(p, e, n_seq, seq_len, s0f, s0b, emit_final)
        yp = _pool_call(p, pw_b, pool_scale, n_seq, seq_len, two_d)
        y = _tail_call(x2, p, scan[0], scan[1], yp, mod3, mod_row, hgrn_norm_g, wo_b,
                       ln1_g, ln1_b, wg_b, wu_b, wd_b, ln2_g, ln2_b, seq_len)
        return y.reshape(n_seq, seq_len, D_MODEL), scan[2:]

    y_ctx, fin = run(x_prompt, n_ctx, t_ctx, lambda i, tps: ctx_row, False,
                     None, None, True)
    y_lat, _ = run(x_sample, n_lat, t_lat, lambda i, tps: i // tps, True,
                   state_hgrn_fwd[:, 0], state_hgrn_bwd[:, 0], False)
    return (y_ctx, y_lat, fin[0][:, None], fin[1][:, None])
```

```python
import functools

import numpy as np
import jax
import jax.numpy as jnp
from jax import lax
from jax.experimental import pallas as pl
from jax.experimental.pallas import tpu as pltpu

F32 = jnp.float32
BF16 = jnp.bfloat16

D_MODEL = 1024
D_REC = 512
D_POOL = 512
HEADS = 4
D_HEAD = 128
D_IN = 5 * D_REC + D_POOL
D_FF = 2816
GRID_W = 64
POOL_WINDOWS = (2, 4, 8, 16)
POOL_GW = 128
ALPHA = 2.0 ** 0.25
LN_EPS = 1e-5
RMS_EPS = 1e-6

CHUNK = 64
MID_F = CHUNK // 2 - 1
MID_B = CHUNK // 2
TM_PROJ = 1024
TB_SCAN = 1024
SEQ_PER_SCAN = 4
TM_TAIL = 512
ROWS = 64
N_FF_SPLIT = 2
P_COLS = 7 * D_REC
E_COLS = 4 * D_REC
VMEM_LIMIT = 56 * 1024 * 1024

_ARB = "arbitrary"


def _params(n_axes):
    return pltpu.CompilerParams(
        dimension_semantics=(_ARB,) * n_axes, vmem_limit_bytes=VMEM_LIMIT)


def _const_spec(shape):
    nd = len(shape)
    return pl.BlockSpec(shape, lambda *_: (0,) * nd, pipeline_mode=pl.Buffered(1))


def _mod_kernel(c_ref, w_ref, b_ref, o_ref):
    c = c_ref[...]
    hi, lo = _split_bf16(c * jax.nn.sigmoid(c))
    w = w_ref[...].astype(BF16)
    o_ref[...] = (jnp.dot(hi, w, preferred_element_type=F32)
                  + jnp.dot(lo, w, preferred_element_type=F32) + b_ref[...])


def _mod_call(c_all, w_mod, b_mod):
    rows = c_all.shape[0]
    n_out = w_mod.shape[1]
    blk = 1024
    return pl.pallas_call(
        _mod_kernel,
        grid=(n_out // blk,),
        in_specs=[
            pl.BlockSpec((rows, D_MODEL), lambda j: (0, 0)),
            pl.BlockSpec((D_MODEL, blk), lambda j: (0, j)),
            pl.BlockSpec((1, blk), lambda j: (0, j)),
        ],
        out_specs=pl.BlockSpec((rows, blk), lambda j: (0, j)),
        out_shape=jax.ShapeDtypeStruct((rows, n_out), F32),
        compiler_params=_params(1),
        name="mod",
    )(c_all, w_mod, b_mod)


def _lower_bound(raw_ref):
    r0 = raw_ref[0:1, :]
    r1 = raw_ref[1:2, :]
    mx = jnp.maximum(r0, r1)
    e0 = jnp.exp(r0 - mx)
    e1 = jnp.exp(r1 - mx)
    return e0 / (e0 + e1)


def _split_bf16(a):
    hi = a.astype(BF16)
    lo = (a - hi.astype(F32)).astype(BF16)
    return hi, lo


def _proj_kernel(x_ref, sh_ref, sc_ref, w_ref, lbf_ref, lbb_ref, p_ref, e_ref,
                 proj_scr, *, cps):
    tm = x_ref.shape[0]
    h = (x_ref[...] * (1.0 + sc_ref[0]) + sh_ref[0]).astype(BF16)
    proj_scr[...] = jnp.dot(h, w_ref[...], preferred_element_type=F32)

    lb_f = _lower_bound(lbf_ref)
    lb_b = _lower_bound(lbb_ref)
    ri = lax.broadcasted_iota(jnp.int32, (CHUNK, CHUNK), 0)
    ci = lax.broadcasted_iota(jnp.int32, (CHUNK, CHUNK), 1)
    tri_pre = jnp.where(ci <= ri, 1.0, 0.0).astype(BF16)
    tri_suf = jnp.where(ci >= ri, 1.0, 0.0).astype(BF16)

    def chunk_sum(tri, a):
        hi, lo = _split_bf16(a)
        return (jnp.dot(tri, hi, preferred_element_type=F32)
                + jnp.dot(tri, lo, preferred_element_type=F32))

    for c in range(tm // CHUNK):
        rows = slice(c * CHUNK, (c + 1) * CHUNK)
        q = proj_scr[rows, 0:D_REC]
        qs = q * jax.nn.sigmoid(q)

        def direction(z, lb, tri, mid, end):
            f = lb + (1.0 - lb) * jax.nn.sigmoid(z)
            k = 1.0 - f
            b = chunk_sum(tri, jnp.log(f))
            m = b[mid:mid + 1, :]
            d = b - m
            qd = qs * jnp.exp(d)
            ki = k * jnp.exp(-d)
            em = jnp.exp(m)
            r = jnp.exp(b[end:end + 1, :] - m)
            return qd, ki, em, r

        qd_f, ki_f, em_f, r_f = direction(
            proj_scr[rows, D_REC:2 * D_REC], lb_f, tri_pre, MID_F, CHUNK - 1)
        qd_b, ki_b, em_b, r_b = direction(
            proj_scr[rows, 2 * D_REC:3 * D_REC], lb_b, tri_suf, MID_B, 0)
        g = proj_scr[rows, 4 * D_REC:5 * D_REC]

        p_ref[rows, 0 * D_REC:1 * D_REC] = qd_f.astype(BF16)
        p_ref[rows, 1 * D_REC:2 * D_REC] = ki_f.astype(BF16)
        p_ref[rows, 2 * D_REC:3 * D_REC] = qd_b.astype(BF16)
        p_ref[rows, 3 * D_REC:4 * D_REC] = ki_b.astype(BF16)
        p_ref[rows, 4 * D_REC:5 * D_REC] = proj_scr[rows, 3 * D_REC:4 * D_REC].astype(BF16)
        p_ref[rows, 5 * D_REC:6 * D_REC] = (g * jax.nn.sigmoid(g)).astype(BF16)
        p_ref[rows, 6 * D_REC:7 * D_REC] = proj_scr[rows, 5 * D_REC:6 * D_REC].astype(BF16)

        s, cc = c // cps, c % cps
        e_ref[s, cc:cc + 1, 0 * D_REC:1 * D_REC] = em_f
        e_ref[s, cc:cc + 1, 1 * D_REC:2 * D_REC] = r_f
        e_ref[s, cc:cc + 1, 2 * D_REC:3 * D_REC] = em_b
        e_ref[s, cc:cc + 1, 3 * D_REC:4 * D_REC] = r_b


def _proj_call(x2, mod3, mod_row, w_in, lbf_raw, lbb_raw, n_seq, seq_len):
    n_tok = x2.shape[0]
    tm = TM_PROJ
    cpt = tm // CHUNK
    cps = min(cpt, seq_len // CHUNK)
    spt = cpt // cps
    tps = max(1, seq_len // tm)
    n_ck = seq_len // CHUNK

    if spt > 1:
        e_spec = pl.BlockSpec((spt, cps, E_COLS), lambda i: (i, 0, 0))
    else:
        e_spec = pl.BlockSpec((1, cps, E_COLS), lambda i: (i // tps, i % tps, 0))

    return pl.pallas_call(
        functools.partial(_proj_kernel, cps=cps),
        grid=(n_tok // tm,),
        in_specs=[
            pl.BlockSpec((tm, D_MODEL), lambda i: (i, 0)),
            pl.BlockSpec((1, 1, D_MODEL), lambda i: (mod_row(i, tps), 0, 0)),
            pl.BlockSpec((1, 1, D_MODEL), lambda i: (mod_row(i, tps), 0, 1)),
            _const_spec((D_MODEL, D_IN)),
            _const_spec((2, D_REC)),
            _const_spec((2, D_REC)),
        ],
        out_specs=[
            pl.BlockSpec((tm, P_COLS), lambda i: (i, 0)),
            e_spec,
        ],
        out_shape=[
            jax.ShapeDtypeStruct((n_tok, P_COLS), BF16),
            jax.ShapeDtypeStruct((n_seq, n_ck, E_COLS), F32),
        ],
        scratch_shapes=[pltpu.VMEM((tm, D_IN), F32)],
        compiler_params=_params(1),
        name="proj",
    )(x2, mod3, mod3, w_in, lbf_raw, lbb_raw)


def _scan_kernel(*refs, nck, spb, has_init, emit_final):
    it = iter(refs)
    kif_ref, vf_ref, ef_ref, kib_ref, vb_ref, eb_ref = (next(it) for _ in range(6))
    if has_init:
        s0f_ref, s0b_ref = next(it), next(it)
    smf_ref, smb_ref = next(it), next(it)
    if emit_final:
        finf_ref, finb_ref = next(it), next(it)
    sf_scr, sb_scr, uf_scr, ub_scr = (next(it) for _ in range(4))

    j = pl.program_id(1)
    seq_heads = [(q, h) for q in range(spb) for h in range(HEADS)]

    @pl.when(j == 0)
    def _():
        for q, h in seq_heads:
            if has_init:
                sf_scr[q, h] = s0f_ref[q, h].T
                sb_scr[q, h] = s0b_ref[q, h].T
            else:
                sf_scr[q, h] = jnp.zeros((D_HEAD, D_HEAD), F32)
                sb_scr[q, h] = jnp.zeros((D_HEAD, D_HEAD), F32)

    def products(ki_ref, v_ref, u_scr):
        for q in range(spb):
            for c in range(nck):
                rows = slice((q * nck + c) * CHUNK, (q * nck + c + 1) * CHUNK)
                for h in range(HEADS):
                    cols = slice(h * D_HEAD, (h + 1) * D_HEAD)
                    u_scr[q, c, h] = lax.dot_general(
                        v_ref[rows, cols], ki_ref[rows, cols], (((0,), (0,)), ((), ())),
                        preferred_element_type=F32)

    def step(e_ref, sm_ref, s_scr, u_scr, c):
        for q, h in seq_heads:
            em = e_ref[q, c:c + 1, h * D_HEAD:(h + 1) * D_HEAD]
            r = e_ref[q, c:c + 1, D_REC + h * D_HEAD:D_REC + (h + 1) * D_HEAD]
            sm = s_scr[q, h] * em
            sm_ref[q, c, h] = sm.astype(BF16).T
            s_scr[q, h] = (sm + u_scr[q, c, h]) * r

    products(kif_ref, vf_ref, uf_scr)
    products(kib_ref, vb_ref, ub_scr)
    for c in range(nck):
        step(ef_ref, smf_ref, sf_scr, uf_scr, c)
        step(eb_ref, smb_ref, sb_scr, ub_scr, nck - 1 - c)

    if emit_final:
        @pl.when(j == pl.num_programs(1) - 1)
        def _():
            for q, h in seq_heads:
                finf_ref[q, h] = sf_scr[q, h].T
                finb_ref[q, h] = sb_scr[q, h].T


def _scan_call(p, e, n_seq, seq_len, s0f=None, s0b=None, emit_final=False):
    tb = min(seq_len, TB_SCAN)
    nt = seq_len // tb
    spb = SEQ_PER_SCAN if nt == 1 else 1
    nck = tb // CHUNK
    n_ck = seq_len // CHUNK
    has_init = s0f is not None
    st_blk = (spb, HEADS, D_HEAD, D_HEAD)
    rows = spb * tb

    in_specs = [
        pl.BlockSpec((rows, D_REC), lambda b, j: (b * nt + j, 1)),
        pl.BlockSpec((rows, D_REC), lambda b, j: (b * nt + j, 4)),
        pl.BlockSpec((spb, nck, 2 * D_REC), lambda b, j: (b, j, 0)),
        pl.BlockSpec((rows, D_REC), lambda b, j: (b * nt + nt - 1 - j, 3)),
        pl.BlockSpec((rows, D_REC), lambda b, j: (b * nt + nt - 1 - j, 4)),
        pl.BlockSpec((spb, nck, 2 * D_REC), lambda b, j: (b, nt - 1 - j, 1)),
    ]
    args = [p, p, e, p, p, e]
    if has_init:
        in_specs += [pl.BlockSpec(st_blk, lambda b, j: (b, 0, 0, 0))] * 2
        args += [s0f, s0b]
    sm_shape = jax.ShapeDtypeStruct((n_seq, n_ck, HEADS, D_HEAD, D_HEAD), BF16)
    sm_blk = (spb, nck, HEADS, D_HEAD, D_HEAD)
    out_specs = [
        pl.BlockSpec(sm_blk, lambda b, j: (b, j, 0, 0, 0)),
        pl.BlockSpec(sm_blk, lambda b, j: (b, nt - 1 - j, 0, 0, 0)),
    ]
    out_shape = [sm_shape, sm_shape]
    if emit_final:
        out_specs += [pl.BlockSpec(st_blk, lambda b, j: (b, 0, 0, 0))] * 2
        out_shape += [jax.ShapeDtypeStruct((n_seq, HEADS, D_HEAD, D_HEAD), F32)] * 2

    return pl.pallas_call(
        functools.partial(_scan_kernel, nck=nck, spb=spb, has_init=has_init,
                          emit_final=emit_final),
        grid=(n_seq // spb, nt),
        in_specs=in_specs,
        out_specs=out_specs,
        out_shape=out_shape,
        scratch_shapes=([pltpu.VMEM((spb, HEADS, D_HEAD, D_HEAD), F32)] * 2
                        + [pltpu.VMEM((spb, nck, HEADS, D_HEAD, D_HEAD), F32)] * 2),
        compiler_params=_params(2),
        name="scan",
    )(*args)


def _window(n, w):
    idx = np.arange(n)
    return np.clip(idx - w // 2, 0, n), np.clip(idx + w // 2, 0, n)


def _band(n, w):
    lo, hi = _window(n, w)
    j = np.arange(n)[None, :]
    return ((j >= lo[:, None]) & (j < hi[:, None])).astype(np.float32), (hi - lo)


def _pool_seq_kernel(u_ref, a_ref, inv_ref, pw_ref, ps_ref, o_ref, diff_scr, *,
                     n_seq_blk, seq_len):
    for gi in range(len(POOL_WINDOWS)):
        cols = slice(gi * POOL_GW, (gi + 1) * POOL_GW)
        for s in range(n_seq_blk):
            rows = slice(s * seq_len, (s + 1) * seq_len)
            ug = u_ref[rows, cols]
            tot = jnp.dot(a_ref[gi], ug, preferred_element_type=F32)
            diff_scr[rows, cols] = (tot * inv_ref[gi] - ug.astype(F32)).astype(BF16)
    for gi in range(len(POOL_WINDOWS)):
        cols = slice(gi * POOL_GW, (gi + 1) * POOL_GW)
        y = jnp.dot(diff_scr[:, cols], pw_ref[gi], preferred_element_type=F32)
        o_ref[:, cols] = (y * ps_ref[:, cols]).astype(BF16)


def _pool_grid_kernel(u_ref, a_ref, invc_ref, pw_ref, ps_ref, o_ref, col_scr, diff_scr,
                      *, n_rows):
    blk = a_ref.shape[1]
    rpb = blk // GRID_W
    for gi, w in enumerate(POOL_WINDOWS):
        cols = slice(gi * POOL_GW, (gi + 1) * POOL_GW)
        for rb in range(n_rows // rpb):
            rows = slice(rb * blk, (rb + 1) * blk)
            tot = jnp.dot(a_ref[gi], u_ref[rows, cols], preferred_element_type=F32)
            col_scr[gi, rows, :] = tot * invc_ref[gi]
    for gi, w in enumerate(POOL_WINDOWS):
        cols = slice(gi * POOL_GW, (gi + 1) * POOL_GW)
        lo, hi = _window(n_rows, w)

        def grid_row(rr):
            return col_scr[gi, rr * GRID_W:(rr + 1) * GRID_W, :]

        acc = None
        for r in range(n_rows):
            if r == 0:
                acc = grid_row(int(lo[0]))
                for rr in range(int(lo[0]) + 1, int(hi[0])):
                    acc = acc + grid_row(rr)
            else:
                for rr in range(int(hi[r - 1]), int(hi[r])):
                    acc = acc + grid_row(rr)
                for rr in range(int(lo[r - 1]), int(lo[r])):
                    acc = acc - grid_row(rr)
            rows = slice(r * GRID_W, (r + 1) * GRID_W)
            mean = acc * (1.0 / float(hi[r] - lo[r]))
            diff_scr[rows, cols] = (mean - u_ref[rows, cols].astype(F32)).astype(BF16)
    for gi in range(len(POOL_WINDOWS)):
        cols = slice(gi * POOL_GW, (gi + 1) * POOL_GW)
        y = jnp.dot(diff_scr[:, cols], pw_ref[gi], preferred_element_type=F32)
        o_ref[:, cols] = (y * ps_ref[:, cols]).astype(BF16)


def _pool_call(p, pool_w, pool_scale, n_seq, seq_len, two_d):
    n_tok = p.shape[0]
    ng = len(POOL_WINDOWS)
    if two_d:
        n_rows = seq_len // GRID_W
        rpb = 4
        blk = rpb * GRID_W
        bands = [_band(GRID_W, w) for w in POOL_WINDOWS]
        a = np.stack([np.kron(np.eye(rpb, dtype=np.float32), b[0]) for b in bands])
        inv = np.stack([np.broadcast_to(np.tile(1.0 / b[1], rpb)[:, None], (blk, POOL_GW))
                        for b in bands]).astype(np.float32)
        kern = functools.partial(_pool_grid_kernel, n_rows=n_rows)
        tb = seq_len
        scratch = [pltpu.VMEM((ng, seq_len, POOL_GW), F32),
                   pltpu.VMEM((seq_len, D_POOL), BF16)]
    else:
        blk = seq_len
        bands = [_band(seq_len, w) for w in POOL_WINDOWS]
        a = np.stack([b[0] for b in bands])
        inv = np.stack([np.broadcast_to((1.0 / b[1])[:, None], (blk, POOL_GW))
                        for b in bands]).astype(np.float32)
        n_seq_blk = 8
        kern = functools.partial(_pool_seq_kernel, n_seq_blk=n_seq_blk, seq_len=seq_len)
        tb = n_seq_blk * seq_len
        scratch = [pltpu.VMEM((tb, D_POOL), BF16)]
    return pl.pallas_call(
        kern,
        grid=(n_tok // tb,),
        in_specs=[
            pl.BlockSpec((tb, D_POOL), lambda i: (i, 6)),
            _const_spec((ng, blk, blk)),
            _const_spec((ng, blk, POOL_GW)),
            _const_spec((ng, POOL_GW, POOL_GW)),
            _const_spec((1, D_POOL)),
        ],
        out_specs=pl.BlockSpec((tb, D_POOL), lambda i: (i, 0)),
        out_shape=jax.ShapeDtypeStruct((n_tok, D_POOL), BF16),
        scratch_shapes=scratch,
        compiler_params=_params(1),
        name="pool",
    )(p, jnp.asarray(a, BF16), jnp.asarray(inv, F32), pool_w, pool_scale)


def _layer_norm(u, g, b):
    mu = jnp.mean(u, axis=-1, keepdims=True)
    d = u - mu
    var = jnp.mean(d * d, axis=-1, keepdims=True)
    return d * lax.rsqrt(var + LN_EPS) * g + b


def _tail_kernel(x_ref, p_ref, smf_ref, smb_ref, yp_ref, g1_ref, sh2_ref, sc2_ref,
                 g2_ref, gn_ref, wo_ref, l1g_ref, l1b_ref, wg_ref, wu_ref, wd_ref,
                 l2g_ref, l2b_ref, o_ref, yr_scr, u_scr, h2_scr, *, cps):
    tm = x_ref.shape[0]
    ri = lax.broadcasted_iota(jnp.int32, (CHUNK, CHUNK), 0)
    ci = lax.broadcasted_iota(jnp.int32, (CHUNK, CHUNK), 1)
    lower = ci <= ri
    upper = ci >= ri
    nt_dims = (((1,), (1,)), ((), ()))
    gn = gn_ref[...]

    th = tm // 2
    half = [slice(0, th), slice(th, tm)]
    n_rb = th // ROWS
    fs = D_FF // N_FF_SPLIT
    assert n_rb == 2 * N_FF_SPLIT

    def rows_of(c):
        return slice(c * CHUNK, (c + 1) * CHUNK)

    def col(k, h):
        return slice(k * D_REC + h * D_HEAD, k * D_REC + (h + 1) * D_HEAD)

    def mixer_half(i, fill):
        cph = th // CHUNK
        pieces = [(c, h) for c in range(i * cph, (i + 1) * cph) for h in range(HEADS)]
        scores = []
        for c, h in pieces:
            rows = rows_of(c)
            sc_f = lax.dot_general(p_ref[rows, col(0, h)], p_ref[rows, col(1, h)],
                                   nt_dims, preferred_element_type=F32)
            sc_b = lax.dot_general(p_ref[rows, col(2, h)], p_ref[rows, col(3, h)],
                                   nt_dims, preferred_element_type=F32)
            scores.append((jnp.where(lower, sc_f, 0.0)
                           + jnp.where(upper, sc_b, 0.0)).astype(BF16))
        fill(0)
        outs = []
        for (c, h), sc in zip(pieces, scores):
            rows = rows_of(c)
            s_i, c_i = c // cps, c % cps
            s_cat = jnp.concatenate([smf_ref[s_i, c_i, h], smb_ref[s_i, c_i, h]], axis=0)
            q_cat = jnp.concatenate([p_ref[rows, col(0, h)], p_ref[rows, col(2, h)]],
                                    axis=1)
            o = jnp.dot(sc, p_ref[rows, col(4, h)], preferred_element_type=F32)
            outs.append(o + jnp.dot(q_cat, s_cat, preferred_element_type=F32))
        fill(1)
        for (c, h), o in zip(pieces, outs):
            rows = rows_of(c)
            ms = jnp.mean(o * o, axis=-1, keepdims=True)
            on = o * lax.rsqrt(ms + RMS_EPS) * gn
            gs = p_ref[rows, col(5, h)]
            yr_scr[rows, h * D_HEAD:(h + 1) * D_HEAD] = (on * gs.astype(F32)).astype(BF16)

    def out_proj(hr):
        y = jnp.dot(yr_scr[hr, :], wo_ref[0:D_REC, :], preferred_element_type=F32)
        y = y + jnp.dot(yp_ref[hr, :], wo_ref[D_REC:2 * D_REC, :],
                        preferred_element_type=F32)
        return ALPHA * x_ref[hr, :] + g1_ref[0] * y

    def norm1_block(i, m):
        rows = slice(i * th + m * ROWS, i * th + (m + 1) * ROWS)
        x1 = _layer_norm(u_scr[rows, :], l1g_ref[...], l1b_ref[...])
        u_scr[rows, :] = x1
        h2_scr[rows, :] = (x1 * (1.0 + sc2_ref[0]) + sh2_ref[0]).astype(BF16)

    def norm2_block(i, m):
        rows = slice(i * th + m * ROWS, i * th + (m + 1) * ROWS)
        o_ref[rows, :] = _layer_norm(u_scr[rows, :], l2g_ref[...], l2b_ref[...])

    def ffn_half(i, fill):
        hr = half[i]
        h2 = h2_scr[hr, :]
        ffn = None
        j = 0
        for k in range(N_FF_SPLIT):
            a = jnp.dot(h2, wg_ref[:, k * fs:(k + 1) * fs], preferred_element_type=F32)
            fill(j); j += 1
            b = jnp.dot(h2, wu_ref[:, k * fs:(k + 1) * fs], preferred_element_type=F32)
            fill(j); j += 1
            act = (a * jax.nn.sigmoid(a) * b).astype(BF16)
            part = jnp.dot(act, wd_ref[k * fs:(k + 1) * fs, :],
                           preferred_element_type=F32)
            ffn = part if ffn is None else ffn + part
        u_scr[hr, :] = ALPHA * u_scr[hr, :] + g2_ref[0] * ffn

    def norm1_a(j):
        for m in range(j * n_rb // 2, (j + 1) * n_rb // 2):
            norm1_block(0, m)

    mixer_half(0, lambda j: None)
    u_scr[half[0], :] = out_proj(half[0])
    mixer_half(1, norm1_a)
    u_scr[half[1], :] = out_proj(half[1])
    ffn_half(0, lambda j: norm1_block(1, j))
    ffn_half(1, lambda j: norm2_block(0, j))
    for m in range(n_rb):
        norm2_block(1, m)


def _tail_call(x2, p, smf, smb, yp, mod3, mod_row, gn, wo, l1g, l1b, wg, wu, wd,
               l2g, l2b, seq_len):
    n_tok = x2.shape[0]
    tm = TM_TAIL
    cpt = tm // CHUNK
    cps = min(cpt, seq_len // CHUNK)
    spt = cpt // cps
    tps = max(1, seq_len // tm)
    sm_blk = (spt, cps, HEADS, D_HEAD, D_HEAD)

    def mod_spec(col):
        return pl.BlockSpec((1, 1, D_MODEL), lambda i: (mod_row(i, tps), 0, col))

    if spt > 1:
        sm_spec = pl.BlockSpec(sm_blk, lambda i: (i, 0, 0, 0, 0))
    else:
        sm_spec = pl.BlockSpec(sm_blk, lambda i: (i // tps, i % tps, 0, 0, 0))
    return pl.pallas_call(
        functools.partial(_tail_kernel, cps=cps),
        grid=(n_tok // tm,),
        in_specs=[
            pl.BlockSpec((tm, D_MODEL), lambda i: (i, 0)),
            pl.BlockSpec((tm, 6 * D_REC), lambda i: (i, 0)),
            sm_spec, sm_spec,
            pl.BlockSpec((tm, D_POOL), lambda i: (i, 0)),
            mod_spec(2), mod_spec(3), mod_spec(4), mod_spec(5),
            _const_spec((1, D_HEAD)),
            _const_spec((D_MODEL, D_MODEL)),
            _const_spec((1, D_MODEL)), _const_spec((1, D_MODEL)),
            _const_spec((D_MODEL, D_FF)), _const_spec((D_MODEL, D_FF)),
            _const_spec((D_FF, D_MODEL)),
            _const_spec((1, D_MODEL)), _const_spec((1, D_MODEL)),
        ],
        out_specs=pl.BlockSpec((tm, D_MODEL), lambda i: (i, 0)),
        out_shape=jax.ShapeDtypeStruct((n_tok, D_MODEL), F32),
        scratch_shapes=[
            pltpu.VMEM((tm, D_REC), BF16),
            pltpu.VMEM((tm, D_MODEL), F32),
            pltpu.VMEM((tm, D_MODEL), BF16),
        ],
        compiler_params=_params(1),
        name="tail",
    )(x2, p, smf, smb, yp, mod3, mod3, mod3, mod3, gn, wo, l1g, l1b, wg, wu, wd,
      l2g, l2b)


def kernel(x_prompt, x_sample, state_hgrn_fwd, state_hgrn_bwd, c, c_ctx, w_mod, b_mod,
           w_in, lb_fwd_raw, lb_bwd_raw, hgrn_norm_g, pool_w, pool_scale, w_out, ln1_g,
           ln1_b, w_ffn_gate, w_ffn_up, w_ffn_down, ln2_g, ln2_b):
    n_ctx, t_ctx, _ = x_prompt.shape
    n_lat, t_lat, _ = x_sample.shape
    ctx_row = n_lat

    c_all = jnp.concatenate(
        [c, c_ctx[None, :], jnp.zeros((16 - n_lat - 1, D_MODEL), F32)], axis=0)
    mod = _mod_call(c_all, w_mod[0], b_mod)
    mod3 = mod.reshape(16, 1, 6 * D_MODEL)

    w_in_b = w_in[0].astype(BF16)
    wo_b = w_out[0].astype(BF16)
    wg_b = w_ffn_gate[0].astype(BF16)
    wu_b = w_ffn_up[0].astype(BF16)
    wd_b = w_ffn_down[0].astype(BF16)
    pw_b = pool_w[0].astype(BF16)

    def run(x, n_seq, seq_len, mod_row, two_d, s0f, s0b, emit_final):
        x2 = x.reshape(n_seq * seq_len, D_MODEL)
        p, e = _proj_call(x2, mod3, mod_row, w_in_b, lb_fwd_raw, lb_bwd_raw, n_seq, seq_len)
        scan = _scan_call(p, e, n_seq, seq_len, s0f, s0b, emit_final)
        yp = _pool_call(p, pw_b, pool_scale, n_seq, seq_len, two_d)
        y = _tail_call(x2, p, scan[0], scan[1], yp, mod3, mod_row, hgrn_norm_g, wo_b,
                       ln1_g, ln1_b, wg_b, wu_b, wd_b, ln2_g, ln2_b, seq_len)
        return y.reshape(n_seq, seq_len, D_MODEL), scan[2:]

    y_ctx, fin = run(x_prompt, n_ctx, t_ctx, lambda i, tps: ctx_row, False,
                     None, None, True)
    y_lat, _ = run(x_sample, n_lat, t_lat, lambda i, tps: i // tps, True,
                   state_hgrn_fwd[:, 0], state_hgrn_bwd[:, 0], False)
    return (y_ctx, y_lat, fin[0][:, None], fin[1][:, None])
```

```python
import functools

import numpy as np
import jax
import jax.numpy as jnp
from jax import lax
from jax.experimental import pallas as pl
from jax.experimental.pallas import tpu as pltpu

F32 = jnp.float32
BF16 = jnp.bfloat16

D_MODEL = 1024
D_REC = 512
D_POOL = 512
HEADS = 4
D_HEAD = 128
D_IN = 5 * D_REC + D_POOL
D_FF = 2816
GRID_W = 64
POOL_WINDOWS = (2, 4, 8, 16)
POOL_GW = 128
ALPHA = 2.0 ** 0.25
LN_EPS = 1e-5
RMS_EPS = 1e-6

CHUNK = 64
MID_F = CHUNK // 2 - 1
MID_B = CHUNK // 2
TM_PROJ = 1024
TB_SCAN = 1024
SEQ_PER_SCAN = 4
TM_TAIL = 512
ROWS = 64
N_FF_SPLIT = 2
P_COLS = 7 * D_REC
E_COLS = 4 * D_REC
VMEM_LIMIT = 56 * 1024 * 1024

_ARB = "arbitrary"


def _params(n_axes):
    return pltpu.CompilerParams(
        dimension_semantics=(_ARB,) * n_axes, vmem_limit_bytes=VMEM_LIMIT)


def _const_spec(shape):
    nd = len(shape)
    return pl.BlockSpec(shape, lambda *_: (0,) * nd, pipeline_mode=pl.Buffered(1))


def _mod_kernel(c_ref, w_ref, b_ref, o_ref):
    c = c_ref[...]
    hi, lo = _split_bf16(c * jax.nn.sigmoid(c))
    w = w_ref[...].astype(BF16)
    o_ref[...] = (jnp.dot(hi, w, preferred_element_type=F32)
                  + jnp.dot(lo, w, preferred_element_type=F32) + b_ref[...])


def _mod_call(c_all, w_mod, b_mod):
    rows = c_all.shape[0]
    n_out = w_mod.shape[1]
    blk = 1024
    return pl.pallas_call(
        _mod_kernel,
        grid=(n_out // blk,),
        in_specs=[
            pl.BlockSpec((rows, D_MODEL), lambda j: (0, 0)),
            pl.BlockSpec((D_MODEL, blk), lambda j: (0, j)),
            pl.BlockSpec((1, blk), lambda j: (0, j)),
        ],
        out_specs=pl.BlockSpec((rows, blk), lambda j: (0, j)),
        out_shape=jax.ShapeDtypeStruct((rows, n_out), F32),
        compiler_params=_params(1),
        name="mod",
    )(c_all, w_mod, b_mod)


def _lower_bound(raw_ref):
    r0 = raw_ref[0:1, :]
    r1 = raw_ref[1:2, :]
    mx = jnp.maximum(r0, r1)
    e0 = jnp.exp(r0 - mx)
    e1 = jnp.exp(r1 - mx)
    return e0 / (e0 + e1)


def _split_bf16(a):
    hi = a.astype(BF16)
    lo = (a - hi.astype(F32)).astype(BF16)
    return hi, lo


def _proj_kernel(x_ref, sh_ref, sc_ref, w_ref, lbf_ref, lbb_ref, p_ref, e_ref,
                 proj_scr, *, cps):
    tm = x_ref.shape[0]
    h = (x_ref[...] * (1.0 + sc_ref[0]) + sh_ref[0]).astype(BF16)
    proj_scr[...] = jnp.dot(h, w_ref[...], preferred_element_type=F32)

    lb_f = _lower_bound(lbf_ref)
    lb_b = _lower_bound(lbb_ref)
    ri = lax.broadcasted_iota(jnp.int32, (CHUNK, CHUNK), 0)
    ci = lax.broadcasted_iota(jnp.int32, (CHUNK, CHUNK), 1)
    tri_pre = jnp.where(ci <= ri, 1.0, 0.0).astype(BF16)
    tri_suf = jnp.where(ci >= ri, 1.0, 0.0).astype(BF16)

    def chunk_sum(tri, a):
        hi, lo = _split_bf16(a)
        return (jnp.dot(tri, hi, preferred_element_type=F32)
                + jnp.dot(tri, lo, preferred_element_type=F32))

    for c in range(tm // CHUNK):
        rows = slice(c * CHUNK, (c + 1) * CHUNK)
        q = proj_scr[rows, 0:D_REC]
        qs = q * jax.nn.sigmoid(q)

        def direction(z, lb, tri, mid, end):
            f = lb + (1.0 - lb) * jax.nn.sigmoid(z)
            k = 1.0 - f
            b = chunk_sum(tri, jnp.log(f))
            m = b[mid:mid + 1, :]
            d = b - m
            qd = qs * jnp.exp(d)
            ki = k * jnp.exp(-d)
            em = jnp.exp(m)
            r = jnp.exp(b[end:end + 1, :] - m)
            return qd, ki, em, r

        qd_f, ki_f, em_f, r_f = direction(
            proj_scr[rows, D_REC:2 * D_REC], lb_f, tri_pre, MID_F, CHUNK - 1)
        qd_b, ki_b, em_b, r_b = direction(
            proj_scr[rows, 2 * D_REC:3 * D_REC], lb_b, tri_suf, MID_B, 0)
        g = proj_scr[rows, 4 * D_REC:5 * D_REC]

        p_ref[rows, 0 * D_REC:1 * D_REC] = qd_f.astype(BF16)
        p_ref[rows, 1 * D_REC:2 * D_REC] = ki_f.astype(BF16)
        p_ref[rows, 2 * D_REC:3 * D_REC] = qd_b.astype(BF16)
        p_ref[rows, 3 * D_REC:4 * D_REC] = ki_b.astype(BF16)
        p_ref[rows, 4 * D_REC:5 * D_REC] = proj_scr[rows, 3 * D_REC:4 * D_REC].astype(BF16)
        p_ref[rows, 5 * D_REC:6 * D_REC] = (g * jax.nn.sigmoid(g)).astype(BF16)
        p_ref[rows, 6 * D_REC:7 * D_REC] = proj_scr[rows, 5 * D_REC:6 * D_REC].astype(BF16)

        s, cc = c // cps, c % cps
        e_ref[s, cc:cc + 1, 0 * D_REC:1 * D_REC] = em_f
        e_ref[s, cc:cc + 1, 1 * D_REC:2 * D_REC] = r_f
        e_ref[s, cc:cc + 1, 2 * D_REC:3 * D_REC] = em_b
        e_ref[s, cc:cc + 1, 3 * D_REC:4 * D_REC] = r_b


def _proj_call(x2, mod3, mod_row, w_in, lbf_raw, lbb_raw, n_seq, seq_len):
    n_tok = x2.shape[0]
    tm = TM_PROJ
    cpt = tm // CHUNK
    cps = min(cpt, seq_len // CHUNK)
    spt = cpt // cps
    tps = max(1, seq_len // tm)
    n_ck = seq_len // CHUNK

    if spt > 1:
        e_spec = pl.BlockSpec((spt, cps, E_COLS), lambda i: (i, 0, 0))
    else:
        e_spec = pl.BlockSpec((1, cps, E_COLS), lambda i: (i // tps, i % tps, 0))

    return pl.pallas_call(
        functools.partial(_proj_kernel, cps=cps),
        grid=(n_tok // tm,),
        in_specs=[
            pl.BlockSpec((tm, D_MODEL), lambda i: (i, 0)),
            pl.BlockSpec((1, 1, D_MODEL), lambda i: (mod_row(i, tps), 0, 0)),
            pl.BlockSpec((1, 1, D_MODEL), lambda i: (mod_row(i, tps), 0, 1)),
            _const_spec((D_MODEL, D_IN)),
            _const_spec((2, D_REC)),
            _const_spec((2, D_REC)),
        ],
        out_specs=[
            pl.BlockSpec((tm, P_COLS), lambda i: (i, 0)),
            e_spec,
        ],
        out_shape=[
            jax.ShapeDtypeStruct((n_tok, P_COLS), BF16),
            jax.ShapeDtypeStruct((n_seq, n_ck, E_COLS), F32),
        ],
        scratch_shapes=[pltpu.VMEM((tm, D_IN), F32)],
        compiler_params=_params(1),
        name="proj",
    )(x2, mod3, mod3, w_in, lbf_raw, lbb_raw)


def _scan_kernel(*refs, nck, spb, has_init, emit_final):
    it = iter(refs)
    qdf_ref, kif_ref, vf_ref, ef_ref = (next(it) for _ in range(4))
    qdb_ref, kib_ref, vb_ref, eb_ref = (next(it) for _ in range(4))
    if has_init:
        s0f_ref, s0b_ref = next(it), next(it)
    of_ref, ob_ref = next(it), next(it)
    if emit_final:
        finf_ref, finb_ref = next(it), next(it)
    sf_scr, sb_scr, uf_scr, ub_scr, scf_scr, scb_scr, tf_scr, tb_scr = (
        next(it) for _ in range(8))

    j = pl.program_id(1)
    seq_heads = [(q, h) for q in range(spb) for h in range(HEADS)]
    ri = lax.broadcasted_iota(jnp.int32, (CHUNK, CHUNK), 0)
    ci = lax.broadcasted_iota(jnp.int32, (CHUNK, CHUNK), 1)
    nt_dims = (((1,), (1,)), ((), ()))
    tn_dims = (((0,), (0,)), ((), ()))

    @pl.when(j == 0)
    def _():
        for q, h in seq_heads:
            if has_init:
                sf_scr[q, h] = s0f_ref[q, h].T
                sb_scr[q, h] = s0b_ref[q, h].T
            else:
                sf_scr[q, h] = jnp.zeros((D_HEAD, D_HEAD), F32)
                sb_scr[q, h] = jnp.zeros((D_HEAD, D_HEAD), F32)

    def tile(q, c, h):
        return (slice((q * nck + c) * CHUNK, (q * nck + c + 1) * CHUNK),
                slice(h * D_HEAD, (h + 1) * D_HEAD))

    def prepare(qd_ref, ki_ref, v_ref, keep, u_scr, sc_scr):
        for q in range(spb):
            for c in range(nck):
                for h in range(HEADS):
                    rows, cols = tile(q, c, h)
                    u_scr[q, c, h] = lax.dot_general(
                        v_ref[rows, cols], ki_ref[rows, cols], tn_dims,
                        preferred_element_type=F32)
                    sc = lax.dot_general(qd_ref[rows, cols], ki_ref[rows, cols], nt_dims,
                                         preferred_element_type=F32)
                    sc_scr[q, c, h] = jnp.where(keep, sc, 0.0).astype(BF16)

    def step(qd_ref, v_ref, e_ref, o_ref, s_scr, u_scr, sc_scr, smt_scr, c):
        for q, h in seq_heads:
            rows, cols = tile(q, c, h)
            em = e_ref[q, c:c + 1, h * D_HEAD:(h + 1) * D_HEAD]
            r = e_ref[q, c:c + 1, D_REC + h * D_HEAD:D_REC + (h + 1) * D_HEAD]
            sm = s_scr[q, h] * em
            smt_scr[q, h] = sm.astype(BF16).T
            o = jnp.dot(qd_ref[rows, cols], smt_scr[q, h], preferred_element_type=F32)
            o = o + jnp.dot(sc_scr[q, c, h], v_ref[rows, cols],
                            preferred_element_type=F32)
            o_ref[rows, cols] = o.astype(BF16)
            s_scr[q, h] = (sm + u_scr[q, c, h]) * r

    prepare(qdf_ref, kif_ref, vf_ref, ci <= ri, uf_scr, scf_scr)
    prepare(qdb_ref, kib_ref, vb_ref, ci >= ri, ub_scr, scb_scr)
    for c in range(nck):
        step(qdf_ref, vf_ref, ef_ref, of_ref, sf_scr, uf_scr, scf_scr, tf_scr, c)
        step(qdb_ref, vb_ref, eb_ref, ob_ref, sb_scr, ub_scr, scb_scr, tb_scr, nck - 1 - c)

    if emit_final:
        @pl.when(j == pl.num_programs(1) - 1)
        def _():
            for q, h in seq_heads:
                finf_ref[q, h] = sf_scr[q, h].T
                finb_ref[q, h] = sb_scr[q, h].T


def _scan_call(p, e, n_seq, seq_len, s0f=None, s0b=None, emit_final=False):
    tb = min(seq_len, TB_SCAN)
    nt = seq_len // tb
    spb = SEQ_PER_SCAN if nt == 1 else 1
    nck = tb // CHUNK
    has_init = s0f is not None
    st_blk = (spb, HEADS, D_HEAD, D_HEAD)
    rows = spb * tb
    n_tok = p.shape[0]

    def fwd(b, j):
        return b * nt + j

    def bwd(b, j):
        return b * nt + nt - 1 - j

    def p_spec(blk, col):
        return pl.BlockSpec((rows, D_REC), lambda b, j: (blk(b, j), col))

    in_specs = [
        p_spec(fwd, 0), p_spec(fwd, 1), p_spec(fwd, 4),
        pl.BlockSpec((spb, nck, 2 * D_REC), lambda b, j: (b, j, 0)),
        p_spec(bwd, 2), p_spec(bwd, 3), p_spec(bwd, 4),
        pl.BlockSpec((spb, nck, 2 * D_REC), lambda b, j: (b, nt - 1 - j, 1)),
    ]
    args = [p, p, p, e, p, p, p, e]
    if has_init:
        in_specs += [pl.BlockSpec(st_blk, lambda b, j: (b, 0, 0, 0))] * 2
        args += [s0f, s0b]
    o_shape = jax.ShapeDtypeStruct((n_tok, D_REC), BF16)
    out_specs = [
        pl.BlockSpec((rows, D_REC), lambda b, j: (fwd(b, j), 0)),
        pl.BlockSpec((rows, D_REC), lambda b, j: (bwd(b, j), 0)),
    ]
    out_shape = [o_shape, o_shape]
    if emit_final:
        out_specs += [pl.BlockSpec(st_blk, lambda b, j: (b, 0, 0, 0))] * 2
        out_shape += [jax.ShapeDtypeStruct((n_seq, HEADS, D_HEAD, D_HEAD), F32)] * 2

    return pl.pallas_call(
        functools.partial(_scan_kernel, nck=nck, spb=spb, has_init=has_init,
                          emit_final=emit_final),
        grid=(n_seq // spb, nt),
        in_specs=in_specs,
        out_specs=out_specs,
        out_shape=out_shape,
        scratch_shapes=([pltpu.VMEM((spb, HEADS, D_HEAD, D_HEAD), F32)] * 2
                        + [pltpu.VMEM((spb, nck, HEADS, D_HEAD, D_HEAD), F32)] * 2
                        + [pltpu.VMEM((spb, nck, HEADS, CHUNK, CHUNK), BF16)] * 2
                        + [pltpu.VMEM((spb, HEADS, D_HEAD, D_HEAD), BF16)] * 2),
        compiler_params=_params(2),
        name="scan",
    )(*args)


def _window(n, w):
    idx = np.arange(n)
    return np.clip(idx - w // 2, 0, n), np.clip(idx + w // 2, 0, n)


def _band(n, w):
    lo, hi = _window(n, w)
    j = np.arange(n)[None, :]
    return ((j >= lo[:, None]) & (j < hi[:, None])).astype(np.float32), (hi - lo)


def _pool_seq_kernel(u_ref, a_ref, inv_ref, pw_ref, ps_ref, o_ref, diff_scr, *,
                     n_seq_blk, seq_len):
    for gi in range(len(POOL_WINDOWS)):
        cols = slice(gi * POOL_GW, (gi + 1) * POOL_GW)
        for s in range(n_seq_blk):
            rows = slice(s * seq_len, (s + 1) * seq_len)
            ug = u_ref[rows, cols]
            tot = jnp.dot(a_ref[gi], ug, preferred_element_type=F32)
            diff_scr[rows, cols] = (tot * inv_ref[gi] - ug.astype(F32)).astype(BF16)
    for gi in range(len(POOL_WINDOWS)):
        cols = slice(gi * POOL_GW, (gi + 1) * POOL_GW)
        y = jnp.dot(diff_scr[:, cols], pw_ref[gi], preferred_element_type=F32)
        o_ref[:, cols] = (y * ps_ref[:, cols]).astype(BF16)


def _pool_grid_kernel(u_ref, a_ref, invc_ref, pw_ref, ps_ref, o_ref, col_scr, diff_scr,
                      *, n_rows):
    blk = a_ref.shape[1]
    rpb = blk // GRID_W
    for gi, w in enumerate(POOL_WINDOWS):
        cols = slice(gi * POOL_GW, (gi + 1) * POOL_GW)
        for rb in range(n_rows // rpb):
            rows = slice(rb * blk, (rb + 1) * blk)
            tot = jnp.dot(a_ref[gi], u_ref[rows, cols], preferred_element_type=F32)
            col_scr[gi, rows, :] = tot * invc_ref[gi]
    for gi, w in enumerate(POOL_WINDOWS):
        cols = slice(gi * POOL_GW, (gi + 1) * POOL_GW)
        lo, hi = _window(n_rows, w)

        def grid_row(rr):
            return col_scr[gi, rr * GRID_W:(rr + 1) * GRID_W, :]

        acc = None
        for r in range(n_rows):
            if r == 0:
                acc = grid_row(int(lo[0]))
                for rr in range(int(lo[0]) + 1, int(hi[0])):
                    acc = acc + grid_row(rr)
            else:
                for rr in range(int(hi[r - 1]), int(hi[r])):
                    acc = acc + grid_row(rr)
                for rr in range(int(lo[r - 1]), int(lo[r])):
                    acc = acc - grid_row(rr)
            rows = slice(r * GRID_W, (r + 1) * GRID_W)
            mean = acc * (1.0 / float(hi[r] - lo[r]))
            diff_scr[rows, cols] = (mean - u_ref[rows, cols].astype(F32)).astype(BF16)
    for gi in range(len(POOL_WINDOWS)):
        cols = slice(gi * POOL_GW, (gi + 1) * POOL_GW)
        y = jnp.dot(diff_scr[:, cols], pw_ref[gi], preferred_element_type=F32)
        o_ref[:, cols] = (y * ps_ref[:, cols]).astype(BF16)


def _pool_call(p, pool_w, pool_scale, n_seq, seq_len, two_d):
    n_tok = p.shape[0]
    ng = len(POOL_WINDOWS)
    if two_d:
        n_rows = seq_len // GRID_W
        rpb = 4
        blk = rpb * GRID_W
        bands = [_band(GRID_W, w) for w in POOL_WINDOWS]
        a = np.stack([np.kron(np.eye(rpb, dtype=np.float32), b[0]) for b in bands])
        inv = np.stack([np.broadcast_to(np.tile(1.0 / b[1], rpb)[:, None], (blk, POOL_GW))
                        for b in bands]).astype(np.float32)
        kern = functools.partial(_pool_grid_kernel, n_rows=n_rows)
        tb = seq_len
        scratch = [pltpu.VMEM((ng, seq_len, POOL_GW), F32),
                   pltpu.VMEM((seq_len, D_POOL), BF16)]
    else:
        blk = seq_len
        bands = [_band(seq_len, w) for w in POOL_WINDOWS]
        a = np.stack([b[0] for b in bands])
        inv = np.stack([np.broadcast_to((1.0 / b[1])[:, None], (blk, POOL_GW))
                        for b in bands]).astype(np.float32)
        n_seq_blk = 8
        kern = functools.partial(_pool_seq_kernel, n_seq_blk=n_seq_blk, seq_len=seq_len)
        tb = n_seq_blk * seq_len
        scratch = [pltpu.VMEM((tb, D_POOL), BF16)]
    return pl.pallas_call(
        kern,
        grid=(n_tok // tb,),
        in_specs=[
            pl.BlockSpec((tb, D_POOL), lambda i: (i, 6)),
            _const_spec((ng, blk, blk)),
            _const_spec((ng, blk, POOL_GW)),
            _const_spec((ng, POOL_GW, POOL_GW)),
            _const_spec((1, D_POOL)),
        ],
        out_specs=pl.BlockSpec((tb, D_POOL), lambda i: (i, 0)),
        out_shape=jax.ShapeDtypeStruct((n_tok, D_POOL), BF16),
        scratch_shapes=scratch,
        compiler_params=_params(1),
        name="pool",
    )(p, jnp.asarray(a, BF16), jnp.asarray(inv, F32), pool_w, pool_scale)


def _layer_norm(u, g, b):
    mu = jnp.mean(u, axis=-1, keepdims=True)
    d = u - mu
    var = jnp.mean(d * d, axis=-1, keepdims=True)
    return d * lax.rsqrt(var + LN_EPS) * g + b


def _tail_kernel(x_ref, of_ref, ob_ref, gs_ref, yp_ref, g1_ref, sh2_ref, sc2_ref,
                 g2_ref, gn_ref, wo_ref, l1g_ref, l1b_ref, wg_ref, wu_ref, wd_ref,
                 l2g_ref, l2b_ref, o_ref, yr_scr, u_scr, h2_scr):
    tm = x_ref.shape[0]
    gn = gn_ref[...]

    th = tm // 2
    half = [slice(0, th), slice(th, tm)]
    n_rb = th // ROWS
    fs = D_FF // N_FF_SPLIT
    assert n_rb == 2 * N_FF_SPLIT

    def gated_norm_block(i, m):
        rows = slice(i * th + m * ROWS, i * th + (m + 1) * ROWS)
        for h in range(HEADS):
            cols = slice(h * D_HEAD, (h + 1) * D_HEAD)
            o = of_ref[rows, cols].astype(F32) + ob_ref[rows, cols].astype(F32)
            ms = jnp.mean(o * o, axis=-1, keepdims=True)
            on = o * lax.rsqrt(ms + RMS_EPS) * gn
            yr_scr[rows, cols] = (on * gs_ref[rows, cols].astype(F32)).astype(BF16)

    def out_proj(hr, fill):
        y = jnp.dot(yr_scr[hr, :], wo_ref[0:D_REC, :], preferred_element_type=F32)
        fill(0)
        y = y + jnp.dot(yp_ref[hr, :], wo_ref[D_REC:2 * D_REC, :],
                        preferred_element_type=F32)
        fill(1)
        return ALPHA * x_ref[hr, :] + g1_ref[0] * y

    def norm1_block(i, m):
        rows = slice(i * th + m * ROWS, i * th + (m + 1) * ROWS)
        x1 = _layer_norm(u_scr[rows, :], l1g_ref[...], l1b_ref[...])
        u_scr[rows, :] = x1
        h2_scr[rows, :] = (x1 * (1.0 + sc2_ref[0]) + sh2_ref[0]).astype(BF16)

    def norm2_block(i, m):
        rows = slice(i * th + m * ROWS, i * th + (m + 1) * ROWS)
        o_ref[rows, :] = _layer_norm(u_scr[rows, :], l2g_ref[...], l2b_ref[...])

    def ffn_half(i, fill):
        hr = half[i]
        h2 = h2_scr[hr, :]
        ffn = None
        j = 0
        for k in range(N_FF_SPLIT):
            a = jnp.dot(h2, wg_ref[:, k * fs:(k + 1) * fs], preferred_element_type=F32)
            fill(j); j += 1
            b = jnp.dot(h2, wu_ref[:, k * fs:(k + 1) * fs], preferred_element_type=F32)
            fill(j); j += 1
            act = (a * jax.nn.sigmoid(a) * b).astype(BF16)
            part = jnp.dot(act, wd_ref[k * fs:(k + 1) * fs, :],
                           preferred_element_type=F32)
            ffn = part if ffn is None else ffn + part
        u_scr[hr, :] = ALPHA * u_scr[hr, :] + g2_ref[0] * ffn

    def in_two(block, i):
        def emit(j):
            for m in range(j * n_rb // 2, (j + 1) * n_rb // 2):
                block(i, m)
        return emit

    for m in range(n_rb):
        gated_norm_block(0, m)
    u_scr[half[0], :] = out_proj(half[0], in_two(gated_norm_block, 1))
    u_scr[half[1], :] = out_proj(half[1], in_two(norm1_block, 0))
    ffn_half(0, lambda j: norm1_block(1, j))
    ffn_half(1, lambda j: norm2_block(0, j))
    for m in range(n_rb):
        norm2_block(1, m)


def _tail_call(x2, o_f, o_b, p, yp, mod3, mod_row, gn, wo, l1g, l1b, wg, wu, wd,
               l2g, l2b, seq_len):
    n_tok = x2.shape[0]
    tm = TM_TAIL
    tps = max(1, seq_len // tm)

    def mod_spec(col):
        return pl.BlockSpec((1, 1, D_MODEL), lambda i: (mod_row(i, tps), 0, col))

    rec_spec = pl.BlockSpec((tm, D_REC), lambda i: (i, 0))
    return pl.pallas_call(
        _tail_kernel,
        grid=(n_tok // tm,),
        in_specs=[
            pl.BlockSpec((tm, D_MODEL), lambda i: (i, 0)),
            rec_spec, rec_spec,
            pl.BlockSpec((tm, D_REC), lambda i: (i, 5)),
            pl.BlockSpec((tm, D_POOL), lambda i: (i, 0)),
            mod_spec(2), mod_spec(3), mod_spec(4), mod_spec(5),
            _const_spec((1, D_HEAD)),
            _const_spec((D_MODEL, D_MODEL)),
            _const_spec((1, D_MODEL)), _const_spec((1, D_MODEL)),
            _const_spec((D_MODEL, D_FF)), _const_spec((D_MODEL, D_FF)),
            _const_spec((D_FF, D_MODEL)),
            _const_spec((1, D_MODEL)), _const_spec((1, D_MODEL)),
        ],
        out_specs=pl.BlockSpec((tm, D_MODEL), lambda i: (i, 0)),
        out_shape=jax.ShapeDtypeStruct((n_tok, D_MODEL), F32),
        scratch_shapes=[
            pltpu.VMEM((tm, D_REC), BF16),
            pltpu.VMEM((tm, D_MODEL), F32),
            pltpu.VMEM((tm, D_MODEL), BF16),
        ],
        compiler_params=_params(1),
        name="tail",
    )(x2, o_f, o_b, p, yp, mod3, mod3, mod3, mod3, gn, wo, l1g, l1b, wg, wu, wd,
      l2g, l2b)


def kernel(x_prompt, x_sample, state_hgrn_fwd, state_hgrn_bwd, c, c_ctx, w_mod, b_mod,
           w_in, lb_fwd_raw, lb_bwd_raw, hgrn_norm_g, pool_w, pool_scale, w_out, ln1_g,
           ln1_b, w_ffn_gate, w_ffn_up, w_ffn_down, ln2_g, ln2_b):
    n_ctx, t_ctx, _ = x_prompt.shape
    n_lat, t_lat, _ = x_sample.shape
    ctx_row = n_lat

    c_all = jnp.concatenate(
        [c, c_ctx[None, :], jnp.zeros((16 - n_lat - 1, D_MODEL), F32)], axis=0)
    mod = _mod_call(c_all, w_mod[0], b_mod)
    mod3 = mod.reshape(16, 1, 6 * D_MODEL)

    w_in_b = w_in[0].astype(BF16)
    wo_b = w_out[0].astype(BF16)
    wg_b = w_ffn_gate[0].astype(BF16)
    wu_b = w_ffn_up[0].astype(BF16)
    wd_b = w_ffn_down[0].astype(BF16)
    pw_b = pool_w[0].astype(BF16)

    def run(x, n_seq, seq_len, mod_row, two_d, s0f, s0b, emit_final):
        x2 = x.reshape(n_seq * seq_len, D_MODEL)
        p, e = _proj_call(x2, mod3, mod_row, w_in_b, lb_fwd_raw, lb_bwd_raw, n_seq, seq_len)
        scan = _scan_call(p, e, n_seq, seq_len, s0f, s0b, emit_final)
        yp = _pool_call(p, pw_b, pool_scale, n_seq, seq_len, two_d)
        y = _tail_call(x2, scan[0], scan[1], p, yp, mod3, mod_row, hgrn_norm_g, wo_b,
                       ln1_g, ln1_b, wg_b, wu_b, wd_b, ln2_g, ln2_b, seq_len)
        return y.reshape(n_seq, seq_len, D_MODEL), scan[2:]

    y_ctx, fin = run(x_prompt, n_ctx, t_ctx, lambda i, tps: ctx_row, False,
                     None, None, True)
    y_lat, _ = run(x_sample, n_lat, t_lat, lambda i, tps: i // tps, True,
                   state_hgrn_fwd[:, 0], state_hgrn_bwd[:, 0], False)
    return (y_ctx, y_lat, fin[0][:, None], fin[1][:, None])
```

```python
import functools

import numpy as np
import jax
import jax.numpy as jnp
from jax import lax
from jax.experimental import pallas as pl
from jax.experimental.pallas import tpu as pltpu

F32 = jnp.float32
BF16 = jnp.bfloat16

D_MODEL = 1024
D_REC = 512
D_POOL = 512
HEADS = 4
D_HEAD = 128
D_IN = 5 * D_REC + D_POOL
D_FF = 2816
GRID_W = 64
POOL_WINDOWS = (2, 4, 8, 16)
POOL_GW = 128
ALPHA = 2.0 ** 0.25
LN_EPS = 1e-5
RMS_EPS = 1e-6

CHUNK = 64
MID_F = CHUNK // 2 - 1
MID_B = CHUNK // 2
TM_PROJ = 1024
TB_SCAN = 1024
SEQ_PER_SCAN = 4
TM_TAIL = 512
ROWS = 64
FF_SPLITS = (0, 1536, D_FF)
P_COLS = 7 * D_REC
E_COLS = 4 * D_REC
VMEM_LIMIT = 56 * 1024 * 1024

_ARB = "arbitrary"


def _params(n_axes):
    return pltpu.CompilerParams(
        dimension_semantics=(_ARB,) * n_axes, vmem_limit_bytes=VMEM_LIMIT)


def _const_spec(shape):
    nd = len(shape)
    return pl.BlockSpec(shape, lambda *_: (0,) * nd, pipeline_mode=pl.Buffered(1))


def _mod_kernel(c_ref, w_ref, b_ref, o_ref):
    c = c_ref[...]
    hi, lo = _split_bf16(c * jax.nn.sigmoid(c))
    w = w_ref[...].astype(BF16)
    o_ref[...] = (jnp.dot(hi, w, preferred_element_type=F32)
                  + jnp.dot(lo, w, preferred_element_type=F32) + b_ref[...])


def _mod_call(c_all, w_mod, b_mod):
    rows = c_all.shape[0]
    n_out = w_mod.shape[1]
    blk = 1024
    return pl.pallas_call(
        _mod_kernel,
        grid=(n_out // blk,),
        in_specs=[
            pl.BlockSpec((rows, D_MODEL), lambda j: (0, 0)),
            pl.BlockSpec((D_MODEL, blk), lambda j: (0, j)),
            pl.BlockSpec((1, blk), lambda j: (0, j)),
        ],
        out_specs=pl.BlockSpec((rows, blk), lambda j: (0, j)),
        out_shape=jax.ShapeDtypeStruct((rows, n_out), F32),
        compiler_params=_params(1),
        name="mod",
    )(c_all, w_mod, b_mod)


def _lower_bound(raw_ref):
    r0 = raw_ref[0:1, :]
    r1 = raw_ref[1:2, :]
    mx = jnp.maximum(r0, r1)
    e0 = jnp.exp(r0 - mx)
    e1 = jnp.exp(r1 - mx)
    return e0 / (e0 + e1)


def _split_bf16(a):
    hi = a.astype(BF16)
    lo = (a - hi.astype(F32)).astype(BF16)
    return hi, lo


def _proj_kernel(x_ref, sh_ref, sc_ref, w_ref, lbf_ref, lbb_ref, p_ref, e_ref,
                 proj_scr, *, cps):
    tm = x_ref.shape[0]
    h = (x_ref[...] * (1.0 + sc_ref[0]) + sh_ref[0]).astype(BF16)
    proj_scr[...] = jnp.dot(h, w_ref[...], preferred_element_type=F32)

    lb_f = _lower_bound(lbf_ref)
    lb_b = _lower_bound(lbb_ref)
    ri = lax.broadcasted_iota(jnp.int32, (CHUNK, CHUNK), 0)
    ci = lax.broadcasted_iota(jnp.int32, (CHUNK, CHUNK), 1)
    tri_pre = jnp.where(ci <= ri, 1.0, 0.0).astype(BF16)
    tri_suf = jnp.where(ci >= ri, 1.0, 0.0).astype(BF16)

    def chunk_sum(tri, a):
        hi, lo = _split_bf16(a)
        return (jnp.dot(tri, hi, preferred_element_type=F32)
                + jnp.dot(tri, lo, preferred_element_type=F32))

    for c in range(tm // CHUNK):
        rows = slice(c * CHUNK, (c + 1) * CHUNK)
        q = proj_scr[rows, 0:D_REC]
        qs = q * jax.nn.sigmoid(q)

        def direction(z, lb, tri, mid, end):
            f = lb + (1.0 - lb) * jax.nn.sigmoid(z)
            k = 1.0 - f
            b = chunk_sum(tri, jnp.log(f))
            m = b[mid:mid + 1, :]
            d = b - m
            qd = qs * jnp.exp(d)
            ki = k * jnp.exp(-d)
            em = jnp.exp(m)
            r = jnp.exp(b[end:end + 1, :] - m)
            return qd, ki, em, r

        qd_f, ki_f, em_f, r_f = direction(
            proj_scr[rows, D_REC:2 * D_REC], lb_f, tri_pre, MID_F, CHUNK - 1)
        qd_b, ki_b, em_b, r_b = direction(
            proj_scr[rows, 2 * D_REC:3 * D_REC], lb_b, tri_suf, MID_B, 0)
        g = proj_scr[rows, 4 * D_REC:5 * D_REC]

        p_ref[rows, 0 * D_REC:1 * D_REC] = qd_f.astype(BF16)
        p_ref[rows, 1 * D_REC:2 * D_REC] = ki_f.astype(BF16)
        p_ref[rows, 2 * D_REC:3 * D_REC] = qd_b.astype(BF16)
        p_ref[rows, 3 * D_REC:4 * D_REC] = ki_b.astype(BF16)
        p_ref[rows, 4 * D_REC:5 * D_REC] = proj_scr[rows, 3 * D_REC:4 * D_REC].astype(BF16)
        p_ref[rows, 5 * D_REC:6 * D_REC] = (g * jax.nn.sigmoid(g)).astype(BF16)
        p_ref[rows, 6 * D_REC:7 * D_REC] = proj_scr[rows, 5 * D_REC:6 * D_REC].astype(BF16)

        s, cc = c // cps, c % cps
        e_ref[s, cc:cc + 1, 0 * D_REC:1 * D_REC] = em_f
        e_ref[s, cc:cc + 1, 1 * D_REC:2 * D_REC] = r_f
        e_ref[s, cc:cc + 1, 2 * D_REC:3 * D_REC] = em_b
        e_ref[s, cc:cc + 1, 3 * D_REC:4 * D_REC] = r_b


def _proj_call(x2, mod3, mod_row, w_in, lbf_raw, lbb_raw, n_seq, seq_len):
    n_tok = x2.shape[0]
    tm = TM_PROJ
    cpt = tm // CHUNK
    cps = min(cpt, seq_len // CHUNK)
    spt = cpt // cps
    tps = max(1, seq_len // tm)
    n_ck = seq_len // CHUNK

    if spt > 1:
        e_spec = pl.BlockSpec((spt, cps, E_COLS), lambda i: (i, 0, 0))
    else:
        e_spec = pl.BlockSpec((1, cps, E_COLS), lambda i: (i // tps, i % tps, 0))

    return pl.pallas_call(
        functools.partial(_proj_kernel, cps=cps),
        grid=(n_tok // tm,),
        in_specs=[
            pl.BlockSpec((tm, D_MODEL), lambda i: (i, 0)),
            pl.BlockSpec((1, 1, D_MODEL), lambda i: (mod_row(i, tps), 0, 0)),
            pl.BlockSpec((1, 1, D_MODEL), lambda i: (mod_row(i, tps), 0, 1)),
            _const_spec((D_MODEL, D_IN)),
            _const_spec((2, D_REC)),
            _const_spec((2, D_REC)),
        ],
        out_specs=[
            pl.BlockSpec((tm, P_COLS), lambda i: (i, 0)),
            e_spec,
        ],
        out_shape=[
            jax.ShapeDtypeStruct((n_tok, P_COLS), BF16),
            jax.ShapeDtypeStruct((n_seq, n_ck, E_COLS), F32),
        ],
        scratch_shapes=[pltpu.VMEM((tm, D_IN), F32)],
        compiler_params=_params(1),
        name="proj",
    )(x2, mod3, mod3, w_in, lbf_raw, lbb_raw)


def _scan_kernel(*refs, nck, spb, has_init, emit_final):
    it = iter(refs)
    qdf_ref, kif_ref, vf_ref, ef_ref = (next(it) for _ in range(4))
    qdb_ref, kib_ref, vb_ref, eb_ref = (next(it) for _ in range(4))
    if has_init:
        s0f_ref, s0b_ref = next(it), next(it)
    of_ref, ob_ref = next(it), next(it)
    if emit_final:
        finf_ref, finb_ref = next(it), next(it)
    sf_scr, sb_scr, uf_scr, ub_scr, scf_scr, scb_scr, tf_scr, tb_scr = (
        next(it) for _ in range(8))

    j = pl.program_id(1)
    seq_heads = [(q, h) for q in range(spb) for h in range(HEADS)]
    ri = lax.broadcasted_iota(jnp.int32, (CHUNK, CHUNK), 0)
    ci = lax.broadcasted_iota(jnp.int32, (CHUNK, CHUNK), 1)
    nt_dims = (((1,), (1,)), ((), ()))
    tn_dims = (((0,), (0,)), ((), ()))

    @pl.when(j == 0)
    def _():
        for q, h in seq_heads:
            if has_init:
                sf_scr[q, h] = s0f_ref[q, h].T
                sb_scr[q, h] = s0b_ref[q, h].T
            else:
                sf_scr[q, h] = jnp.zeros((D_HEAD, D_HEAD), F32)
                sb_scr[q, h] = jnp.zeros((D_HEAD, D_HEAD), F32)

    def tile(q, c, h):
        return (slice((q * nck + c) * CHUNK, (q * nck + c + 1) * CHUNK),
                slice(h * D_HEAD, (h + 1) * D_HEAD))

    def prepare(qd_ref, ki_ref, v_ref, keep, u_scr, sc_scr):
        for q in range(spb):
            for c in range(nck):
                for h in range(HEADS):
                    rows, cols = tile(q, c, h)
                    u_scr[q, c, h] = lax.dot_general(
                        v_ref[rows, cols], ki_ref[rows, cols], tn_dims,
                        preferred_element_type=F32)
                    sc = lax.dot_general(qd_ref[rows, cols], ki_ref[rows, cols], nt_dims,
                                         preferred_element_type=F32)
                    sc_scr[q, c, h] = jnp.where(keep, sc, 0.0).astype(BF16)

    def step(qd_ref, v_ref, e_ref, o_ref, s_scr, u_scr, sc_scr, smt_scr, c):
        for q, h in seq_heads:
            rows, cols = tile(q, c, h)
            em = e_ref[q, c:c + 1, h * D_HEAD:(h + 1) * D_HEAD]
            r = e_ref[q, c:c + 1, D_REC + h * D_HEAD:D_REC + (h + 1) * D_HEAD]
            sm = s_scr[q, h] * em
            smt_scr[q, h] = sm.astype(BF16).T
            o = jnp.dot(qd_ref[rows, cols], smt_scr[q, h], preferred_element_type=F32)
            o = o + jnp.dot(sc_scr[q, c, h], v_ref[rows, cols],
                            preferred_element_type=F32)
            o_ref[rows, cols] = o.astype(BF16)
            s_scr[q, h] = (sm + u_scr[q, c, h]) * r

    prepare(qdf_ref, kif_ref, vf_ref, ci <= ri, uf_scr, scf_scr)
    prepare(qdb_ref, kib_ref, vb_ref, ci >= ri, ub_scr, scb_scr)
    for c in range(nck):
        step(qdf_ref, vf_ref, ef_ref, of_ref, sf_scr, uf_scr, scf_scr, tf_scr, c)
        step(qdb_ref, vb_ref, eb_ref, ob_ref, sb_scr, ub_scr, scb_scr, tb_scr, nck - 1 - c)

    if emit_final:
        @pl.when(j == pl.num_programs(1) - 1)
        def _():
            for q, h in seq_heads:
                finf_ref[q, h] = sf_scr[q, h].T
                finb_ref[q, h] = sb_scr[q, h].T


def _scan_call(p, e, n_seq, seq_len, s0f=None, s0b=None, emit_final=False):
    tb = min(seq_len, TB_SCAN)
    nt = seq_len // tb
    spb = SEQ_PER_SCAN if nt == 1 else 1
    nck = tb // CHUNK
    has_init = s0f is not None
    st_blk = (spb, HEADS, D_HEAD, D_HEAD)
    rows = spb * tb
    n_tok = p.shape[0]

    def fwd(b, j):
        return b * nt + j

    def bwd(b, j):
        return b * nt + nt - 1 - j

    def p_spec(blk, col):
        return pl.BlockSpec((rows, D_REC), lambda b, j: (blk(b, j), col))

    in_specs = [
        p_spec(fwd, 0), p_spec(fwd, 1), p_spec(fwd, 4),
        pl.BlockSpec((spb, nck, 2 * D_REC), lambda b, j: (b, j, 0)),
        p_spec(bwd, 2), p_spec(bwd, 3), p_spec(bwd, 4),
        pl.BlockSpec((spb, nck, 2 * D_REC), lambda b, j: (b, nt - 1 - j, 1)),
    ]
    args = [p, p, p, e, p, p, p, e]
    if has_init:
        in_specs += [pl.BlockSpec(st_blk, lambda b, j: (b, 0, 0, 0))] * 2
        args += [s0f, s0b]
    o_shape = jax.ShapeDtypeStruct((n_tok, D_REC), BF16)
    out_specs = [
        pl.BlockSpec((rows, D_REC), lambda b, j: (fwd(b, j), 0)),
        pl.BlockSpec((rows, D_REC), lambda b, j: (bwd(b, j), 0)),
    ]
    out_shape = [o_shape, o_shape]
    if emit_final:
        out_specs += [pl.BlockSpec(st_blk, lambda b, j: (b, 0, 0, 0))] * 2
        out_shape += [jax.ShapeDtypeStruct((n_seq, HEADS, D_HEAD, D_HEAD), F32)] * 2

    return pl.pallas_call(
        functools.partial(_scan_kernel, nck=nck, spb=spb, has_init=has_init,
                          emit_final=emit_final),
        grid=(n_seq // spb, nt),
        in_specs=in_specs,
        out_specs=out_specs,
        out_shape=out_shape,
        scratch_shapes=([pltpu.VMEM((spb, HEADS, D_HEAD, D_HEAD), F32)] * 2
                        + [pltpu.VMEM((spb, nck, HEADS, D_HEAD, D_HEAD), F32)] * 2
                        + [pltpu.VMEM((spb, nck, HEADS, CHUNK, CHUNK), BF16)] * 2
                        + [pltpu.VMEM((spb, HEADS, D_HEAD, D_HEAD), BF16)] * 2),
        compiler_params=_params(2),
        name="scan",
    )(*args)


def _window(n, w):
    idx = np.arange(n)
    return np.clip(idx - w // 2, 0, n), np.clip(idx + w // 2, 0, n)


def _band(n, w):
    lo, hi = _window(n, w)
    j = np.arange(n)[None, :]
    return ((j >= lo[:, None]) & (j < hi[:, None])).astype(np.float32), (hi - lo)


def _pool_seq_kernel(u_ref, a_ref, inv_ref, pw_ref, ps_ref, o_ref, diff_scr, *,
                     n_seq_blk, seq_len):
    for gi in range(len(POOL_WINDOWS)):
        cols = slice(gi * POOL_GW, (gi + 1) * POOL_GW)
        for s in range(n_seq_blk):
            rows = slice(s * seq_len, (s + 1) * seq_len)
            ug = u_ref[rows, cols]
            tot = jnp.dot(a_ref[gi], ug, preferred_element_type=F32)
            diff_scr[rows, cols] = (tot * inv_ref[gi] - ug.astype(F32)).astype(BF16)
    for gi in range(len(POOL_WINDOWS)):
        cols = slice(gi * POOL_GW, (gi + 1) * POOL_GW)
        y = jnp.dot(diff_scr[:, cols], pw_ref[gi], preferred_element_type=F32)
        o_ref[:, cols] = (y * ps_ref[:, cols]).astype(BF16)


def _pool_grid_kernel(u_ref, a_ref, invc_ref, pw_ref, ps_ref, o_ref, col_scr, diff_scr,
                      *, n_rows):
    blk = a_ref.shape[1]
    rpb = blk // GRID_W
    for gi, w in enumerate(POOL_WINDOWS):
        cols = slice(gi * POOL_GW, (gi + 1) * POOL_GW)
        for rb in range(n_rows // rpb):
            rows = slice(rb * blk, (rb + 1) * blk)
            tot = jnp.dot(a_ref[gi], u_ref[rows, cols], preferred_element_type=F32)
            col_scr[gi, rows, :] = tot * invc_ref[gi]
    for gi, w in enumerate(POOL_WINDOWS):
        cols = slice(gi * POOL_GW, (gi + 1) * POOL_GW)
        lo, hi = _window(n_rows, w)

        def grid_row(rr):
            return col_scr[gi, rr * GRID_W:(rr + 1) * GRID_W, :]

        acc = None
        for r in range(n_rows):
            if r == 0:
                acc = grid_row(int(lo[0]))
                for rr in range(int(lo[0]) + 1, int(hi[0])):
                    acc = acc + grid_row(rr)
            else:
                for rr in range(int(hi[r - 1]), int(hi[r])):
                    acc = acc + grid_row(rr)
                for rr in range(int(lo[r - 1]), int(lo[r])):
                    acc = acc - grid_row(rr)
            rows = slice(r * GRID_W, (r + 1) * GRID_W)
            mean = acc * (1.0 / float(hi[r] - lo[r]))
            diff_scr[rows, cols] = (mean - u_ref[rows, cols].astype(F32)).astype(BF16)
    for gi in range(len(POOL_WINDOWS)):
        cols = slice(gi * POOL_GW, (gi + 1) * POOL_GW)
        y = jnp.dot(diff_scr[:, cols], pw_ref[gi], preferred_element_type=F32)
        o_ref[:, cols] = (y * ps_ref[:, cols]).astype(BF16)


def _pool_call(p, pool_w, pool_scale, n_seq, seq_len, two_d):
    n_tok = p.shape[0]
    ng = len(POOL_WINDOWS)
    if two_d:
        n_rows = seq_len // GRID_W
        rpb = 4
        blk = rpb * GRID_W
        bands = [_band(GRID_W, w) for w in POOL_WINDOWS]
        a = np.stack([np.kron(np.eye(rpb, dtype=np.float32), b[0]) for b in bands])
        inv = np.stack([np.broadcast_to(np.tile(1.0 / b[1], rpb)[:, None], (blk, POOL_GW))
                        for b in bands]).astype(np.float32)
        kern = functools.partial(_pool_grid_kernel, n_rows=n_rows)
        tb = seq_len
        scratch = [pltpu.VMEM((ng, seq_len, POOL_GW), F32),
                   pltpu.VMEM((seq_len, D_POOL), BF16)]
    else:
        blk = seq_len
        bands = [_band(seq_len, w) for w in POOL_WINDOWS]
        a = np.stack([b[0] for b in bands])
        inv = np.stack([np.broadcast_to((1.0 / b[1])[:, None], (blk, POOL_GW))
                        for b in bands]).astype(np.float32)
        n_seq_blk = 8
        kern = functools.partial(_pool_seq_kernel, n_seq_blk=n_seq_blk, seq_len=seq_len)
        tb = n_seq_blk * seq_len
        scratch = [pltpu.VMEM((tb, D_POOL), BF16)]
    return pl.pallas_call(
        kern,
        grid=(n_tok // tb,),
        in_specs=[
            pl.BlockSpec((tb, D_POOL), lambda i: (i, 6)),
            _const_spec((ng, blk, blk)),
            _const_spec((ng, blk, POOL_GW)),
            _const_spec((ng, POOL_GW, POOL_GW)),
            _const_spec((1, D_POOL)),
        ],
        out_specs=pl.BlockSpec((tb, D_POOL), lambda i: (i, 0)),
        out_shape=jax.ShapeDtypeStruct((n_tok, D_POOL), BF16),
        scratch_shapes=scratch,
        compiler_params=_params(1),
        name="pool",
    )(p, jnp.asarray(a, BF16), jnp.asarray(inv, F32), pool_w, pool_scale)


def _layer_norm(u, g, b):
    mu = jnp.mean(u, axis=-1, keepdims=True)
    d = u - mu
    var = jnp.mean(d * d, axis=-1, keepdims=True)
    return d * lax.rsqrt(var + LN_EPS) * g + b


def _tail_kernel(x_ref, of_ref, ob_ref, gs_ref, yp_ref, g1_ref, sh2_ref, sc2_ref,
                 g2_ref, gn_ref, wo_ref, l1g_ref, l1b_ref, wg_ref, wu_ref, wd_ref,
                 l2g_ref, l2b_ref, o_ref, yr_scr, u_scr, h2_scr):
    tm = x_ref.shape[0]
    gn = gn_ref[...]

    th = tm // 2
    half = [slice(0, th), slice(th, tm)]
    n_rb = th // ROWS
    ff_slices = [slice(lo, hi) for lo, hi in zip(FF_SPLITS[:-1], FF_SPLITS[1:])]
    assert n_rb == 2 * len(ff_slices)

    def gated_norm_block(i, m):
        rows = slice(i * th + m * ROWS, i * th + (m + 1) * ROWS)
        for h in range(HEADS):
            cols = slice(h * D_HEAD, (h + 1) * D_HEAD)
            o = of_ref[rows, cols].astype(F32) + ob_ref[rows, cols].astype(F32)
            ms = jnp.mean(o * o, axis=-1, keepdims=True)
            on = o * lax.rsqrt(ms + RMS_EPS) * gn
            yr_scr[rows, cols] = (on * gs_ref[rows, cols].astype(F32)).astype(BF16)

    def out_proj(hr, fill):
        y = jnp.dot(yr_scr[hr, :], wo_ref[0:D_REC, :], preferred_element_type=F32)
        fill(0)
        y = y + jnp.dot(yp_ref[hr, :], wo_ref[D_REC:2 * D_REC, :],
                        preferred_element_type=F32)
        fill(1)
        return ALPHA * x_ref[hr, :] + g1_ref[0] * y

    def norm1_block(i, m):
        rows = slice(i * th + m * ROWS, i * th + (m + 1) * ROWS)
        x1 = _layer_norm(u_scr[rows, :], l1g_ref[...], l1b_ref[...])
        u_scr[rows, :] = x1
        h2_scr[rows, :] = (x1 * (1.0 + sc2_ref[0]) + sh2_ref[0]).astype(BF16)

    def norm2_block(i, m):
        rows = slice(i * th + m * ROWS, i * th + (m + 1) * ROWS)
        o_ref[rows, :] = _layer_norm(u_scr[rows, :], l2g_ref[...], l2b_ref[...])

    def ffn_half(i, fill):
        hr = half[i]
        h2 = h2_scr[hr, :]
        ffn = None
        j = 0
        for cs in ff_slices:
            a = jnp.dot(h2, wg_ref[:, cs], preferred_element_type=F32)
            fill(j); j += 1
            b = jnp.dot(h2, wu_ref[:, cs], preferred_element_type=F32)
            fill(j); j += 1
            act = (a * jax.nn.sigmoid(a) * b).astype(BF16)
            part = jnp.dot(act, wd_ref[cs, :], preferred_element_type=F32)
            ffn = part if ffn is None else ffn + part
        u_scr[hr, :] = ALPHA * u_scr[hr, :] + g2_ref[0] * ffn

    def in_two(block, i):
        def emit(j):
            for m in range(j * n_rb // 2, (j + 1) * n_rb // 2):
                block(i, m)
        return emit

    for m in range(n_rb):
        gated_norm_block(0, m)
    u_scr[half[0], :] = out_proj(half[0], in_two(gated_norm_block, 1))
    u_scr[half[1], :] = out_proj(half[1], in_two(norm1_block, 0))
    ffn_half(0, lambda j: norm1_block(1, j))
    ffn_half(1, lambda j: norm2_block(0, j))
    for m in range(n_rb):
        norm2_block(1, m)


def _tail_call(x2, o_f, o_b, p, yp, mod3, mod_row, gn, wo, l1g, l1b, wg, wu, wd,
               l2g, l2b, seq_len):
    n_tok = x2.shape[0]
    tm = TM_TAIL
    tps = max(1, seq_len // tm)

    def mod_spec(col):
        return pl.BlockSpec((1, 1, D_MODEL), lambda i: (mod_row(i, tps), 0, col))

    rec_spec = pl.BlockSpec((tm, D_REC), lambda i: (i, 0))
    return pl.pallas_call(
        _tail_kernel,
        grid=(n_tok // tm,),
        in_specs=[
            pl.BlockSpec((tm, D_MODEL), lambda i: (i, 0)),
            rec_spec, rec_spec,
            pl.BlockSpec((tm, D_REC), lambda i: (i, 5)),
            pl.BlockSpec((tm, D_POOL), lambda i: (i, 0)),
            mod_spec(2), mod_spec(3), mod_spec(4), mod_spec(5),
            _const_spec((1, D_HEAD)),
            _const_spec((D_MODEL, D_MODEL)),
            _const_spec((1, D_MODEL)), _const_spec((1, D_MODEL)),
            _const_spec((D_MODEL, D_FF)), _const_spec((D_MODEL, D_FF)),
            _const_spec((D_FF, D_MODEL)),
            _const_spec((1, D_MODEL)), _const_spec((1, D_MODEL)),
        ],
        out_specs=pl.BlockSpec((tm, D_MODEL), lambda i: (i, 0)),
        out_shape=jax.ShapeDtypeStruct((n_tok, D_MODEL), F32),
        scratch_shapes=[
            pltpu.VMEM((tm, D_REC), BF16),
            pltpu.VMEM((tm, D_MODEL), F32),
            pltpu.VMEM((tm, D_MODEL), BF16),
        ],
        compiler_params=_params(1),
        name="tail",
    )(x2, o_f, o_b, p, yp, mod3, mod3, mod3, mod3, gn, wo, l1g, l1b, wg, wu, wd,
      l2g, l2b)


def kernel(x_prompt, x_sample, state_hgrn_fwd, state_hgrn_bwd, c, c_ctx, w_mod, b_mod,
           w_in, lb_fwd_raw, lb_bwd_raw, hgrn_norm_g, pool_w, pool_scale, w_out, ln1_g,
           ln1_b, w_ffn_gate, w_ffn_up, w_ffn_down, ln2_g, ln2_b):
    n_ctx, t_ctx, _ = x_prompt.shape
    n_lat, t_lat, _ = x_sample.shape
    ctx_row = n_lat

    c_all = jnp.concatenate(
        [c, c_ctx[None, :], jnp.zeros((16 - n_lat - 1, D_MODEL), F32)], axis=0)
    mod = _mod_call(c_all, w_mod[0], b_mod)
    mod3 = mod.reshape(16, 1, 6 * D_MODEL)

    w_in_b = w_in[0].astype(BF16)
    wo_b = w_out[0].astype(BF16)
    wg_b = w_ffn_gate[0].astype(BF16)
    wu_b = w_ffn_up[0].astype(BF16)
    wd_b = w_ffn_down[0].astype(BF16)
    pw_b = pool_w[0].astype(BF16)

    def run(x, n_seq, seq_len, mod_row, two_d, s0f, s0b, emit_final):
        x2 = x.reshape(n_seq * seq_len, D_MODEL)
        p, e = _proj_call(x2, mod3, mod_row, w_in_b, lb_fwd_raw, lb_bwd_raw, n_seq, seq_len)
        scan = _scan_call(p, e, n_seq, seq_len, s0f, s0b, emit_final)
        yp = _pool_call(p, pw_b, pool_scale, n_seq, seq_len, two_d)
        y = _tail_call(x2, scan[0], scan[1], p, yp, mod3, mod_row, hgrn_norm_g, wo_b,
                       ln1_g, ln1_b, wg_b, wu_b, wd_b, ln2_g, ln2_b, seq_len)
        return y.reshape(n_seq, seq_len, D_MODEL), scan[2:]

    y_ctx, fin = run(x_prompt, n_ctx, t_ctx, lambda i, tps: ctx_row, False,
                     None, None, True)
    y_lat, _ = run(x_sample, n_lat, t_lat, lambda i, tps: i // tps, True,
                   state_hgrn_fwd[:, 0], state_hgrn_bwd[:, 0], False)
    return (y_ctx, y_lat, fin[0][:, None], fin[1][:, None])
```

```python
import functools

import numpy as np
import jax
import jax.numpy as jnp
from jax import lax
from jax.experimental import pallas as pl
from jax.experimental.pallas import tpu as pltpu

F32 = jnp.float32
BF16 = jnp.bfloat16

D_MODEL = 1024
D_REC = 512
D_POOL = 512
HEADS = 4
D_HEAD = 128
D_IN = 5 * D_REC + D_POOL
D_FF = 2816
GRID_W = 64
POOL_WINDOWS = (2, 4, 8, 16)
POOL_GW = 128
ALPHA = 2.0 ** 0.25
LN_EPS = 1e-5
RMS_EPS = 1e-6

CHUNK = 64
MID_F = CHUNK // 2 - 1
MID_B = CHUNK // 2
TM_PROJ = 1024
TB_SCAN = 1024
SEQ_PER_SCAN = 4
TM_TAIL = 512
ROWS = 64
MXU_TILE = 256
FF_SPLITS = (0, 1536, D_FF)
P_COLS = 7 * D_REC
E_COLS = 4 * D_REC
VMEM_LIMIT = 56 * 1024 * 1024

_ARB = "arbitrary"


def _params(n_axes):
    return pltpu.CompilerParams(
        dimension_semantics=(_ARB,) * n_axes, vmem_limit_bytes=VMEM_LIMIT)


def _const_spec(shape):
    nd = len(shape)
    return pl.BlockSpec(shape, lambda *_: (0,) * nd, pipeline_mode=pl.Buffered(1))


def _mod_kernel(c_ref, w_ref, b_ref, o_ref):
    c = c_ref[...]
    hi, lo = _split_bf16(c * jax.nn.sigmoid(c))
    w = w_ref[...].astype(BF16)
    o_ref[...] = (jnp.dot(hi, w, preferred_element_type=F32)
                  + jnp.dot(lo, w, preferred_element_type=F32) + b_ref[...])


def _mod_call(c_all, w_mod, b_mod):
    rows = c_all.shape[0]
    n_out = w_mod.shape[1]
    blk = 1024
    return pl.pallas_call(
        _mod_kernel,
        grid=(n_out // blk,),
        in_specs=[
            pl.BlockSpec((rows, D_MODEL), lambda j: (0, 0)),
            pl.BlockSpec((D_MODEL, blk), lambda j: (0, j)),
            pl.BlockSpec((1, blk), lambda j: (0, j)),
        ],
        out_specs=pl.BlockSpec((rows, blk), lambda j: (0, j)),
        out_shape=jax.ShapeDtypeStruct((rows, n_out), F32),
        compiler_params=_params(1),
        name="mod",
    )(c_all, w_mod, b_mod)


def _lower_bound(raw_ref):
    r0 = raw_ref[0:1, :]
    r1 = raw_ref[1:2, :]
    mx = jnp.maximum(r0, r1)
    e0 = jnp.exp(r0 - mx)
    e1 = jnp.exp(r1 - mx)
    return e0 / (e0 + e1)


def _split_bf16(a):
    hi = a.astype(BF16)
    lo = (a - hi.astype(F32)).astype(BF16)
    return hi, lo


def _proj_kernel(x_ref, sh_ref, sc_ref, w_ref, lbf_ref, lbb_ref, p_ref, e_ref,
                 proj_scr, *, cps):
    tm = x_ref.shape[0]
    h = (x_ref[...] * (1.0 + sc_ref[0]) + sh_ref[0]).astype(BF16)
    proj_scr[...] = jnp.dot(h, w_ref[...], preferred_element_type=F32)

    lb_f = _lower_bound(lbf_ref)
    lb_b = _lower_bound(lbb_ref)
    ri = lax.broadcasted_iota(jnp.int32, (CHUNK, CHUNK), 0)
    ci = lax.broadcasted_iota(jnp.int32, (CHUNK, CHUNK), 1)
    tri_pre = jnp.where(ci <= ri, 1.0, 0.0).astype(BF16)
    tri_suf = jnp.where(ci >= ri, 1.0, 0.0).astype(BF16)

    def chunk_sum(tri, a):
        hi, lo = _split_bf16(a)
        return (jnp.dot(tri, hi, preferred_element_type=F32)
                + jnp.dot(tri, lo, preferred_element_type=F32))

    for c in range(tm // CHUNK):
        rows = slice(c * CHUNK, (c + 1) * CHUNK)
        q = proj_scr[rows, 0:D_REC]
        qs = q * jax.nn.sigmoid(q)

        def direction(z, lb, tri, mid, end):
            f = lb + (1.0 - lb) * jax.nn.sigmoid(z)
            k = 1.0 - f
            b = chunk_sum(tri, jnp.log(f))
            m = b[mid:mid + 1, :]
            d = b - m
            qd = qs * jnp.exp(d)
            ki = k * jnp.exp(-d)
            em = jnp.exp(m)
            r = jnp.exp(b[end:end + 1, :] - m)
            return qd, ki, em, r

        qd_f, ki_f, em_f, r_f = direction(
            proj_scr[rows, D_REC:2 * D_REC], lb_f, tri_pre, MID_F, CHUNK - 1)
        qd_b, ki_b, em_b, r_b = direction(
            proj_scr[rows, 2 * D_REC:3 * D_REC], lb_b, tri_suf, MID_B, 0)
        g = proj_scr[rows, 4 * D_REC:5 * D_REC]

        p_ref[rows, 0 * D_REC:1 * D_REC] = qd_f.astype(BF16)
        p_ref[rows, 1 * D_REC:2 * D_REC] = ki_f.astype(BF16)
        p_ref[rows, 2 * D_REC:3 * D_REC] = qd_b.astype(BF16)
        p_ref[rows, 3 * D_REC:4 * D_REC] = ki_b.astype(BF16)
        p_ref[rows, 4 * D_REC:5 * D_REC] = proj_scr[rows, 3 * D_REC:4 * D_REC].astype(BF16)
        p_ref[rows, 5 * D_REC:6 * D_REC] = (g * jax.nn.sigmoid(g)).astype(BF16)
        p_ref[rows, 6 * D_REC:7 * D_REC] = proj_scr[rows, 5 * D_REC:6 * D_REC].astype(BF16)

        s, cc = c // cps, c % cps
        e_ref[s, cc:cc + 1, 0 * D_REC:1 * D_REC] = em_f
        e_ref[s, cc:cc + 1, 1 * D_REC:2 * D_REC] = r_f
        e_ref[s, cc:cc + 1, 2 * D_REC:3 * D_REC] = em_b
        e_ref[s, cc:cc + 1, 3 * D_REC:4 * D_REC] = r_b


def _proj_call(x2, mod3, mod_row, w_in, lbf_raw, lbb_raw, n_seq, seq_len):
    n_tok = x2.shape[0]
    tm = TM_PROJ
    cpt = tm // CHUNK
    cps = min(cpt, seq_len // CHUNK)
    spt = cpt // cps
    tps = max(1, seq_len // tm)
    n_ck = seq_len // CHUNK

    if spt > 1:
        e_spec = pl.BlockSpec((spt, cps, E_COLS), lambda i: (i, 0, 0))
    else:
        e_spec = pl.BlockSpec((1, cps, E_COLS), lambda i: (i // tps, i % tps, 0))

    return pl.pallas_call(
        functools.partial(_proj_kernel, cps=cps),
        grid=(n_tok // tm,),
        in_specs=[
            pl.BlockSpec((tm, D_MODEL), lambda i: (i, 0)),
            pl.BlockSpec((1, 1, D_MODEL), lambda i: (mod_row(i, tps), 0, 0)),
            pl.BlockSpec((1, 1, D_MODEL), lambda i: (mod_row(i, tps), 0, 1)),
            _const_spec((D_MODEL, D_IN)),
            _const_spec((2, D_REC)),
            _const_spec((2, D_REC)),
        ],
        out_specs=[
            pl.BlockSpec((tm, P_COLS), lambda i: (i, 0)),
            e_spec,
        ],
        out_shape=[
            jax.ShapeDtypeStruct((n_tok, P_COLS), BF16),
            jax.ShapeDtypeStruct((n_seq, n_ck, E_COLS), F32),
        ],
        scratch_shapes=[pltpu.VMEM((tm, D_IN), F32)],
        compiler_params=_params(1),
        name="proj",
    )(x2, mod3, mod3, w_in, lbf_raw, lbb_raw)


def _scan_kernel(*refs, nck, spb, has_init, emit_final):
    it = iter(refs)
    qdf_ref, kif_ref, vf_ref, ef_ref = (next(it) for _ in range(4))
    qdb_ref, kib_ref, vb_ref, eb_ref = (next(it) for _ in range(4))
    if has_init:
        s0f_ref, s0b_ref = next(it), next(it)
    of_ref, ob_ref = next(it), next(it)
    if emit_final:
        finf_ref, finb_ref = next(it), next(it)
    sf_scr, sb_scr, uf_scr, ub_scr, scf_scr, scb_scr, tf_scr, tb_scr = (
        next(it) for _ in range(8))

    j = pl.program_id(1)
    seq_heads = [(q, h) for q in range(spb) for h in range(HEADS)]
    ri = lax.broadcasted_iota(jnp.int32, (CHUNK, CHUNK), 0)
    ci = lax.broadcasted_iota(jnp.int32, (CHUNK, CHUNK), 1)
    nt_dims = (((1,), (1,)), ((), ()))
    tn_dims = (((0,), (0,)), ((), ()))

    @pl.when(j == 0)
    def _():
        for q, h in seq_heads:
            if has_init:
                sf_scr[q, h] = s0f_ref[q, h].T
                sb_scr[q, h] = s0b_ref[q, h].T
            else:
                sf_scr[q, h] = jnp.zeros((D_HEAD, D_HEAD), F32)
                sb_scr[q, h] = jnp.zeros((D_HEAD, D_HEAD), F32)

    def tile(q, c, h):
        return (slice((q * nck + c) * CHUNK, (q * nck + c + 1) * CHUNK),
                slice(h * D_HEAD, (h + 1) * D_HEAD))

    def prepare(qd_ref, ki_ref, v_ref, keep, u_scr, sc_scr):
        for q in range(spb):
            for c in range(nck):
                for h in range(HEADS):
                    rows, cols = tile(q, c, h)
                    u_scr[q, c, h] = lax.dot_general(
                        v_ref[rows, cols], ki_ref[rows, cols], tn_dims,
                        preferred_element_type=F32)
                    sc = lax.dot_general(qd_ref[rows, cols], ki_ref[rows, cols], nt_dims,
                                         preferred_element_type=F32)
                    sc_scr[q, c, h] = jnp.where(keep, sc, 0.0).astype(BF16)

    def step(qd_ref, v_ref, e_ref, o_ref, s_scr, u_scr, sc_scr, smt_scr, c):
        for q, h in seq_heads:
            rows, cols = tile(q, c, h)
            em = e_ref[q, c:c + 1, h * D_HEAD:(h + 1) * D_HEAD]
            r = e_ref[q, c:c + 1, D_REC + h * D_HEAD:D_REC + (h + 1) * D_HEAD]
            sm = s_scr[q, h] * em
            smt_scr[q, h] = sm.astype(BF16).T
            o = jnp.dot(qd_ref[rows, cols], smt_scr[q, h], preferred_element_type=F32)
            o = o + jnp.dot(sc_scr[q, c, h], v_ref[rows, cols],
                            preferred_element_type=F32)
            o_ref[rows, cols] = o.astype(BF16)
            s_scr[q, h] = (sm + u_scr[q, c, h]) * r

    prepare(qdf_ref, kif_ref, vf_ref, ci <= ri, uf_scr, scf_scr)
    prepare(qdb_ref, kib_ref, vb_ref, ci >= ri, ub_scr, scb_scr)
    for c in range(nck):
        step(qdf_ref, vf_ref, ef_ref, of_ref, sf_scr, uf_scr, scf_scr, tf_scr, c)
        step(qdb_ref, vb_ref, eb_ref, ob_ref, sb_scr, ub_scr, scb_scr, tb_scr, nck - 1 - c)

    if emit_final:
        @pl.when(j == pl.num_programs(1) - 1)
        def _():
            for q, h in seq_heads:
                finf_ref[q, h] = sf_scr[q, h].T
                finb_ref[q, h] = sb_scr[q, h].T


def _scan_call(p, e, n_seq, seq_len, s0f=None, s0b=None, emit_final=False):
    tb = min(seq_len, TB_SCAN)
    nt = seq_len // tb
    spb = SEQ_PER_SCAN if nt == 1 else 1
    nck = tb // CHUNK
    has_init = s0f is not None
    st_blk = (spb, HEADS, D_HEAD, D_HEAD)
    rows = spb * tb
    n_tok = p.shape[0]

    def fwd(b, j):
        return b * nt + j

    def bwd(b, j):
        return b * nt + nt - 1 - j

    def p_spec(blk, col):
        return pl.BlockSpec((rows, D_REC), lambda b, j: (blk(b, j), col))

    in_specs = [
        p_spec(fwd, 0), p_spec(fwd, 1), p_spec(fwd, 4),
        pl.BlockSpec((spb, nck, 2 * D_REC), lambda b, j: (b, j, 0)),
        p_spec(bwd, 2), p_spec(bwd, 3), p_spec(bwd, 4),
        pl.BlockSpec((spb, nck, 2 * D_REC), lambda b, j: (b, nt - 1 - j, 1)),
    ]
    args = [p, p, p, e, p, p, p, e]
    if has_init:
        in_specs += [pl.BlockSpec(st_blk, lambda b, j: (b, 0, 0, 0))] * 2
        args += [s0f, s0b]
    o_shape = jax.ShapeDtypeStruct((n_tok, D_REC), BF16)
    out_specs = [
        pl.BlockSpec((rows, D_REC), lambda b, j: (fwd(b, j), 0)),
        pl.BlockSpec((rows, D_REC), lambda b, j: (bwd(b, j), 0)),
    ]
    out_shape = [o_shape, o_shape]
    if emit_final:
        out_specs += [pl.BlockSpec(st_blk, lambda b, j: (b, 0, 0, 0))] * 2
        out_shape += [jax.ShapeDtypeStruct((n_seq, HEADS, D_HEAD, D_HEAD), F32)] * 2

    return pl.pallas_call(
        functools.partial(_scan_kernel, nck=nck, spb=spb, has_init=has_init,
                          emit_final=emit_final),
        grid=(n_seq // spb, nt),
        in_specs=in_specs,
        out_specs=out_specs,
        out_shape=out_shape,
        scratch_shapes=([pltpu.VMEM((spb, HEADS, D_HEAD, D_HEAD), F32)] * 2
                        + [pltpu.VMEM((spb, nck, HEADS, D_HEAD, D_HEAD), F32)] * 2
                        + [pltpu.VMEM((spb, nck, HEADS, CHUNK, CHUNK), BF16)] * 2
                        + [pltpu.VMEM((spb, HEADS, D_HEAD, D_HEAD), BF16)] * 2),
        compiler_params=_params(2),
        name="scan",
    )(*args)


def _window(n, w):
    idx = np.arange(n)
    return np.clip(idx - w // 2, 0, n), np.clip(idx + w // 2, 0, n)


def _band(n, w):
    lo, hi = _window(n, w)
    j = np.arange(n)[None, :]
    return ((j >= lo[:, None]) & (j < hi[:, None])).astype(np.float32), (hi - lo)


def _pool_seq_kernel(u_ref, a_ref, inv_ref, pw_ref, ps_ref, o_ref, diff_scr, *,
                     n_seq_blk, seq_len):
    for gi in range(len(POOL_WINDOWS)):
        cols = slice(gi * POOL_GW, (gi + 1) * POOL_GW)
        for s in range(n_seq_blk):
            rows = slice(s * seq_len, (s + 1) * seq_len)
            ug = u_ref[rows, cols]
            tot = jnp.dot(a_ref[gi], ug, preferred_element_type=F32)
            diff_scr[rows, cols] = (tot * inv_ref[gi] - ug.astype(F32)).astype(BF16)
    for gi in range(len(POOL_WINDOWS)):
        cols = slice(gi * POOL_GW, (gi + 1) * POOL_GW)
        y = jnp.dot(diff_scr[:, cols], pw_ref[gi], preferred_element_type=F32)
        o_ref[:, cols] = (y * ps_ref[:, cols]).astype(BF16)


def _pool_grid_kernel(u_ref, a_ref, invc_ref, pw_ref, ps_ref, o_ref, col_scr, diff_scr,
                      *, n_rows):
    blk = a_ref.shape[1]
    rpb = blk // GRID_W
    for gi, w in enumerate(POOL_WINDOWS):
        cols = slice(gi * POOL_GW, (gi + 1) * POOL_GW)
        for rb in range(n_rows // rpb):
            rows = slice(rb * blk, (rb + 1) * blk)
            tot = jnp.dot(a_ref[gi], u_ref[rows, cols], preferred_element_type=F32)
            col_scr[gi, rows, :] = tot * invc_ref[gi]
    for gi, w in enumerate(POOL_WINDOWS):
        cols = slice(gi * POOL_GW, (gi + 1) * POOL_GW)
        lo, hi = _window(n_rows, w)

        def grid_row(rr):
            return col_scr[gi, rr * GRID_W:(rr + 1) * GRID_W, :]

        acc = None
        for r in range(n_rows):
            if r == 0:
                acc = grid_row(int(lo[0]))
                for rr in range(int(lo[0]) + 1, int(hi[0])):
                    acc = acc + grid_row(rr)
            else:
                for rr in range(int(hi[r - 1]), int(hi[r])):
                    acc = acc + grid_row(rr)
                for rr in range(int(lo[r - 1]), int(lo[r])):
                    acc = acc - grid_row(rr)
            rows = slice(r * GRID_W, (r + 1) * GRID_W)
            mean = acc * (1.0 / float(hi[r] - lo[r]))
            diff_scr[rows, cols] = (mean - u_ref[rows, cols].astype(F32)).astype(BF16)
    for gi in range(len(POOL_WINDOWS)):
        cols = slice(gi * POOL_GW, (gi + 1) * POOL_GW)
        y = jnp.dot(diff_scr[:, cols], pw_ref[gi], preferred_element_type=F32)
        o_ref[:, cols] = (y * ps_ref[:, cols]).astype(BF16)


def _pool_call(p, pool_w, pool_scale, n_seq, seq_len, two_d):
    n_tok = p.shape[0]
    ng = len(POOL_WINDOWS)
    if two_d:
        n_rows = seq_len // GRID_W
        rpb = 4
        blk = rpb * GRID_W
        bands = [_band(GRID_W, w) for w in POOL_WINDOWS]
        a = np.stack([np.kron(np.eye(rpb, dtype=np.float32), b[0]) for b in bands])
        inv = np.stack([np.broadcast_to(np.tile(1.0 / b[1], rpb)[:, None], (blk, POOL_GW))
                        for b in bands]).astype(np.float32)
        kern = functools.partial(_pool_grid_kernel, n_rows=n_rows)
        tb = seq_len
        scratch = [pltpu.VMEM((ng, seq_len, POOL_GW), F32),
                   pltpu.VMEM((seq_len, D_POOL), BF16)]
    else:
        blk = seq_len
        bands = [_band(seq_len, w) for w in POOL_WINDOWS]
        a = np.stack([b[0] for b in bands])
        inv = np.stack([np.broadcast_to((1.0 / b[1])[:, None], (blk, POOL_GW))
                        for b in bands]).astype(np.float32)
        n_seq_blk = 8
        kern = functools.partial(_pool_seq_kernel, n_seq_blk=n_seq_blk, seq_len=seq_len)
        tb = n_seq_blk * seq_len
        scratch = [pltpu.VMEM((tb, D_POOL), BF16)]
    return pl.pallas_call(
        kern,
        grid=(n_tok // tb,),
        in_specs=[
            pl.BlockSpec((tb, D_POOL), lambda i: (i, 6)),
            _const_spec((ng, blk, blk)),
            _const_spec((ng, blk, POOL_GW)),
            _const_spec((ng, POOL_GW, POOL_GW)),
            _const_spec((1, D_POOL)),
        ],
        out_specs=pl.BlockSpec((tb, D_POOL), lambda i: (i, 0)),
        out_shape=jax.ShapeDtypeStruct((n_tok, D_POOL), BF16),
        scratch_shapes=scratch,
        compiler_params=_params(1),
        name="pool",
    )(p, jnp.asarray(a, BF16), jnp.asarray(inv, F32), pool_w, pool_scale)


def _layer_norm(u, g, b):
    mu = jnp.mean(u, axis=-1, keepdims=True)
    d = u - mu
    var = jnp.mean(d * d, axis=-1, keepdims=True)
    return d * lax.rsqrt(var + LN_EPS) * g + b


def _tail_kernel(x_ref, of_ref, ob_ref, gs_ref, yp_ref, g1_ref, sh2_ref, sc2_ref,
                 g2_ref, gn_ref, wo_ref, l1g_ref, l1b_ref, wgu_ref, wd_ref,
                 l2g_ref, l2b_ref, o_ref, yr_scr, u_scr, h2_scr):
    tm = x_ref.shape[0]
    gn = gn_ref[...]

    th = tm // 2
    half = [slice(0, th), slice(th, tm)]
    n_rb = th // ROWS
    ff_slices = [slice(lo, hi) for lo, hi in zip(FF_SPLITS[:-1], FF_SPLITS[1:])]
    assert n_rb % len(ff_slices) == 0
    per_fill = n_rb // len(ff_slices)

    def gated_norm_block(i, m):
        rows = slice(i * th + m * ROWS, i * th + (m + 1) * ROWS)
        for h in range(HEADS):
            cols = slice(h * D_HEAD, (h + 1) * D_HEAD)
            o = of_ref[rows, cols].astype(F32) + ob_ref[rows, cols].astype(F32)
            ms = jnp.mean(o * o, axis=-1, keepdims=True)
            on = o * lax.rsqrt(ms + RMS_EPS) * gn
            yr_scr[rows, cols] = (on * gs_ref[rows, cols].astype(F32)).astype(BF16)

    def out_proj(hr, fill):
        y = jnp.dot(yr_scr[hr, :], wo_ref[0:D_REC, :], preferred_element_type=F32)
        fill(0)
        y = y + jnp.dot(yp_ref[hr, :], wo_ref[D_REC:2 * D_REC, :],
                        preferred_element_type=F32)
        fill(1)
        return ALPHA * x_ref[hr, :] + g1_ref[0] * y

    def norm1_block(i, m):
        rows = slice(i * th + m * ROWS, i * th + (m + 1) * ROWS)
        x1 = _layer_norm(u_scr[rows, :], l1g_ref[...], l1b_ref[...])
        u_scr[rows, :] = x1
        h2_scr[rows, :] = (x1 * (1.0 + sc2_ref[0]) + sh2_ref[0]).astype(BF16)

    def norm2_block(i, m):
        rows = slice(i * th + m * ROWS, i * th + (m + 1) * ROWS)
        o_ref[rows, :] = _layer_norm(u_scr[rows, :], l2g_ref[...], l2b_ref[...])

    def ffn_half(i, fill):
        hr = half[i]
        h2 = h2_scr[hr, :]
        ffn = None
        for j, cs in enumerate(ff_slices):
            ab = jnp.dot(h2, wgu_ref[:, 2 * cs.start:2 * cs.stop],
                         preferred_element_type=F32)
            fill(j)
            acts = []
            for t in range((cs.stop - cs.start) // MXU_TILE):
                a = ab[:, 2 * t * MXU_TILE:(2 * t + 1) * MXU_TILE]
                b = ab[:, (2 * t + 1) * MXU_TILE:(2 * t + 2) * MXU_TILE]
                acts.append((a * jax.nn.sigmoid(a) * b).astype(BF16))
            part = jnp.dot(jnp.concatenate(acts, axis=1), wd_ref[cs, :],
                           preferred_element_type=F32)
            ffn = part if ffn is None else ffn + part
        u_scr[hr, :] = ALPHA * u_scr[hr, :] + g2_ref[0] * ffn

    def in_two(block, i):
        def emit(j):
            for m in range(j * n_rb // 2, (j + 1) * n_rb // 2):
                block(i, m)
        return emit

    for m in range(n_rb):
        gated_norm_block(0, m)
    u_scr[half[0], :] = out_proj(half[0], in_two(gated_norm_block, 1))
    u_scr[half[1], :] = out_proj(half[1], in_two(norm1_block, 0))
    def spread(block, i):
        def emit(j):
            for m in range(j * per_fill, (j + 1) * per_fill):
                block(i, m)
        return emit

    ffn_half(0, spread(norm1_block, 1))
    ffn_half(1, spread(norm2_block, 0))
    for m in range(n_rb):
        norm2_block(1, m)


def _tail_call(x2, o_f, o_b, p, yp, mod3, mod_row, gn, wo, l1g, l1b, wgu, wd,
               l2g, l2b, seq_len):
    n_tok = x2.shape[0]
    tm = TM_TAIL
    tps = max(1, seq_len // tm)

    def mod_spec(col):
        return pl.BlockSpec((1, 1, D_MODEL), lambda i: (mod_row(i, tps), 0, col))

    rec_spec = pl.BlockSpec((tm, D_REC), lambda i: (i, 0))
    return pl.pallas_call(
        _tail_kernel,
        grid=(n_tok // tm,),
        in_specs=[
            pl.BlockSpec((tm, D_MODEL), lambda i: (i, 0)),
            rec_spec, rec_spec,
            pl.BlockSpec((tm, D_REC), lambda i: (i, 5)),
            pl.BlockSpec((tm, D_POOL), lambda i: (i, 0)),
            mod_spec(2), mod_spec(3), mod_spec(4), mod_spec(5),
            _const_spec((1, D_HEAD)),
            _const_spec((D_MODEL, D_MODEL)),
            _const_spec((1, D_MODEL)), _const_spec((1, D_MODEL)),
            _const_spec((D_MODEL, 2 * D_FF)),
            _const_spec((D_FF, D_MODEL)),
            _const_spec((1, D_MODEL)), _const_spec((1, D_MODEL)),
        ],
        out_specs=pl.BlockSpec((tm, D_MODEL), lambda i: (i, 0)),
        out_shape=jax.ShapeDtypeStruct((n_tok, D_MODEL), F32),
        scratch_shapes=[
            pltpu.VMEM((tm, D_REC), BF16),
            pltpu.VMEM((tm, D_MODEL), F32),
            pltpu.VMEM((tm, D_MODEL), BF16),
        ],
        compiler_params=_params(1),
        name="tail",
    )(x2, o_f, o_b, p, yp, mod3, mod3, mod3, mod3, gn, wo, l1g, l1b, wgu, wd,
      l2g, l2b)


def kernel(x_prompt, x_sample, state_hgrn_fwd, state_hgrn_bwd, c, c_ctx, w_mod, b_mod,
           w_in, lb_fwd_raw, lb_bwd_raw, hgrn_norm_g, pool_w, pool_scale, w_out, ln1_g,
           ln1_b, w_ffn_gate, w_ffn_up, w_ffn_down, ln2_g, ln2_b):
    n_ctx, t_ctx, _ = x_prompt.shape
    n_lat, t_lat, _ = x_sample.shape
    ctx_row = n_lat

    c_all = jnp.concatenate(
        [c, c_ctx[None, :], jnp.zeros((16 - n_lat - 1, D_MODEL), F32)], axis=0)
    mod = _mod_call(c_all, w_mod[0], b_mod)
    mod3 = mod.reshape(16, 1, 6 * D_MODEL)

    w_in_b = w_in[0].astype(BF16)
    wo_b = w_out[0].astype(BF16)
    n_t = D_FF // MXU_TILE
    wgu_b = jnp.stack([w_ffn_gate[0].reshape(D_MODEL, n_t, MXU_TILE),
                       w_ffn_up[0].reshape(D_MODEL, n_t, MXU_TILE)],
                      axis=2).reshape(D_MODEL, 2 * D_FF).astype(BF16)
    wd_b = w_ffn_down[0].astype(BF16)
    pw_b = pool_w[0].astype(BF16)

    def run(x, n_seq, seq_len, mod_row, two_d, s0f, s0b, emit_final):
        x2 = x.reshape(n_seq * seq_len, D_MODEL)
        p, e = _proj_call(x2, mod3, mod_row, w_in_b, lb_fwd_raw, lb_bwd_raw, n_seq, seq_len)
        scan = _scan_call(p, e, n_seq, seq_len, s0f, s0b, emit_final)
        yp = _pool_call(p, pw_b, pool_scale, n_seq, seq_len, two_d)
        y = _tail_call(x2, scan[0], scan[1], p, yp, mod3, mod_row, hgrn_norm_g, wo_b,
                       ln1_g, ln1_b, wgu_b, wd_b, ln2_g, ln2_b, seq_len)
        return y.reshape(n_seq, seq_len, D_MODEL), scan[2:]

    y_ctx, fin = run(x_prompt, n_ctx, t_ctx, lambda i, tps: ctx_row, False,
                     None, None, True)
    y_lat, _ = run(x_sample, n_lat, t_lat, lambda i, tps: i // tps, True,
                   state_hgrn_fwd[:, 0], state_hgrn_bwd[:, 0], False)
    return (y_ctx, y_lat, fin[0][:, None], fin[1][:, None])
```

```python
import functools

import numpy as np
import jax
import jax.numpy as jnp
from jax import lax
from jax.experimental import pallas as pl
from jax.experimental.pallas import tpu as pltpu

F32 = jnp.float32
BF16 = jnp.bfloat16

D_MODEL = 1024
D_REC = 512
D_POOL = 512
HEADS = 4
D_HEAD = 128
D_IN = 5 * D_REC + D_POOL
D_FF = 2816
GRID_W = 64
POOL_WINDOWS = (2, 4, 8, 16)
POOL_GW = 128
ALPHA = 2.0 ** 0.25
LN_EPS = 1e-5
RMS_EPS = 1e-6

CHUNK = 64
MID_F = CHUNK // 2 - 1
MID_B = CHUNK // 2
TM_PROJ = 1024
TB_SCAN = 1024
SEQ_PER_SCAN = 4
TM_TAIL = 512
ROWS = 64
FF_SPLITS = (0, 1536, D_FF)
P_COLS = 7 * D_REC
E_COLS = 4 * D_REC
VMEM_LIMIT = 56 * 1024 * 1024

_ARB = "arbitrary"


def _params(n_axes):
    return pltpu.CompilerParams(
        dimension_semantics=(_ARB,) * n_axes, vmem_limit_bytes=VMEM_LIMIT)


def _const_spec(shape):
    nd = len(shape)
    return pl.BlockSpec(shape, lambda *_: (0,) * nd, pipeline_mode=pl.Buffered(1))


def _mod_kernel(c_ref, w_ref, b_ref, o_ref):
    c = c_ref[...]
    hi, lo = _split_bf16(c * jax.nn.sigmoid(c))
    w = w_ref[...].astype(BF16)
    o_ref[...] = (jnp.dot(hi, w, preferred_element_type=F32)
                  + jnp.dot(lo, w, preferred_element_type=F32) + b_ref[...])


def _mod_call(c_all, w_mod, b_mod):
    rows = c_all.shape[0]
    n_out = w_mod.shape[1]
    blk = 1024
    return pl.pallas_call(
        _mod_kernel,
        grid=(n_out // blk,),
        in_specs=[
            pl.BlockSpec((rows, D_MODEL), lambda j: (0, 0)),
            pl.BlockSpec((D_MODEL, blk), lambda j: (0, j)),
            pl.BlockSpec((1, blk), lambda j: (0, j)),
        ],
        out_specs=pl.BlockSpec((rows, blk), lambda j: (0, j)),
        out_shape=jax.ShapeDtypeStruct((rows, n_out), F32),
        compiler_params=_params(1),
        name="mod",
    )(c_all, w_mod, b_mod)


def _lower_bound(raw_ref):
    r0 = raw_ref[0:1, :]
    r1 = raw_ref[1:2, :]
    mx = jnp.maximum(r0, r1)
    e0 = jnp.exp(r0 - mx)
    e1 = jnp.exp(r1 - mx)
    return e0 / (e0 + e1)


def _split_bf16(a):
    hi = a.astype(BF16)
    lo = (a - hi.astype(F32)).astype(BF16)
    return hi, lo


def _proj_kernel(x_ref, sh_ref, sc_ref, w_ref, lbf_ref, lbb_ref, p_ref, e_ref,
                 proj_scr, *, cps):
    tm = x_ref.shape[0]
    h = (x_ref[...] * (1.0 + sc_ref[0]) + sh_ref[0]).astype(BF16)
    proj_scr[...] = jnp.dot(h, w_ref[...], preferred_element_type=F32)

    lb_f = _lower_bound(lbf_ref)
    lb_b = _lower_bound(lbb_ref)
    ri = lax.broadcasted_iota(jnp.int32, (CHUNK, CHUNK), 0)
    ci = lax.broadcasted_iota(jnp.int32, (CHUNK, CHUNK), 1)
    tri_pre = jnp.where(ci <= ri, 1.0, 0.0).astype(BF16)
    tri_suf = jnp.where(ci >= ri, 1.0, 0.0).astype(BF16)

    def chunk_sum(tri, a):
        hi, lo = _split_bf16(a)
        return (jnp.dot(tri, hi, preferred_element_type=F32)
                + jnp.dot(tri, lo, preferred_element_type=F32))

    for c in range(tm // CHUNK):
        rows = slice(c * CHUNK, (c + 1) * CHUNK)
        q = proj_scr[rows, 0:D_REC]
        qs = q * jax.nn.sigmoid(q)

        def direction(z, lb, tri, mid, end):
            f = lb + (1.0 - lb) * jax.nn.sigmoid(z)
            k = 1.0 - f
            b = chunk_sum(tri, jnp.log(f))
            m = b[mid:mid + 1, :]
            d = b - m
            qd = qs * jnp.exp(d)
            ki = k * jnp.exp(-d)
            em = jnp.exp(m)
            r = jnp.exp(b[end:end + 1, :] - m)
            return qd, ki, em, r

        qd_f, ki_f, em_f, r_f = direction(
            proj_scr[rows, D_REC:2 * D_REC], lb_f, tri_pre, MID_F, CHUNK - 1)
        qd_b, ki_b, em_b, r_b = direction(
            proj_scr[rows, 2 * D_REC:3 * D_REC], lb_b, tri_suf, MID_B, 0)
        g = proj_scr[rows, 4 * D_REC:5 * D_REC]

        p_ref[rows, 0 * D_REC:1 * D_REC] = qd_f.astype(BF16)
        p_ref[rows, 1 * D_REC:2 * D_REC] = ki_f.astype(BF16)
        p_ref[rows, 2 * D_REC:3 * D_REC] = qd_b.astype(BF16)
        p_ref[rows, 3 * D_REC:4 * D_REC] = ki_b.astype(BF16)
        p_ref[rows, 4 * D_REC:5 * D_REC] = proj_scr[rows, 3 * D_REC:4 * D_REC].astype(BF16)
        p_ref[rows, 5 * D_REC:6 * D_REC] = (g * jax.nn.sigmoid(g)).astype(BF16)
        p_ref[rows, 6 * D_REC:7 * D_REC] = proj_scr[rows, 5 * D_REC:6 * D_REC].astype(BF16)

        s, cc = c // cps, c % cps
        e_ref[s, cc:cc + 1, 0 * D_REC:1 * D_REC] = em_f
        e_ref[s, cc:cc + 1, 1 * D_REC:2 * D_REC] = r_f
        e_ref[s, cc:cc + 1, 2 * D_REC:3 * D_REC] = em_b
        e_ref[s, cc:cc + 1, 3 * D_REC:4 * D_REC] = r_b


def _proj_call(x2, mod3, mod_row, w_in, lbf_raw, lbb_raw, n_seq, seq_len):
    n_tok = x2.shape[0]
    tm = TM_PROJ
    cpt = tm // CHUNK
    cps = min(cpt, seq_len // CHUNK)
    spt = cpt // cps
    tps = max(1, seq_len // tm)
    n_ck = seq_len // CHUNK

    if spt > 1:
        e_spec = pl.BlockSpec((spt, cps, E_COLS), lambda i: (i, 0, 0))
    else:
        e_spec = pl.BlockSpec((1, cps, E_COLS), lambda i: (i // tps, i % tps, 0))

    return pl.pallas_call(
        functools.partial(_proj_kernel, cps=cps),
        grid=(n_tok // tm,),
        in_specs=[
            pl.BlockSpec((tm, D_MODEL), lambda i: (i, 0)),
            pl.BlockSpec((1, 1, D_MODEL), lambda i: (mod_row(i, tps), 0, 0)),
            pl.BlockSpec((1, 1, D_MODEL), lambda i: (mod_row(i, tps), 0, 1)),
            _const_spec((D_MODEL, D_IN)),
            _const_spec((2, D_REC)),
            _const_spec((2, D_REC)),
        ],
        out_specs=[
            pl.BlockSpec((tm, P_COLS), lambda i: (i, 0)),
            e_spec,
        ],
        out_shape=[
            jax.ShapeDtypeStruct((n_tok, P_COLS), BF16),
            jax.ShapeDtypeStruct((n_seq, n_ck, E_COLS), F32),
        ],
        scratch_shapes=[pltpu.VMEM((tm, D_IN), F32)],
        compiler_params=_params(1),
        name="proj",
    )(x2, mod3, mod3, w_in, lbf_raw, lbb_raw)


def _scan_kernel(*refs, nck, spb, has_init, emit_final):
    it = iter(refs)
    qdf_ref, kif_ref, vf_ref, ef_ref = (next(it) for _ in range(4))
    qdb_ref, kib_ref, vb_ref, eb_ref = (next(it) for _ in range(4))
    if has_init:
        s0f_ref, s0b_ref = next(it), next(it)
    of_ref, ob_ref = next(it), next(it)
    if emit_final:
        finf_ref, finb_ref = next(it), next(it)
    sf_scr, sb_scr, uf_scr, ub_scr, scf_scr, scb_scr, tf_scr, tb_scr = (
        next(it) for _ in range(8))

    j = pl.program_id(1)
    seq_heads = [(q, h) for q in range(spb) for h in range(HEADS)]
    ri = lax.broadcasted_iota(jnp.int32, (CHUNK, CHUNK), 0)
    ci = lax.broadcasted_iota(jnp.int32, (CHUNK, CHUNK), 1)
    nt_dims = (((1,), (1,)), ((), ()))
    tn_dims = (((0,), (0,)), ((), ()))

    @pl.when(j == 0)
    def _():
        for q, h in seq_heads:
            if has_init:
                sf_scr[q, h] = s0f_ref[q, h].T
                sb_scr[q, h] = s0b_ref[q, h].T
            else:
                sf_scr[q, h] = jnp.zeros((D_HEAD, D_HEAD), F32)
                sb_scr[q, h] = jnp.zeros((D_HEAD, D_HEAD), F32)

    def tile(q, c, h):
        return (slice((q * nck + c) * CHUNK, (q * nck + c + 1) * CHUNK),
                slice(h * D_HEAD, (h + 1) * D_HEAD))

    def prepare(qd_ref, ki_ref, v_ref, keep, u_scr, sc_scr):
        for q in range(spb):
            for c in range(nck):
                for h in range(HEADS):
                    rows, cols = tile(q, c, h)
                    u_scr[q, c, h] = lax.dot_general(
                        v_ref[rows, cols], ki_ref[rows, cols], tn_dims,
                        preferred_element_type=F32)
                    sc = lax.dot_general(qd_ref[rows, cols], ki_ref[rows, cols], nt_dims,
                                         preferred_element_type=F32)
                    sc_scr[q, c, h] = jnp.where(keep, sc, 0.0).astype(BF16)

    def step(qd_ref, v_ref, e_ref, o_ref, s_scr, u_scr, sc_scr, smt_scr, c):
        for q, h in seq_heads:
            rows, cols = tile(q, c, h)
            em = e_ref[q, c:c + 1, h * D_HEAD:(h + 1) * D_HEAD]
            r = e_ref[q, c:c + 1, D_REC + h * D_HEAD:D_REC + (h + 1) * D_HEAD]
            sm = s_scr[q, h] * em
            smt_scr[q, h] = sm.astype(BF16).T
            o = jnp.dot(qd_ref[rows, cols], smt_scr[q, h], preferred_element_type=F32)
            o = o + jnp.dot(sc_scr[q, c, h], v_ref[rows, cols],
                            preferred_element_type=F32)
            o_ref[rows, cols] = o.astype(BF16)
            s_scr[q, h] = (sm + u_scr[q, c, h]) * r

    prepare(qdf_ref, kif_ref, vf_ref, ci <= ri, uf_scr, scf_scr)
    prepare(qdb_ref, kib_ref, vb_ref, ci >= ri, ub_scr, scb_scr)
    for c in range(nck):
        step(qdf_ref, vf_ref, ef_ref, of_ref, sf_scr, uf_scr, scf_scr, tf_scr, c)
        step(qdb_ref, vb_ref, eb_ref, ob_ref, sb_scr, ub_scr, scb_scr, tb_scr, nck - 1 - c)

    if emit_final:
        @pl.when(j == pl.num_programs(1) - 1)
        def _():
            for q, h in seq_heads:
                finf_ref[q, h] = sf_scr[q, h].T
                finb_ref[q, h] = sb_scr[q, h].T


def _scan_call(p, e, n_seq, seq_len, s0f=None, s0b=None, emit_final=False):
    tb = min(seq_len, TB_SCAN)
    nt = seq_len // tb
    spb = SEQ_PER_SCAN if nt == 1 else 1
    nck = tb // CHUNK
    has_init = s0f is not None
    st_blk = (spb, HEADS, D_HEAD, D_HEAD)
    rows = spb * tb
    n_tok = p.shape[0]

    def fwd(b, j):
        return b * nt + j

    def bwd(b, j):
        return b * nt + nt - 1 - j

    def p_spec(blk, col):
        return pl.BlockSpec((rows, D_REC), lambda b, j: (blk(b, j), col))

    in_specs = [
        p_spec(fwd, 0), p_spec(fwd, 1), p_spec(fwd, 4),
        pl.BlockSpec((spb, nck, 2 * D_REC), lambda b, j: (b, j, 0)),
        p_spec(bwd, 2), p_spec(bwd, 3), p_spec(bwd, 4),
        pl.BlockSpec((spb, nck, 2 * D_REC), lambda b, j: (b, nt - 1 - j, 1)),
    ]
    args = [p, p, p, e, p, p, p, e]
    if has_init:
        in_specs += [pl.BlockSpec(st_blk, lambda b, j: (b, 0, 0, 0))] * 2
        args += [s0f, s0b]
    o_shape = jax.ShapeDtypeStruct((n_tok, D_REC), BF16)
    out_specs = [
        pl.BlockSpec((rows, D_REC), lambda b, j: (fwd(b, j), 0)),
        pl.BlockSpec((rows, D_REC), lambda b, j: (bwd(b, j), 0)),
    ]
    out_shape = [o_shape, o_shape]
    if emit_final:
        out_specs += [pl.BlockSpec(st_blk, lambda b, j: (b, 0, 0, 0))] * 2
        out_shape += [jax.ShapeDtypeStruct((n_seq, HEADS, D_HEAD, D_HEAD), F32)] * 2

    return pl.pallas_call(
        functools.partial(_scan_kernel, nck=nck, spb=spb, has_init=has_init,
                          emit_final=emit_final),
        grid=(n_seq // spb, nt),
        in_specs=in_specs,
        out_specs=out_specs,
        out_shape=out_shape,
        scratch_shapes=([pltpu.VMEM((spb, HEADS, D_HEAD, D_HEAD), F32)] * 2
                        + [pltpu.VMEM((spb, nck, HEADS, D_HEAD, D_HEAD), F32)] * 2
                        + [pltpu.VMEM((spb, nck, HEADS, CHUNK, CHUNK), BF16)] * 2
                        + [pltpu.VMEM((spb, HEADS, D_HEAD, D_HEAD), BF16)] * 2),
        compiler_params=_params(2),
        name="scan",
    )(*args)


def _window(n, w):
    idx = np.arange(n)
    return np.clip(idx - w // 2, 0, n), np.clip(idx + w // 2, 0, n)


def _band(n, w):
    lo, hi = _window(n, w)
    j = np.arange(n)[None, :]
    return ((j >= lo[:, None]) & (j < hi[:, None])).astype(np.float32), (hi - lo)


def _pool_seq_kernel(u_ref, a_ref, inv_ref, pw_ref, ps_ref, o_ref, diff_scr, *,
                     n_seq_blk, seq_len):
    for gi in range(len(POOL_WINDOWS)):
        cols = slice(gi * POOL_GW, (gi + 1) * POOL_GW)
        for s in range(n_seq_blk):
            rows = slice(s * seq_len, (s + 1) * seq_len)
            ug = u_ref[rows, cols]
            tot = jnp.dot(a_ref[gi], ug, preferred_element_type=F32)
            diff_scr[rows, cols] = (tot * inv_ref[gi] - ug.astype(F32)).astype(BF16)
    for gi in range(len(POOL_WINDOWS)):
        cols = slice(gi * POOL_GW, (gi + 1) * POOL_GW)
        y = jnp.dot(diff_scr[:, cols], pw_ref[gi], preferred_element_type=F32)
        o_ref[:, cols] = (y * ps_ref[:, cols]).astype(BF16)


def _pool_grid_kernel(u_ref, a_ref, invc_ref, pw_ref, ps_ref, o_ref, col_scr, diff_scr,
                      *, n_rows):
    blk = a_ref.shape[1]
    rpb = blk // GRID_W
    for gi, w in enumerate(POOL_WINDOWS):
        cols = slice(gi * POOL_GW, (gi + 1) * POOL_GW)
        for rb in range(n_rows // rpb):
            rows = slice(rb * blk, (rb + 1) * blk)
            tot = jnp.dot(a_ref[gi], u_ref[rows, cols], preferred_element_type=F32)
            col_scr[gi, rows, :] = tot * invc_ref[gi]
    for gi, w in enumerate(POOL_WINDOWS):
        cols = slice(gi * POOL_GW, (gi + 1) * POOL_GW)
        lo, hi = _window(n_rows, w)

        def grid_row(rr):
            return col_scr[gi, rr * GRID_W:(rr + 1) * GRID_W, :]

        acc = None
        for r in range(n_rows):
            if r == 0:
                acc = grid_row(int(lo[0]))
                for rr in range(int(lo[0]) + 1, int(hi[0])):
                    acc = acc + grid_row(rr)
            else:
                for rr in range(int(hi[r - 1]), int(hi[r])):
                    acc = acc + grid_row(rr)
                for rr in range(int(lo[r - 1]), int(lo[r])):
                    acc = acc - grid_row(rr)
            rows = slice(r * GRID_W, (r + 1) * GRID_W)
            mean = acc * (1.0 / float(hi[r] - lo[r]))
            diff_scr[rows, cols] = (mean - u_ref[rows, cols].astype(F32)).astype(BF16)
    for gi in range(len(POOL_WINDOWS)):
        cols = slice(gi * POOL_GW, (gi + 1) * POOL_GW)
        y = jnp.dot(diff_scr[:, cols], pw_ref[gi], preferred_element_type=F32)
        o_ref[:, cols] = (y * ps_ref[:, cols]).astype(BF16)


def _pool_call(p, pool_w, pool_scale, n_seq, seq_len, two_d):
    n_tok = p.shape[0]
    ng = len(POOL_WINDOWS)
    if two_d:
        n_rows = seq_len // GRID_W
        rpb = 4
        blk = rpb * GRID_W
        bands = [_band(GRID_W, w) for w in POOL_WINDOWS]
        a = np.stack([np.kron(np.eye(rpb, dtype=np.float32), b[0]) for b in bands])
        inv = np.stack([np.broadcast_to(np.tile(1.0 / b[1], rpb)[:, None], (blk, POOL_GW))
                        for b in bands]).astype(np.float32)
        kern = functools.partial(_pool_grid_kernel, n_rows=n_rows)
        tb = seq_len
        scratch = [pltpu.VMEM((ng, seq_len, POOL_GW), F32),
                   pltpu.VMEM((seq_len, D_POOL), BF16)]
    else:
        blk = seq_len
        bands = [_band(seq_len, w) for w in POOL_WINDOWS]
        a = np.stack([b[0] for b in bands])
        inv = np.stack([np.broadcast_to((1.0 / b[1])[:, None], (blk, POOL_GW))
                        for b in bands]).astype(np.float32)
        n_seq_blk = 8
        kern = functools.partial(_pool_seq_kernel, n_seq_blk=n_seq_blk, seq_len=seq_len)
        tb = n_seq_blk * seq_len
        scratch = [pltpu.VMEM((tb, D_POOL), BF16)]
    return pl.pallas_call(
        kern,
        grid=(n_tok // tb,),
        in_specs=[
            pl.BlockSpec((tb, D_POOL), lambda i: (i, 6)),
            _const_spec((ng, blk, blk)),
            _const_spec((ng, blk, POOL_GW)),
            _const_spec((ng, POOL_GW, POOL_GW)),
            _const_spec((1, D_POOL)),
        ],
        out_specs=pl.BlockSpec((tb, D_POOL), lambda i: (i, 0)),
        out_shape=jax.ShapeDtypeStruct((n_tok, D_POOL), BF16),
        scratch_shapes=scratch,
        compiler_params=_params(1),
        name="pool",
    )(p, jnp.asarray(a, BF16), jnp.asarray(inv, F32), pool_w, pool_scale)


def _layer_norm(u, g, b):
    mu = jnp.mean(u, axis=-1, keepdims=True)
    d = u - mu
    var = jnp.mean(d * d, axis=-1, keepdims=True)
    return d * lax.rsqrt(var + LN_EPS) * g + b


def _tail_kernel(x_ref, of_ref, ob_ref, gs_ref, yp_ref, g1_ref, sh2_ref, sc2_ref,
                 g2_ref, gn_ref, wo_ref, l1g_ref, l1b_ref, wg_ref, wu_ref, wd_ref,
                 l2g_ref, l2b_ref, o_ref, yr_scr, u_scr, h2_scr, u2_scr):
    tm = x_ref.shape[0]
    gn = gn_ref[...]
    step = pl.program_id(0)
    last = pl.num_programs(0) - 1

    th = tm // 2
    half = [slice(0, th), slice(th, tm)]
    n_rb = th // ROWS
    ff_slices = [slice(lo, hi) for lo, hi in zip(FF_SPLITS[:-1], FF_SPLITS[1:])]
    assert n_rb == 2 * len(ff_slices)

    def gated_norm_block(i, m):
        rows = slice(i * th + m * ROWS, i * th + (m + 1) * ROWS)
        for h in range(HEADS):
            cols = slice(h * D_HEAD, (h + 1) * D_HEAD)
            o = of_ref[rows, cols].astype(F32) + ob_ref[rows, cols].astype(F32)
            ms = jnp.mean(o * o, axis=-1, keepdims=True)
            on = o * lax.rsqrt(ms + RMS_EPS) * gn
            yr_scr[rows, cols] = (on * gs_ref[rows, cols].astype(F32)).astype(BF16)

    def out_proj(hr, fill):
        y = jnp.dot(yr_scr[hr, :], wo_ref[0:D_REC, :], preferred_element_type=F32)
        fill(0)
        y = y + jnp.dot(yp_ref[hr, :], wo_ref[D_REC:2 * D_REC, :],
                        preferred_element_type=F32)
        fill(1)
        return ALPHA * x_ref[hr, :] + g1_ref[0] * y

    def norm1_block(i, m):
        rows = slice(i * th + m * ROWS, i * th + (m + 1) * ROWS)
        x1 = _layer_norm(u_scr[rows, :], l1g_ref[...], l1b_ref[...])
        u_scr[rows, :] = x1
        h2_scr[rows, :] = (x1 * (1.0 + sc2_ref[0]) + sh2_ref[0]).astype(BF16)

    def norm2_block(i, m):
        rows = slice(i * th + m * ROWS, i * th + (m + 1) * ROWS)
        o_ref[rows, :] = _layer_norm(u2_scr[rows, :], l2g_ref[...], l2b_ref[...])

    def ffn_half(i, fill):
        hr = half[i]
        h2 = h2_scr[hr, :]
        ffn = None
        j = 0
        for cs in ff_slices:
            a = jnp.dot(h2, wg_ref[:, cs], preferred_element_type=F32)
            fill(j); j += 1
            b = jnp.dot(h2, wu_ref[:, cs], preferred_element_type=F32)
            fill(j); j += 1
            act = (a * jax.nn.sigmoid(a) * b).astype(BF16)
            part = jnp.dot(act, wd_ref[cs, :], preferred_element_type=F32)
            ffn = part if ffn is None else ffn + part
        u2_scr[hr, :] = ALPHA * u_scr[hr, :] + g2_ref[0] * ffn

    def in_two(block, i):
        def emit(j):
            for m in range(j * n_rb // 2, (j + 1) * n_rb // 2):
                block(i, m)
        return emit

    @pl.when(step == 0)
    def _():
        u2_scr[...] = jnp.zeros_like(u2_scr)

    def norm1_b_and_prev_a(j):
        norm1_block(1, j)
        norm2_block(0, j)

    @pl.when(step < last)
    def _():
        for m in range(n_rb):
            gated_norm_block(0, m)
        u_scr[half[0], :] = out_proj(half[0], in_two(gated_norm_block, 1))
        u_scr[half[1], :] = out_proj(half[1], in_two(norm1_block, 0))
        ffn_half(0, norm1_b_and_prev_a)
        ffn_half(1, lambda j: norm2_block(1, j))

    @pl.when(step == last)
    def _():
        for k in range(2):
            for m in range(n_rb):
                norm2_block(k, m)


def _tail_call(x2, o_f, o_b, p, yp, mod3, mod_row, gn, wo, l1g, l1b, wg, wu, wd,
               l2g, l2b, seq_len):
    n_tok = x2.shape[0]
    tm = TM_TAIL
    tps = max(1, seq_len // tm)
    n_tiles = n_tok // tm

    def cur(i):
        return jnp.minimum(i, n_tiles - 1)

    def prev(i):
        return jnp.maximum(i - 1, 0)

    def mod_spec(col):
        return pl.BlockSpec((1, 1, D_MODEL), lambda i: (mod_row(cur(i), tps), 0, col))

    rec_spec = pl.BlockSpec((tm, D_REC), lambda i: (cur(i), 0))
    return pl.pallas_call(
        _tail_kernel,
        grid=(n_tiles + 1,),
        in_specs=[
            pl.BlockSpec((tm, D_MODEL), lambda i: (cur(i), 0)),
            rec_spec, rec_spec,
            pl.BlockSpec((tm, D_REC), lambda i: (cur(i), 5)),
            pl.BlockSpec((tm, D_POOL), lambda i: (cur(i), 0)),
            mod_spec(2), mod_spec(3), mod_spec(4), mod_spec(5),
            _const_spec((1, D_HEAD)),
            _const_spec((D_MODEL, D_MODEL)),
            _const_spec((1, D_MODEL)), _const_spec((1, D_MODEL)),
            _const_spec((D_MODEL, D_FF)), _const_spec((D_MODEL, D_FF)),
            _const_spec((D_FF, D_MODEL)),
            _const_spec((1, D_MODEL)), _const_spec((1, D_MODEL)),
        ],
        out_specs=pl.BlockSpec((tm, D_MODEL), lambda i: (prev(i), 0)),
        out_shape=jax.ShapeDtypeStruct((n_tok, D_MODEL), F32),
        scratch_shapes=[
            pltpu.VMEM((tm, D_REC), BF16),
            pltpu.VMEM((tm, D_MODEL), F32),
            pltpu.VMEM((tm, D_MODEL), BF16),
            pltpu.VMEM((tm, D_MODEL), F32),
        ],
        compiler_params=_params(1),
        name="tail",
    )(x2, o_f, o_b, p, yp, mod3, mod3, mod3, mod3, gn, wo, l1g, l1b, wg, wu, wd,
      l2g, l2b)


def kernel(x_prompt, x_sample, state_hgrn_fwd, state_hgrn_bwd, c, c_ctx, w_mod, b_mod,
           w_in, lb_fwd_raw, lb_bwd_raw, hgrn_norm_g, pool_w, pool_scale, w_out, ln1_g,
           ln1_b, w_ffn_gate, w_ffn_up, w_ffn_down, ln2_g, ln2_b):
    n_ctx, t_ctx, _ = x_prompt.shape
    n_lat, t_lat, _ = x_sample.shape
    ctx_row = n_lat

    c_all = jnp.concatenate(
        [c, c_ctx[None, :], jnp.zeros((16 - n_lat - 1, D_MODEL), F32)], axis=0)
    mod = _mod_call(c_all, w_mod[0], b_mod)
    mod3 = mod.reshape(16, 1, 6 * D_MODEL)

    w_in_b = w_in[0].astype(BF16)
    wo_b = w_out[0].astype(BF16)
    wg_b = w_ffn_gate[0].astype(BF16)
    wu_b = w_ffn_up[0].astype(BF16)
    wd_b = w_ffn_down[0].astype(BF16)
    pw_b = pool_w[0].astype(BF16)

    def run(x, n_seq, seq_len, mod_row, two_d, s0f, s0b, emit_final):
        x2 = x.reshape(n_seq * seq_len, D_MODEL)
        p, e = _proj_call(x2, mod3, mod_row, w_in_b, lb_fwd_raw, lb_bwd_raw, n_seq, seq_len)
        scan = _scan_call(p, e, n_seq, seq_len, s0f, s0b, emit_final)
        yp = _pool_call(p, pw_b, pool_scale, n_seq, seq_len, two_d)
        y = _tail_call(x2, scan[0], scan[1], p, yp, mod3, mod_row, hgrn_norm_g, wo_b,
                       ln1_g, ln1_b, wg_b, wu_b, wd_b, ln2_g, ln2_b, seq_len)
        return y.reshape(n_seq, seq_len, D_MODEL), scan[2:]

    y_ctx, fin = run(x_prompt, n_ctx, t_ctx, lambda i, tps: ctx_row, False,
                     None, None, True)
    y_lat, _ = run(x_sample, n_lat, t_lat, lambda i, tps: i // tps, True,
                   state_hgrn_fwd[:, 0], state_hgrn_bwd[:, 0], False)
    return (y_ctx, y_lat, fin[0][:, None], fin[1][:, None])
```

```python
import functools

import numpy as np
import jax
import jax.numpy as jnp
from jax import lax
from jax.experimental import pallas as pl
from jax.experimental.pallas import tpu as pltpu

F32 = jnp.float32
BF16 = jnp.bfloat16

D_MODEL = 1024
D_REC = 512
D_POOL = 512
HEADS = 4
D_HEAD = 128
D_IN = 5 * D_REC + D_POOL
D_FF = 2816
GRID_W = 64
POOL_WINDOWS = (2, 4, 8, 16)
POOL_GW = 128
ALPHA = 2.0 ** 0.25
LN_EPS = 1e-5
RMS_EPS = 1e-6

CHUNK = 64
MID_F = CHUNK // 2 - 1
MID_B = CHUNK // 2
TM_PROJ = 1024
TB_SCAN = 1024
SEQ_PER_SCAN = 4
TM_TAIL = 512
ROWS = 64
FF_SPLITS = (0, 1536, D_FF)
P_COLS = 7 * D_REC
E_COLS = 4 * D_REC
VMEM_LIMIT = 56 * 1024 * 1024

_ARB = "arbitrary"


def _params(n_axes):
    return pltpu.CompilerParams(
        dimension_semantics=(_ARB,) * n_axes, vmem_limit_bytes=VMEM_LIMIT)


def _const_spec(shape):
    nd = len(shape)
    return pl.BlockSpec(shape, lambda *_: (0,) * nd, pipeline_mode=pl.Buffered(1))


def _mod_kernel(c_ref, w_ref, b_ref, o_ref):
    c = c_ref[...]
    hi, lo = _split_bf16(c * jax.nn.sigmoid(c))
    w = w_ref[...].astype(BF16)
    o_ref[...] = (jnp.dot(hi, w, preferred_element_type=F32)
                  + jnp.dot(lo, w, preferred_element_type=F32) + b_ref[...])


def _mod_call(c_all, w_mod, b_mod):
    rows = c_all.shape[0]
    n_out = w_mod.shape[1]
    blk = 1024
    return pl.pallas_call(
        _mod_kernel,
        grid=(n_out // blk,),
        in_specs=[
            pl.BlockSpec((rows, D_MODEL), lambda j: (0, 0)),
            pl.BlockSpec((D_MODEL, blk), lambda j: (0, j)),
            pl.BlockSpec((1, blk), lambda j: (0, j)),
        ],
        out_specs=pl.BlockSpec((rows, blk), lambda j: (0, j)),
        out_shape=jax.ShapeDtypeStruct((rows, n_out), F32),
        compiler_params=_params(1),
        name="mod",
    )(c_all, w_mod, b_mod)


def _lower_bound(raw_ref):
    r0 = raw_ref[0:1, :]
    r1 = raw_ref[1:2, :]
    mx = jnp.maximum(r0, r1)
    e0 = jnp.exp(r0 - mx)
    e1 = jnp.exp(r1 - mx)
    return e0 / (e0 + e1)


def _split_bf16(a):
    hi = a.astype(BF16)
    lo = (a - hi.astype(F32)).astype(BF16)
    return hi, lo


def _proj_kernel(x_ref, sh_ref, sc_ref, w_ref, lbf_ref, lbb_ref, p_ref, e_ref,
                 proj_scr, *, cps):
    tm = x_ref.shape[0]
    h = (x_ref[...] * (1.0 + sc_ref[0]) + sh_ref[0]).astype(BF16)
    proj_scr[...] = jnp.dot(h, w_ref[...], preferred_element_type=F32)

    lb_f = _lower_bound(lbf_ref)
    lb_b = _lower_bound(lbb_ref)
    ri = lax.broadcasted_iota(jnp.int32, (CHUNK, CHUNK), 0)
    ci = lax.broadcasted_iota(jnp.int32, (CHUNK, CHUNK), 1)
    tri_pre = jnp.where(ci <= ri, 1.0, 0.0).astype(BF16)
    tri_suf = jnp.where(ci >= ri, 1.0, 0.0).astype(BF16)

    def chunk_sum(tri, a):
        hi, lo = _split_bf16(a)
        return (jnp.dot(tri, hi, preferred_element_type=F32)
                + jnp.dot(tri, lo, preferred_element_type=F32))

    for c in range(tm // CHUNK):
        rows = slice(c * CHUNK, (c + 1) * CHUNK)
        q = proj_scr[rows, 0:D_REC]
        qs = q * jax.nn.sigmoid(q)

        def direction(z, lb, tri, mid, end):
            f = lb + (1.0 - lb) * jax.nn.sigmoid(z)
            k = 1.0 - f
            b = chunk_sum(tri, jnp.log(f))
            m = b[mid:mid + 1, :]
            d = b - m
            qd = qs * jnp.exp(d)
            ki = k * jnp.exp(-d)
            em = jnp.exp(m)
            r = jnp.exp(b[end:end + 1, :] - m)
            return qd, ki, em, r

        qd_f, ki_f, em_f, r_f = direction(
            proj_scr[rows, D_REC:2 * D_REC], lb_f, tri_pre, MID_F, CHUNK - 1)
        qd_b, ki_b, em_b, r_b = direction(
            proj_scr[rows, 2 * D_REC:3 * D_REC], lb_b, tri_suf, MID_B, 0)
        g = proj_scr[rows, 4 * D_REC:5 * D_REC]

        p_ref[rows, 0 * D_REC:1 * D_REC] = qd_f.astype(BF16)
        p_ref[rows, 1 * D_REC:2 * D_REC] = ki_f.astype(BF16)
        p_ref[rows, 2 * D_REC:3 * D_REC] = qd_b.astype(BF16)
        p_ref[rows, 3 * D_REC:4 * D_REC] = ki_b.astype(BF16)
        p_ref[rows, 4 * D_REC:5 * D_REC] = proj_scr[rows, 3 * D_REC:4 * D_REC].astype(BF16)
        p_ref[rows, 5 * D_REC:6 * D_REC] = (g * jax.nn.sigmoid(g)).astype(BF16)
        p_ref[rows, 6 * D_REC:7 * D_REC] = proj_scr[rows, 5 * D_REC:6 * D_REC].astype(BF16)

        s, cc = c // cps, c % cps
        e_ref[s, cc:cc + 1, 0 * D_REC:1 * D_REC] = em_f
        e_ref[s, cc:cc + 1, 1 * D_REC:2 * D_REC] = r_f
        e_ref[s, cc:cc + 1, 2 * D_REC:3 * D_REC] = em_b
        e_ref[s, cc:cc + 1, 3 * D_REC:4 * D_REC] = r_b


def _proj_call(x2, mod3, mod_row, w_in, lbf_raw, lbb_raw, n_seq, seq_len):
    n_tok = x2.shape[0]
    tm = TM_PROJ
    cpt = tm // CHUNK
    cps = min(cpt, seq_len // CHUNK)
    spt = cpt // cps
    tps = max(1, seq_len // tm)
    n_ck = seq_len // CHUNK

    if spt > 1:
        e_spec = pl.BlockSpec((spt, cps, E_COLS), lambda i: (i, 0, 0))
    else:
        e_spec = pl.BlockSpec((1, cps, E_COLS), lambda i: (i // tps, i % tps, 0))

    return pl.pallas_call(
        functools.partial(_proj_kernel, cps=cps),
        grid=(n_tok // tm,),
        in_specs=[
            pl.BlockSpec((tm, D_MODEL), lambda i: (i, 0)),
            pl.BlockSpec((1, 1, D_MODEL), lambda i: (mod_row(i, tps), 0, 0)),
            pl.BlockSpec((1, 1, D_MODEL), lambda i: (mod_row(i, tps), 0, 1)),
            _const_spec((D_MODEL, D_IN)),
            _const_spec((2, D_REC)),
            _const_spec((2, D_REC)),
        ],
        out_specs=[
            pl.BlockSpec((tm, P_COLS), lambda i: (i, 0)),
            e_spec,
        ],
        out_shape=[
            jax.ShapeDtypeStruct((n_tok, P_COLS), BF16),
            jax.ShapeDtypeStruct((n_seq, n_ck, E_COLS), F32),
        ],
        scratch_shapes=[pltpu.VMEM((tm, D_IN), F32)],
        compiler_params=_params(1),
        name="proj",
    )(x2, mod3, mod3, w_in, lbf_raw, lbb_raw)


def _scan_kernel(*refs, nck, spb, has_init, emit_final):
    it = iter(refs)
    qdf_ref, kif_ref, vf_ref, ef_ref = (next(it) for _ in range(4))
    qdb_ref, kib_ref, vb_ref, eb_ref = (next(it) for _ in range(4))
    if has_init:
        s0f_ref, s0b_ref = next(it), next(it)
    of_ref, ob_ref = next(it), next(it)
    if emit_final:
        finf_ref, finb_ref = next(it), next(it)
    sf_scr, sb_scr, uf_scr, ub_scr, scf_scr, scb_scr, tf_scr, tb_scr = (
        next(it) for _ in range(8))

    j = pl.program_id(1)
    seq_heads = [(q, h) for q in range(spb) for h in range(HEADS)]
    ri = lax.broadcasted_iota(jnp.int32, (CHUNK, CHUNK), 0)
    ci = lax.broadcasted_iota(jnp.int32, (CHUNK, CHUNK), 1)
    nt_dims = (((1,), (1,)), ((), ()))
    tn_dims = (((0,), (0,)), ((), ()))

    @pl.when(j == 0)
    def _():
        for q, h in seq_heads:
            if has_init:
                sf_scr[q, h] = s0f_ref[q, h].T
                sb_scr[q, h] = s0b_ref[q, h].T
            else:
                sf_scr[q, h] = jnp.zeros((D_HEAD, D_HEAD), F32)
                sb_scr[q, h] = jnp.zeros((D_HEAD, D_HEAD), F32)

    def tile(q, c, h):
        return (slice((q * nck + c) * CHUNK, (q * nck + c + 1) * CHUNK),
                slice(h * D_HEAD, (h + 1) * D_HEAD))

    def prepare(qd_ref, ki_ref, v_ref, keep, u_scr, sc_scr):
        for q in range(spb):
            for c in range(nck):
                for h in range(HEADS):
                    rows, cols = tile(q, c, h)
                    u_scr[q, c, h] = lax.dot_general(
                        v_ref[rows, cols], ki_ref[rows, cols], tn_dims,
                        preferred_element_type=F32)
                    sc = lax.dot_general(qd_ref[rows, cols], ki_ref[rows, cols], nt_dims,
                                         preferred_element_type=F32)
                    sc_scr[q, c, h] = jnp.where(keep, sc, 0.0).astype(BF16)

    def step(qd_ref, v_ref, e_ref, o_ref, s_scr, u_scr, sc_scr, smt_scr, c):
        for q, h in seq_heads:
            rows, cols = tile(q, c, h)
            em = e_ref[q, c:c + 1, h * D_HEAD:(h + 1) * D_HEAD]
            r = e_ref[q, c:c + 1, D_REC + h * D_HEAD:D_REC + (h + 1) * D_HEAD]
            sm = s_scr[q, h] * em
            smt_scr[q, h] = sm.astype(BF16).T
            o = jnp.dot(qd_ref[rows, cols], smt_scr[q, h], preferred_element_type=F32)
            o = o + jnp.dot(sc_scr[q, c, h], v_ref[rows, cols],
                            preferred_element_type=F32)
            o_ref[rows, cols] = o.astype(BF16)
            s_scr[q, h] = (sm + u_scr[q, c, h]) * r

    prepare(qdf_ref, kif_ref, vf_ref, ci <= ri, uf_scr, scf_scr)
    prepare(qdb_ref, kib_ref, vb_ref, ci >= ri, ub_scr, scb_scr)
    for c in range(nck):
        step(qdf_ref, vf_ref, ef_ref, of_ref, sf_scr, uf_scr, scf_scr, tf_scr, c)
        step(qdb_ref, vb_ref, eb_ref, ob_ref, sb_scr, ub_scr, scb_scr, tb_scr, nck - 1 - c)

    if emit_final:
        @pl.when(j == pl.num_programs(1) - 1)
        def _():
            for q, h in seq_heads:
                finf_ref[q, h] = sf_scr[q, h].T
                finb_ref[q, h] = sb_scr[q, h].T


def _scan_call(p, e, n_seq, seq_len, s0f=None, s0b=None, emit_final=False):
    tb = min(seq_len, TB_SCAN)
    nt = seq_len // tb
    spb = SEQ_PER_SCAN if nt == 1 else 1
    nck = tb // CHUNK
    has_init = s0f is not None
    st_blk = (spb, HEADS, D_HEAD, D_HEAD)
    rows = spb * tb
    n_tok = p.shape[0]

    def fwd(b, j):
        return b * nt + j

    def bwd(b, j):
        return b * nt + nt - 1 - j

    def p_spec(blk, col):
        return pl.BlockSpec((rows, D_REC), lambda b, j: (blk(b, j), col))

    in_specs = [
        p_spec(fwd, 0), p_spec(fwd, 1), p_spec(fwd, 4),
        pl.BlockSpec((spb, nck, 2 * D_REC), lambda b, j: (b, j, 0)),
        p_spec(bwd, 2), p_spec(bwd, 3), p_spec(bwd, 4),
        pl.BlockSpec((spb, nck, 2 * D_REC), lambda b, j: (b, nt - 1 - j, 1)),
    ]
    args = [p, p, p, e, p, p, p, e]
    if has_init:
        in_specs += [pl.BlockSpec(st_blk, lambda b, j: (b, 0, 0, 0))] * 2
        args += [s0f, s0b]
    o_shape = jax.ShapeDtypeStruct((n_tok, D_REC), BF16)
    out_specs = [
        pl.BlockSpec((rows, D_REC), lambda b, j: (fwd(b, j), 0)),
        pl.BlockSpec((rows, D_REC), lambda b, j: (bwd(b, j), 0)),
    ]
    out_shape = [o_shape, o_shape]
    if emit_final:
        out_specs += [pl.BlockSpec(st_blk, lambda b, j: (b, 0, 0, 0))] * 2
        out_shape += [jax.ShapeDtypeStruct((n_seq, HEADS, D_HEAD, D_HEAD), F32)] * 2

    return pl.pallas_call(
        functools.partial(_scan_kernel, nck=nck, spb=spb, has_init=has_init,
                          emit_final=emit_final),
        grid=(n_seq // spb, nt),
        in_specs=in_specs,
        out_specs=out_specs,
        out_shape=out_shape,
        scratch_shapes=([pltpu.VMEM((spb, HEADS, D_HEAD, D_HEAD), F32)] * 2
                        + [pltpu.VMEM((spb, nck, HEADS, D_HEAD, D_HEAD), F32)] * 2
                        + [pltpu.VMEM((spb, nck, HEADS, CHUNK, CHUNK), BF16)] * 2
                        + [pltpu.VMEM((spb, HEADS, D_HEAD, D_HEAD), BF16)] * 2),
        compiler_params=_params(2),
        name="scan",
    )(*args)


def _window(n, w):
    idx = np.arange(n)
    return np.clip(idx - w // 2, 0, n), np.clip(idx + w // 2, 0, n)


def _band(n, w):
    lo, hi = _window(n, w)
    j = np.arange(n)[None, :]
    return ((j >= lo[:, None]) & (j < hi[:, None])).astype(np.float32), (hi - lo)


def _pool_seq_kernel(u_ref, a_ref, inv_ref, pw_ref, ps_ref, o_ref, diff_scr, *,
                     n_seq_blk, seq_len):
    for gi in range(len(POOL_WINDOWS)):
        cols = slice(gi * POOL_GW, (gi + 1) * POOL_GW)
        for s in range(0, n_seq_blk, 2):
            r0 = slice(s * seq_len, (s + 1) * seq_len)
            r1 = slice((s + 1) * seq_len, (s + 2) * seq_len)
            u0, u1 = u_ref[r0, cols], u_ref[r1, cols]
            tot = jnp.dot(a_ref[gi], jnp.concatenate([u0, u1], axis=1),
                          preferred_element_type=F32)
            diff_scr[r0, cols] = (tot[:, 0:POOL_GW] * inv_ref[gi]
                                  - u0.astype(F32)).astype(BF16)
            diff_scr[r1, cols] = (tot[:, POOL_GW:2 * POOL_GW] * inv_ref[gi]
                                  - u1.astype(F32)).astype(BF16)
    _mix_groups(diff_scr, pw_ref, ps_ref, o_ref)


def _mix_groups(diff_scr, pw_ref, ps_ref, o_ref):
    for k in range(pw_ref.shape[0]):
        cols = slice(2 * k * POOL_GW, (2 * k + 2) * POOL_GW)
        y = jnp.dot(diff_scr[:, cols], pw_ref[k], preferred_element_type=F32)
        o_ref[:, cols] = (y * ps_ref[:, cols]).astype(BF16)


def _pool_grid_kernel(u_ref, a_ref, invc_ref, pw_ref, ps_ref, o_ref, col_scr, diff_scr,
                      *, n_rows):
    blk = a_ref.shape[1]
    rpb = blk // GRID_W
    for gi, w in enumerate(POOL_WINDOWS):
        cols = slice(gi * POOL_GW, (gi + 1) * POOL_GW)
        for rb in range(0, n_rows // rpb, 2):
            r0 = slice(rb * blk, (rb + 1) * blk)
            r1 = slice((rb + 1) * blk, (rb + 2) * blk)
            pair = jnp.concatenate([u_ref[r0, cols], u_ref[r1, cols]], axis=1)
            tot = jnp.dot(a_ref[gi], pair, preferred_element_type=F32)
            col_scr[gi, r0, :] = tot[:, 0:POOL_GW] * invc_ref[gi]
            col_scr[gi, r1, :] = tot[:, POOL_GW:2 * POOL_GW] * invc_ref[gi]
    for gi, w in enumerate(POOL_WINDOWS):
        cols = slice(gi * POOL_GW, (gi + 1) * POOL_GW)
        lo, hi = _window(n_rows, w)

        def grid_row(rr):
            return col_scr[gi, rr * GRID_W:(rr + 1) * GRID_W, :]

        acc = None
        for r in range(n_rows):
            if r == 0:
                acc = grid_row(int(lo[0]))
                for rr in range(int(lo[0]) + 1, int(hi[0])):
                    acc = acc + grid_row(rr)
            else:
                for rr in range(int(hi[r - 1]), int(hi[r])):
                    acc = acc + grid_row(rr)
                for rr in range(int(lo[r - 1]), int(lo[r])):
                    acc = acc - grid_row(rr)
            rows = slice(r * GRID_W, (r + 1) * GRID_W)
            mean = acc * (1.0 / float(hi[r] - lo[r]))
            diff_scr[rows, cols] = (mean - u_ref[rows, cols].astype(F32)).astype(BF16)
    _mix_groups(diff_scr, pw_ref, ps_ref, o_ref)


def _pool_call(p, pool_w, pool_scale, n_seq, seq_len, two_d):
    n_tok = p.shape[0]
    ng = len(POOL_WINDOWS)
    zero = jnp.zeros((POOL_GW, POOL_GW), pool_w.dtype)
    pw_pairs = jnp.stack([jnp.block([[pool_w[2 * k], zero], [zero, pool_w[2 * k + 1]]])
                          for k in range(ng // 2)])
    if two_d:
        n_rows = seq_len // GRID_W
        rpb = 4
        blk = rpb * GRID_W
        bands = [_band(GRID_W, w) for w in POOL_WINDOWS]
        a = np.stack([np.kron(np.eye(rpb, dtype=np.float32), b[0]) for b in bands])
        inv = np.stack([np.broadcast_to(np.tile(1.0 / b[1], rpb)[:, None], (blk, POOL_GW))
                        for b in bands]).astype(np.float32)
        kern = functools.partial(_pool_grid_kernel, n_rows=n_rows)
        tb = seq_len
        scratch = [pltpu.VMEM((ng, seq_len, POOL_GW), F32),
                   pltpu.VMEM((seq_len, D_POOL), BF16)]
    else:
        blk = seq_len
        bands = [_band(seq_len, w) for w in POOL_WINDOWS]
        a = np.stack([b[0] for b in bands])
        inv = np.stack([np.broadcast_to((1.0 / b[1])[:, None], (blk, POOL_GW))
                        for b in bands]).astype(np.float32)
        n_seq_blk = 8
        kern = functools.partial(_pool_seq_kernel, n_seq_blk=n_seq_blk, seq_len=seq_len)
        tb = n_seq_blk * seq_len
        scratch = [pltpu.VMEM((tb, D_POOL), BF16)]
    return pl.pallas_call(
        kern,
        grid=(n_tok // tb,),
        in_specs=[
            pl.BlockSpec((tb, D_POOL), lambda i: (i, 6)),
            _const_spec((ng, blk, blk)),
            _const_spec((ng, blk, POOL_GW)),
            _const_spec((ng // 2, 2 * POOL_GW, 2 * POOL_GW)),
            _const_spec((1, D_POOL)),
        ],
        out_specs=pl.BlockSpec((tb, D_POOL), lambda i: (i, 0)),
        out_shape=jax.ShapeDtypeStruct((n_tok, D_POOL), BF16),
        scratch_shapes=scratch,
        compiler_params=_params(1),
        name="pool",
    )(p, jnp.asarray(a, BF16), jnp.asarray(inv, F32), pw_pairs, pool_scale)


def _layer_norm(u, g, b):
    mu = jnp.mean(u, axis=-1, keepdims=True)
    d = u - mu
    var = jnp.mean(d * d, axis=-1, keepdims=True)
    return d * lax.rsqrt(var + LN_EPS) * g + b


def _tail_kernel(x_ref, of_ref, ob_ref, gs_ref, yp_ref, g1_ref, sh2_ref, sc2_ref,
                 g2_ref, gn_ref, wo_ref, l1g_ref, l1b_ref, wg_ref, wu_ref, wd_ref,
                 l2g_ref, l2b_ref, o_ref, yr_scr, u_scr, h2_scr):
    tm = x_ref.shape[0]
    gn = gn_ref[...]

    th = tm // 2
    half = [slice(0, th), slice(th, tm)]
    n_rb = th // ROWS
    ff_slices = [slice(lo, hi) for lo, hi in zip(FF_SPLITS[:-1], FF_SPLITS[1:])]
    assert n_rb == 2 * len(ff_slices)

    def gated_norm_block(i, m):
        rows = slice(i * th + m * ROWS, i * th + (m + 1) * ROWS)
        for h in range(HEADS):
            cols = slice(h * D_HEAD, (h + 1) * D_HEAD)
            o = of_ref[rows, cols].astype(F32) + ob_ref[rows, cols].astype(F32)
            ms = jnp.mean(o * o, axis=-1, keepdims=True)
            on = o * lax.rsqrt(ms + RMS_EPS) * gn
            yr_scr[rows, cols] = (on * gs_ref[rows, cols].astype(F32)).astype(BF16)

    def out_proj(hr, fill):
        y = jnp.dot(yr_scr[hr, :], wo_ref[0:D_REC, :], preferred_element_type=F32)
        fill(0)
        y = y + jnp.dot(yp_ref[hr, :], wo_ref[D_REC:2 * D_REC, :],
                        preferred_element_type=F32)
        fill(1)
        return ALPHA * x_ref[hr, :] + g1_ref[0] * y

    def norm1_block(i, m):
        rows = slice(i * th + m * ROWS, i * th + (m + 1) * ROWS)
        x1 = _layer_norm(u_scr[rows, :], l1g_ref[...], l1b_ref[...])
        u_scr[rows, :] = x1
        h2_scr[rows, :] = (x1 * (1.0 + sc2_ref[0]) + sh2_ref[0]).astype(BF16)

    def norm2_block(i, m):
        rows = slice(i * th + m * ROWS, i * th + (m + 1) * ROWS)
        o_ref[rows, :] = _layer_norm(u_scr[rows, :], l2g_ref[...], l2b_ref[...])

    def ffn_half(i, fill):
        hr = half[i]
        h2 = h2_scr[hr, :]
        ffn = None
        j = 0
        for cs in ff_slices:
            a = jnp.dot(h2, wg_ref[:, cs], preferred_element_type=F32)
            fill(j); j += 1
            b = jnp.dot(h2, wu_ref[:, cs], preferred_element_type=F32)
            fill(j); j += 1
            act = (a * jax.nn.sigmoid(a) * b).astype(BF16)
            part = jnp.dot(act, wd_ref[cs, :], preferred_element_type=F32)
            ffn = part if ffn is None else ffn + part
        u_scr[hr, :] = ALPHA * u_scr[hr, :] + g2_ref[0] * ffn

    def in_two(block, i):
        def emit(j):
            for m in range(j * n_rb // 2, (j + 1) * n_rb // 2):
                block(i, m)
        return emit

    for m in range(n_rb):
        gated_norm_block(0, m)
    u_scr[half[0], :] = out_proj(half[0], in_two(gated_norm_block, 1))
    u_scr[half[1], :] = out_proj(half[1], in_two(norm1_block, 0))
    ffn_half(0, lambda j: norm1_block(1, j))
    ffn_half(1, lambda j: norm2_block(0, j))
    for m in range(n_rb):
        norm2_block(1, m)


def _tail_call(x2, o_f, o_b, p, yp, mod3, mod_row, gn, wo, l1g, l1b, wg, wu, wd,
               l2g, l2b, seq_len):
    n_tok = x2.shape[0]
    tm = TM_TAIL
    tps = max(1, seq_len // tm)

    def mod_spec(col):
        return pl.BlockSpec((1, 1, D_MODEL), lambda i: (mod_row(i, tps), 0, col))

    rec_spec = pl.BlockSpec((tm, D_REC), lambda i: (i, 0))
    return pl.pallas_call(
        _tail_kernel,
        grid=(n_tok // tm,),
        in_specs=[
            pl.BlockSpec((tm, D_MODEL), lambda i: (i, 0)),
            rec_spec, rec_spec,
            pl.BlockSpec((tm, D_REC), lambda i: (i, 5)),
            pl.BlockSpec((tm, D_POOL), lambda i: (i, 0)),
            mod_spec(2), mod_spec(3), mod_spec(4), mod_spec(5),
            _const_spec((1, D_HEAD)),
            _const_spec((D_MODEL, D_MODEL)),
            _const_spec((1, D_MODEL)), _const_spec((1, D_MODEL)),
            _const_spec((D_MODEL, D_FF)), _const_spec((D_MODEL, D_FF)),
            _const_spec((D_FF, D_MODEL)),
            _const_spec((1, D_MODEL)), _const_spec((1, D_MODEL)),
        ],
        out_specs=pl.BlockSpec((tm, D_MODEL), lambda i: (i, 0)),
        out_shape=jax.ShapeDtypeStruct((n_tok, D_MODEL), F32),
        scratch_shapes=[
            pltpu.VMEM((tm, D_REC), BF16),
            pltpu.VMEM((tm, D_MODEL), F32),
            pltpu.VMEM((tm, D_MODEL), BF16),
        ],
        compiler_params=_params(1),
        name="tail",
    )(x2, o_f, o_b, p, yp, mod3, mod3, mod3, mod3, gn, wo, l1g, l1b, wg, wu, wd,
      l2g, l2b)


def kernel(x_prompt, x_sample, state_hgrn_fwd, state_hgrn_bwd, c, c_ctx, w_mod, b_mod,
           w_in, lb_fwd_raw, lb_bwd_raw, hgrn_norm_g, pool_w, pool_scale, w_out, ln1_g,
           ln1_b, w_ffn_gate, w_ffn_up, w_ffn_down, ln2_g, ln2_b):
    n_ctx, t_ctx, _ = x_prompt.shape
    n_lat, t_lat, _ = x_sample.shape
    ctx_row = n_lat

    c_all = jnp.concatenate(
        [c, c_ctx[None, :], jnp.zeros((16 - n_lat - 1, D_MODEL), F32)], axis=0)
    mod = _mod_call(c_all, w_mod[0], b_mod)
    mod3 = mod.reshape(16, 1, 6 * D_MODEL)

    w_in_b = w_in[0].astype(BF16)
    wo_b = w_out[0].astype(BF16)
    wg_b = w_ffn_gate[0].astype(BF16)
    wu_b = w_ffn_up[0].astype(BF16)
    wd_b = w_ffn_down[0].astype(BF16)
    pw_b = pool_w[0].astype(BF16)

    def run(x, n_seq, seq_len, mod_row, two_d, s0f, s0b, emit_final):
        x2 = x.reshape(n_seq * seq_len, D_MODEL)
        p, e = _proj_call(x2, mod3, mod_row, w_in_b, lb_fwd_raw, lb_bwd_raw, n_seq, seq_len)
        scan = _scan_call(p, e, n_seq, seq_len, s0f, s0b, emit_final)
        yp = _pool_call(p, pw_b, pool_scale, n_seq, seq_len, two_d)
        y = _tail_call(x2, scan[0], scan[1], p, yp, mod3, mod_row, hgrn_norm_g, wo_b,
                       ln1_g, ln1_b, wg_b, wu_b, wd_b, ln2_g, ln2_b, seq_len)
        return y.reshape(n_seq, seq_len, D_MODEL), scan[2:]

    y_ctx, fin = run(x_prompt, n_ctx, t_ctx, lambda i, tps: ctx_row, False,
                     None, None, True)
    y_lat, _ = run(x_sample, n_lat, t_lat, lambda i, tps: i // tps, True,
                   state_hgrn_fwd[:, 0], state_hgrn_bwd[:, 0], False)
    return (y_ctx, y_lat, fin[0][:, None], fin[1][:, None])
```

```python
import functools

import numpy as np
import jax
import jax.numpy as jnp
from jax import lax
from jax.experimental import pallas as pl
from jax.experimental.pallas import tpu as pltpu

F32 = jnp.float32
BF16 = jnp.bfloat16

D_MODEL = 1024
D_REC = 512
D_POOL = 512
HEADS = 4
D_HEAD = 128
D_IN = 5 * D_REC + D_POOL
D_FF = 2816
GRID_W = 64
POOL_WINDOWS = (2, 4, 8, 16)
POOL_GW = 128
ALPHA = 2.0 ** 0.25
LN_EPS = 1e-5
RMS_EPS = 1e-6

CHUNK = 64
MID_F = CHUNK // 2 - 1
MID_B = CHUNK // 2
TM_PROJ = 1024
TB_SCAN = 1024
SEQ_PER_SCAN = 4
TM_TAIL = 1024
PART = 256
ROWS = 64
FF_SPLITS = (0, 1536, D_FF)
P_COLS = 7 * D_REC
E_COLS = 4 * D_REC
VMEM_LIMIT = 56 * 1024 * 1024

_ARB = "arbitrary"


def _params(n_axes):
    return pltpu.CompilerParams(
        dimension_semantics=(_ARB,) * n_axes, vmem_limit_bytes=VMEM_LIMIT)


def _const_spec(shape):
    nd = len(shape)
    return pl.BlockSpec(shape, lambda *_: (0,) * nd, pipeline_mode=pl.Buffered(1))


def _mod_kernel(c_ref, w_ref, b_ref, o_ref):
    c = c_ref[...]
    hi, lo = _split_bf16(c * jax.nn.sigmoid(c))
    w = w_ref[...].astype(BF16)
    o_ref[...] = (jnp.dot(hi, w, preferred_element_type=F32)
                  + jnp.dot(lo, w, preferred_element_type=F32) + b_ref[...])


def _mod_call(c_all, w_mod, b_mod):
    rows = c_all.shape[0]
    n_out = w_mod.shape[1]
    blk = 1024
    return pl.pallas_call(
        _mod_kernel,
        grid=(n_out // blk,),
        in_specs=[
            pl.BlockSpec((rows, D_MODEL), lambda j: (0, 0)),
            pl.BlockSpec((D_MODEL, blk), lambda j: (0, j)),
            pl.BlockSpec((1, blk), lambda j: (0, j)),
        ],
        out_specs=pl.BlockSpec((rows, blk), lambda j: (0, j)),
        out_shape=jax.ShapeDtypeStruct((rows, n_out), F32),
        compiler_params=_params(1),
        name="mod",
    )(c_all, w_mod, b_mod)


def _lower_bound(raw_ref):
    r0 = raw_ref[0:1, :]
    r1 = raw_ref[1:2, :]
    mx = jnp.maximum(r0, r1)
    e0 = jnp.exp(r0 - mx)
    e1 = jnp.exp(r1 - mx)
    return e0 / (e0 + e1)


def _split_bf16(a):
    hi = a.astype(BF16)
    lo = (a - hi.astype(F32)).astype(BF16)
    return hi, lo


def _proj_kernel(x_ref, sh_ref, sc_ref, w_ref, lbf_ref, lbb_ref, p_ref, e_ref,
                 proj_scr, *, cps):
    tm = x_ref.shape[0]
    h = (x_ref[...] * (1.0 + sc_ref[0]) + sh_ref[0]).astype(BF16)
    proj_scr[...] = jnp.dot(h, w_ref[...], preferred_element_type=F32)

    lb_f = _lower_bound(lbf_ref)
    lb_b = _lower_bound(lbb_ref)
    ri = lax.broadcasted_iota(jnp.int32, (CHUNK, CHUNK), 0)
    ci = lax.broadcasted_iota(jnp.int32, (CHUNK, CHUNK), 1)
    tri_pre = jnp.where(ci <= ri, 1.0, 0.0).astype(BF16)
    tri_suf = jnp.where(ci >= ri, 1.0, 0.0).astype(BF16)

    def chunk_sum(tri, a):
        hi, lo = _split_bf16(a)
        return (jnp.dot(tri, hi, preferred_element_type=F32)
                + jnp.dot(tri, lo, preferred_element_type=F32))

    for c in range(tm // CHUNK):
        rows = slice(c * CHUNK, (c + 1) * CHUNK)
        q = proj_scr[rows, 0:D_REC]
        qs = q * jax.nn.sigmoid(q)

        def direction(z, lb, tri, mid, end):
            f = lb + (1.0 - lb) * jax.nn.sigmoid(z)
            k = 1.0 - f
            b = chunk_sum(tri, jnp.log(f))
            m = b[mid:mid + 1, :]
            d = b - m
            qd = qs * jnp.exp(d)
            ki = k * jnp.exp(-d)
            em = jnp.exp(m)
            r = jnp.exp(b[end:end + 1, :] - m)
            return qd, ki, em, r

        qd_f, ki_f, em_f, r_f = direction(
            proj_scr[rows, D_REC:2 * D_REC], lb_f, tri_pre, MID_F, CHUNK - 1)
        qd_b, ki_b, em_b, r_b = direction(
            proj_scr[rows, 2 * D_REC:3 * D_REC], lb_b, tri_suf, MID_B, 0)
        g = proj_scr[rows, 4 * D_REC:5 * D_REC]

        p_ref[rows, 0 * D_REC:1 * D_REC] = qd_f.astype(BF16)
        p_ref[rows, 1 * D_REC:2 * D_REC] = ki_f.astype(BF16)
        p_ref[rows, 2 * D_REC:3 * D_REC] = qd_b.astype(BF16)
        p_ref[rows, 3 * D_REC:4 * D_REC] = ki_b.astype(BF16)
        p_ref[rows, 4 * D_REC:5 * D_REC] = proj_scr[rows, 3 * D_REC:4 * D_REC].astype(BF16)
        p_ref[rows, 5 * D_REC:6 * D_REC] = (g * jax.nn.sigmoid(g)).astype(BF16)
        p_ref[rows, 6 * D_REC:7 * D_REC] = proj_scr[rows, 5 * D_REC:6 * D_REC].astype(BF16)

        s, cc = c // cps, c % cps
        e_ref[s, cc:cc + 1, 0 * D_REC:1 * D_REC] = em_f
        e_ref[s, cc:cc + 1, 1 * D_REC:2 * D_REC] = r_f
        e_ref[s, cc:cc + 1, 2 * D_REC:3 * D_REC] = em_b
        e_ref[s, cc:cc + 1, 3 * D_REC:4 * D_REC] = r_b


def _proj_call(x2, mod3, mod_row, w_in, lbf_raw, lbb_raw, n_seq, seq_len):
    n_tok = x2.shape[0]
    tm = TM_PROJ
    cpt = tm // CHUNK
    cps = min(cpt, seq_len // CHUNK)
    spt = cpt // cps
    tps = max(1, seq_len // tm)
    n_ck = seq_len // CHUNK

    if spt > 1:
        e_spec = pl.BlockSpec((spt, cps, E_COLS), lambda i: (i, 0, 0))
    else:
        e_spec = pl.BlockSpec((1, cps, E_COLS), lambda i: (i // tps, i % tps, 0))

    return pl.pallas_call(
        functools.partial(_proj_kernel, cps=cps),
        grid=(n_tok // tm,),
        in_specs=[
            pl.BlockSpec((tm, D_MODEL), lambda i: (i, 0)),
            pl.BlockSpec((1, 1, D_MODEL), lambda i: (mod_row(i, tps), 0, 0)),
            pl.BlockSpec((1, 1, D_MODEL), lambda i: (mod_row(i, tps), 0, 1)),
            _const_spec((D_MODEL, D_IN)),
            _const_spec((2, D_REC)),
            _const_spec((2, D_REC)),
        ],
        out_specs=[
            pl.BlockSpec((tm, P_COLS), lambda i: (i, 0)),
            e_spec,
        ],
        out_shape=[
            jax.ShapeDtypeStruct((n_tok, P_COLS), BF16),
            jax.ShapeDtypeStruct((n_seq, n_ck, E_COLS), F32),
        ],
        scratch_shapes=[pltpu.VMEM((tm, D_IN), F32)],
        compiler_params=_params(1),
        name="proj",
    )(x2, mod3, mod3, w_in, lbf_raw, lbb_raw)


def _scan_kernel(*refs, nck, spb, has_init, emit_final):
    it = iter(refs)
    qdf_ref, kif_ref, vf_ref, ef_ref = (next(it) for _ in range(4))
    qdb_ref, kib_ref, vb_ref, eb_ref = (next(it) for _ in range(4))
    if has_init:
        s0f_ref, s0b_ref = next(it), next(it)
    of_ref, ob_ref = next(it), next(it)
    if emit_final:
        finf_ref, finb_ref = next(it), next(it)
    sf_scr, sb_scr, uf_scr, ub_scr, scf_scr, scb_scr, tf_scr, tb_scr = (
        next(it) for _ in range(8))

    j = pl.program_id(1)
    seq_heads = [(q, h) for q in range(spb) for h in range(HEADS)]
    ri = lax.broadcasted_iota(jnp.int32, (CHUNK, CHUNK), 0)
    ci = lax.broadcasted_iota(jnp.int32, (CHUNK, CHUNK), 1)
    nt_dims = (((1,), (1,)), ((), ()))
    tn_dims = (((0,), (0,)), ((), ()))

    @pl.when(j == 0)
    def _():
        for q, h in seq_heads:
            if has_init:
                sf_scr[q, h] = s0f_ref[q, h].T
                sb_scr[q, h] = s0b_ref[q, h].T
            else:
                sf_scr[q, h] = jnp.zeros((D_HEAD, D_HEAD), F32)
                sb_scr[q, h] = jnp.zeros((D_HEAD, D_HEAD), F32)

    def tile(q, c, h):
        return (slice((q * nck + c) * CHUNK, (q * nck + c + 1) * CHUNK),
                slice(h * D_HEAD, (h + 1) * D_HEAD))

    def prepare(qd_ref, ki_ref, v_ref, keep, u_scr, sc_scr):
        for q in range(spb):
            for c in range(nck):
                for h in range(HEADS):
                    rows, cols = tile(q, c, h)
                    u_scr[q, c, h] = lax.dot_general(
                        v_ref[rows, cols], ki_ref[rows, cols], tn_dims,
                        preferred_element_type=F32)
                    sc = lax.dot_general(qd_ref[rows, cols], ki_ref[rows, cols], nt_dims,
                                         preferred_element_type=F32)
                    sc_scr[q, c, h] = jnp.where(keep, sc, 0.0).astype(BF16)

    def step(qd_ref, v_ref, e_ref, o_ref, s_scr, u_scr, sc_scr, smt_scr, c):
        for q, h in seq_heads:
            rows, cols = tile(q, c, h)
            em = e_ref[q, c:c + 1, h * D_HEAD:(h + 1) * D_HEAD]
            r = e_ref[q, c:c + 1, D_REC + h * D_HEAD:D_REC + (h + 1) * D_HEAD]
            sm = s_scr[q, h] * em
            smt_scr[q, h] = sm.astype(BF16).T
            o = jnp.dot(qd_ref[rows, cols], smt_scr[q, h], preferred_element_type=F32)
            o = o + jnp.dot(sc_scr[q, c, h], v_ref[rows, cols],
                            preferred_element_type=F32)
            o_ref[rows, cols] = o.astype(BF16)
            s_scr[q, h] = (sm + u_scr[q, c, h]) * r

    prepare(qdf_ref, kif_ref, vf_ref, ci <= ri, uf_scr, scf_scr)
    prepare(qdb_ref, kib_ref, vb_ref, ci >= ri, ub_scr, scb_scr)
    for c in range(nck):
        step(qdf_ref, vf_ref, ef_ref, of_ref, sf_scr, uf_scr, scf_scr, tf_scr, c)
        step(qdb_ref, vb_ref, eb_ref, ob_ref, sb_scr, ub_scr, scb_scr, tb_scr, nck - 1 - c)

    if emit_final:
        @pl.when(j == pl.num_programs(1) - 1)
        def _():
            for q, h in seq_heads:
                finf_ref[q, h] = sf_scr[q, h].T
                finb_ref[q, h] = sb_scr[q, h].T


def _scan_call(p, e, n_seq, seq_len, s0f=None, s0b=None, emit_final=False):
    tb = min(seq_len, TB_SCAN)
    nt = seq_len // tb
    spb = SEQ_PER_SCAN if nt == 1 else 1
    nck = tb // CHUNK
    has_init = s0f is not None
    st_blk = (spb, HEADS, D_HEAD, D_HEAD)
    rows = spb * tb
    n_tok = p.shape[0]

    def fwd(b, j):
        return b * nt + j

    def bwd(b, j):
        return b * nt + nt - 1 - j

    def p_spec(blk, col):
        return pl.BlockSpec((rows, D_REC), lambda b, j: (blk(b, j), col))

    in_specs = [
        p_spec(fwd, 0), p_spec(fwd, 1), p_spec(fwd, 4),
        pl.BlockSpec((spb, nck, 2 * D_REC), lambda b, j: (b, j, 0)),
        p_spec(bwd, 2), p_spec(bwd, 3), p_spec(bwd, 4),
        pl.BlockSpec((spb, nck, 2 * D_REC), lambda b, j: (b, nt - 1 - j, 1)),
    ]
    args = [p, p, p, e, p, p, p, e]
    if has_init:
        in_specs += [pl.BlockSpec(st_blk, lambda b, j: (b, 0, 0, 0))] * 2
        args += [s0f, s0b]
    o_shape = jax.ShapeDtypeStruct((n_tok, D_REC), BF16)
    out_specs = [
        pl.BlockSpec((rows, D_REC), lambda b, j: (fwd(b, j), 0)),
        pl.BlockSpec((rows, D_REC), lambda b, j: (bwd(b, j), 0)),
    ]
    out_shape = [o_shape, o_shape]
    if emit_final:
        out_specs += [pl.BlockSpec(st_blk, lambda b, j: (b, 0, 0, 0))] * 2
        out_shape += [jax.ShapeDtypeStruct((n_seq, HEADS, D_HEAD, D_HEAD), F32)] * 2

    return pl.pallas_call(
        functools.partial(_scan_kernel, nck=nck, spb=spb, has_init=has_init,
                          emit_final=emit_final),
        grid=(n_seq // spb, nt),
        in_specs=in_specs,
        out_specs=out_specs,
        out_shape=out_shape,
        scratch_shapes=([pltpu.VMEM((spb, HEADS, D_HEAD, D_HEAD), F32)] * 2
                        + [pltpu.VMEM((spb, nck, HEADS, D_HEAD, D_HEAD), F32)] * 2
                        + [pltpu.VMEM((spb, nck, HEADS, CHUNK, CHUNK), BF16)] * 2
                        + [pltpu.VMEM((spb, HEADS, D_HEAD, D_HEAD), BF16)] * 2),
        compiler_params=_params(2),
        name="scan",
    )(*args)


def _window(n, w):
    idx = np.arange(n)
    return np.clip(idx - w // 2, 0, n), np.clip(idx + w // 2, 0, n)


def _band(n, w):
    lo, hi = _window(n, w)
    j = np.arange(n)[None, :]
    return ((j >= lo[:, None]) & (j < hi[:, None])).astype(np.float32), (hi - lo)


def _pool_seq_kernel(u_ref, a_ref, inv_ref, pw_ref, ps_ref, o_ref, diff_scr, *,
                     n_seq_blk, seq_len):
    for gi in range(len(POOL_WINDOWS)):
        cols = slice(gi * POOL_GW, (gi + 1) * POOL_GW)
        for s in range(0, n_seq_blk, 2):
            r0 = slice(s * seq_len, (s + 1) * seq_len)
            r1 = slice((s + 1) * seq_len, (s + 2) * seq_len)
            u0, u1 = u_ref[r0, cols], u_ref[r1, cols]
            tot = jnp.dot(a_ref[gi], jnp.concatenate([u0, u1], axis=1),
                          preferred_element_type=F32)
            diff_scr[r0, cols] = (tot[:, 0:POOL_GW] * inv_ref[gi]
                                  - u0.astype(F32)).astype(BF16)
            diff_scr[r1, cols] = (tot[:, POOL_GW:2 * POOL_GW] * inv_ref[gi]
                                  - u1.astype(F32)).astype(BF16)
    _mix_groups(diff_scr, pw_ref, ps_ref, o_ref)


def _mix_groups(diff_scr, pw_ref, ps_ref, o_ref):
    for k in range(pw_ref.shape[0]):
        cols = slice(2 * k * POOL_GW, (2 * k + 2) * POOL_GW)
        y = jnp.dot(diff_scr[:, cols], pw_ref[k], preferred_element_type=F32)
        o_ref[:, cols] = (y * ps_ref[:, cols]).astype(BF16)


def _pool_grid_kernel(u_ref, a_ref, invc_ref, pw_ref, ps_ref, o_ref, col_scr, diff_scr,
                      *, n_rows):
    blk = a_ref.shape[1]
    rpb = blk // GRID_W
    for gi, w in enumerate(POOL_WINDOWS):
        cols = slice(gi * POOL_GW, (gi + 1) * POOL_GW)
        for rb in range(0, n_rows // rpb, 2):
            r0 = slice(rb * blk, (rb + 1) * blk)
            r1 = slice((rb + 1) * blk, (rb + 2) * blk)
            pair = jnp.concatenate([u_ref[r0, cols], u_ref[r1, cols]], axis=1)
            tot = jnp.dot(a_ref[gi], pair, preferred_element_type=F32)
            col_scr[gi, r0, :] = tot[:, 0:POOL_GW] * invc_ref[gi]
            col_scr[gi, r1, :] = tot[:, POOL_GW:2 * POOL_GW] * invc_ref[gi]
    for gi, w in enumerate(POOL_WINDOWS):
        cols = slice(gi * POOL_GW, (gi + 1) * POOL_GW)
        lo, hi = _window(n_rows, w)

        def grid_row(rr):
            return col_scr[gi, rr * GRID_W:(rr + 1) * GRID_W, :]

        acc = None
        for r in range(n_rows):
            if r == 0:
                acc = grid_row(int(lo[0]))
                for rr in range(int(lo[0]) + 1, int(hi[0])):
                    acc = acc + grid_row(rr)
            else:
                for rr in range(int(hi[r - 1]), int(hi[r])):
                    acc = acc + grid_row(rr)
                for rr in range(int(lo[r - 1]), int(lo[r])):
                    acc = acc - grid_row(rr)
            rows = slice(r * GRID_W, (r + 1) * GRID_W)
            mean = acc * (1.0 / float(hi[r] - lo[r]))
            diff_scr[rows, cols] = (mean - u_ref[rows, cols].astype(F32)).astype(BF16)
    _mix_groups(diff_scr, pw_ref, ps_ref, o_ref)


def _pool_call(p, pool_w, pool_scale, n_seq, seq_len, two_d):
    n_tok = p.shape[0]
    ng = len(POOL_WINDOWS)
    zero = jnp.zeros((POOL_GW, POOL_GW), pool_w.dtype)
    pw_pairs = jnp.stack([jnp.block([[pool_w[2 * k], zero], [zero, pool_w[2 * k + 1]]])
                          for k in range(ng // 2)])
    if two_d:
        n_rows = seq_len // GRID_W
        rpb = 4
        blk = rpb * GRID_W
        bands = [_band(GRID_W, w) for w in POOL_WINDOWS]
        a = np.stack([np.kron(np.eye(rpb, dtype=np.float32), b[0]) for b in bands])
        inv = np.stack([np.broadcast_to(np.tile(1.0 / b[1], rpb)[:, None], (blk, POOL_GW))
                        for b in bands]).astype(np.float32)
        kern = functools.partial(_pool_grid_kernel, n_rows=n_rows)
        tb = seq_len
        scratch = [pltpu.VMEM((ng, seq_len, POOL_GW), F32),
                   pltpu.VMEM((seq_len, D_POOL), BF16)]
    else:
        blk = seq_len
        bands = [_band(seq_len, w) for w in POOL_WINDOWS]
        a = np.stack([b[0] for b in bands])
        inv = np.stack([np.broadcast_to((1.0 / b[1])[:, None], (blk, POOL_GW))
                        for b in bands]).astype(np.float32)
        n_seq_blk = 8
        kern = functools.partial(_pool_seq_kernel, n_seq_blk=n_seq_blk, seq_len=seq_len)
        tb = n_seq_blk * seq_len
        scratch = [pltpu.VMEM((tb, D_POOL), BF16)]
    return pl.pallas_call(
        kern,
        grid=(n_tok // tb,),
        in_specs=[
            pl.BlockSpec((tb, D_POOL), lambda i: (i, 6)),
            _const_spec((ng, blk, blk)),
            _const_spec((ng, blk, POOL_GW)),
            _const_spec((ng // 2, 2 * POOL_GW, 2 * POOL_GW)),
            _const_spec((1, D_POOL)),
        ],
        out_specs=pl.BlockSpec((tb, D_POOL), lambda i: (i, 0)),
        out_shape=jax.ShapeDtypeStruct((n_tok, D_POOL), BF16),
        scratch_shapes=scratch,
        compiler_params=_params(1),
        name="pool",
    )(p, jnp.asarray(a, BF16), jnp.asarray(inv, F32), pw_pairs, pool_scale)


def _layer_norm(u, g, b):
    mu = jnp.mean(u, axis=-1, keepdims=True)
    d = u - mu
    var = jnp.mean(d * d, axis=-1, keepdims=True)
    return d * lax.rsqrt(var + LN_EPS) * g + b


def _tail_kernel(x_ref, of_ref, ob_ref, gs_ref, yp_ref, g1_ref, sh2_ref, sc2_ref,
                 g2_ref, gn_ref, wo_ref, l1g_ref, l1b_ref, wg_ref, wu_ref, wd_ref,
                 l2g_ref, l2b_ref, o_ref, yr_scr, h2_scr):
    tm = x_ref.shape[0]
    gn = gn_ref[...]
    n_part = tm // PART
    n_rb = PART // ROWS
    ff_slices = [slice(lo, hi) for lo, hi in zip(FF_SPLITS[:-1], FF_SPLITS[1:])]

    def block(p, m):
        return slice(p * PART + m * ROWS, p * PART + (m + 1) * ROWS)

    def gated_norm_block(p, m):
        rows = block(p, m)
        for h in range(HEADS):
            cols = slice(h * D_HEAD, (h + 1) * D_HEAD)
            o = of_ref[rows, cols].astype(F32) + ob_ref[rows, cols].astype(F32)
            ms = jnp.mean(o * o, axis=-1, keepdims=True)
            on = o * lax.rsqrt(ms + RMS_EPS) * gn
            yr_scr[rows, cols] = (on * gs_ref[rows, cols].astype(F32)).astype(BF16)

    def norm1_block(p, m):
        rows = block(p, m)
        x1 = _layer_norm(o_ref[rows, :], l1g_ref[...], l1b_ref[...])
        o_ref[rows, :] = x1
        h2_scr[rows, :] = (x1 * (1.0 + sc2_ref[0]) + sh2_ref[0]).astype(BF16)

    def norm2_block(p, m):
        rows = block(p, m)
        o_ref[rows, :] = _layer_norm(o_ref[rows, :], l2g_ref[...], l2b_ref[...])

    def out_proj(p, fill):
        hr = slice(p * PART, (p + 1) * PART)
        y = jnp.dot(yr_scr[hr, :], wo_ref[0:D_REC, :], preferred_element_type=F32)
        fill(0)
        y = y + jnp.dot(yp_ref[hr, :], wo_ref[D_REC:2 * D_REC, :],
                        preferred_element_type=F32)
        fill(1)
        o_ref[hr, :] = ALPHA * x_ref[hr, :] + g1_ref[0] * y

    def ffn(p, fill):
        hr = slice(p * PART, (p + 1) * PART)
        h2 = h2_scr[hr, :]
        acc = None
        j = 0
        for cs in ff_slices:
            a = jnp.dot(h2, wg_ref[:, cs], preferred_element_type=F32)
            fill(j); j += 1
            b = jnp.dot(h2, wu_ref[:, cs], preferred_element_type=F32)
            fill(j); j += 1
            act = (a * jax.nn.sigmoid(a) * b).astype(BF16)
            part = jnp.dot(act, wd_ref[cs, :], preferred_element_type=F32)
            acc = part if acc is None else acc + part
        o_ref[hr, :] = ALPHA * o_ref[hr, :] + g2_ref[0] * acc

    def spread(tasks, n_slots):
        blocks = [(fn, p, m) for fn, p in tasks for m in range(n_rb)]

        def emit(j):
            lo = j * len(blocks) // n_slots
            hi = (j + 1) * len(blocks) // n_slots
            for fn, p, m in blocks[lo:hi]:
                fn(p, m)
        return emit

    n_ffn_slots = 2 * len(ff_slices)
    gn_, n1_, n2_ = gated_norm_block, norm1_block, norm2_block
    if n_part == 2:
        plan = [("op", 0, [(gn_, 1)]), ("op", 1, [(n1_, 0)]),
                ("ffn", 0, [(n1_, 1)]), ("ffn", 1, [(n2_, 0)])]
    else:
        assert n_part == 4
        plan = [("op", 0, [(gn_, 1)]), ("op", 1, [(n1_, 0)]),
                ("ffn", 0, [(gn_, 2), (n1_, 1)]), ("op", 2, [(gn_, 3)]),
                ("ffn", 1, [(n2_, 0)]), ("op", 3, [(n1_, 2)]),
                ("ffn", 2, [(n2_, 1), (n1_, 3)]), ("ffn", 3, [(n2_, 2)])]
    for m in range(n_rb):
        gated_norm_block(0, m)
    for kind, p, tasks in plan:
        if kind == "op":
            out_proj(p, spread(tasks, 2))
        else:
            ffn(p, spread(tasks, n_ffn_slots))
    for m in range(n_rb):
        norm2_block(n_part - 1, m)


def _tail_call(x2, o_f, o_b, p, yp, mod3, mod_row, gn, wo, l1g, l1b, wg, wu, wd,
               l2g, l2b, seq_len):
    n_tok = x2.shape[0]
    tm = TM_TAIL
    tps = max(1, seq_len // tm)

    def mod_spec(col):
        return pl.BlockSpec((1, 1, D_MODEL), lambda i: (mod_row(i, tps), 0, col))

    rec_spec = pl.BlockSpec((tm, D_REC), lambda i: (i, 0))
    return pl.pallas_call(
        _tail_kernel,
        grid=(n_tok // tm,),
        in_specs=[
            pl.BlockSpec((tm, D_MODEL), lambda i: (i, 0)),
            rec_spec, rec_spec,
            pl.BlockSpec((tm, D_REC), lambda i: (i, 5)),
            pl.BlockSpec((tm, D_POOL), lambda i: (i, 0)),
            mod_spec(2), mod_spec(3), mod_spec(4), mod_spec(5),
            _const_spec((1, D_HEAD)),
            _const_spec((D_MODEL, D_MODEL)),
            _const_spec((1, D_MODEL)), _const_spec((1, D_MODEL)),
            _const_spec((D_MODEL, D_FF)), _const_spec((D_MODEL, D_FF)),
            _const_spec((D_FF, D_MODEL)),
            _const_spec((1, D_MODEL)), _const_spec((1, D_MODEL)),
        ],
        out_specs=pl.BlockSpec((tm, D_MODEL), lambda i: (i, 0)),
        out_shape=jax.ShapeDtypeStruct((n_tok, D_MODEL), F32),
        scratch_shapes=[
            pltpu.VMEM((tm, D_REC), BF16),
            pltpu.VMEM((tm, D_MODEL), BF16),
        ],
        compiler_params=_params(1),
        name="tail",
    )(x2, o_f, o_b, p, yp, mod3, mod3, mod3, mod3, gn, wo, l1g, l1b, wg, wu, wd,
      l2g, l2b)


def kernel(x_prompt, x_sample, state_hgrn_fwd, state_hgrn_bwd, c, c_ctx, w_mod, b_mod,
           w_in, lb_fwd_raw, lb_bwd_raw, hgrn_norm_g, pool_w, pool_scale, w_out, ln1_g,
           ln1_b, w_ffn_gate, w_ffn_up, w_ffn_down, ln2_g, ln2_b):
    n_ctx, t_ctx, _ = x_prompt.shape
    n_lat, t_lat, _ = x_sample.shape
    ctx_row = n_lat

    c_all = jnp.concatenate(
        [c, c_ctx[None, :], jnp.zeros((16 - n_lat - 1, D_MODEL), F32)], axis=0)
    mod = _mod_call(c_all, w_mod[0], b_mod)
    mod3 = mod.reshape(16, 1, 6 * D_MODEL)

    w_in_b = w_in[0].astype(BF16)
    wo_b = w_out[0].astype(BF16)
    wg_b = w_ffn_gate[0].astype(BF16)
    wu_b = w_ffn_up[0].astype(BF16)
    wd_b = w_ffn_down[0].astype(BF16)
    pw_b = pool_w[0].astype(BF16)

    def run(x, n_seq, seq_len, mod_row, two_d, s0f, s0b, emit_final):
        x2 = x.reshape(n_seq * seq_len, D_MODEL)
        p, e = _proj_call(x2, mod3, mod_row, w_in_b, lb_fwd_raw, lb_bwd_raw, n_seq, seq_len)
        scan = _scan_call(p, e, n_seq, seq_len, s0f, s0b, emit_final)
        yp = _pool_call(p, pw_b, pool_scale, n_seq, seq_len, two_d)
        y = _tail_call(x2, scan[0], scan[1], p, yp, mod3, mod_row, hgrn_norm_g, wo_b,
                       ln1_g, ln1_b, wg_b, wu_b, wd_b, ln2_g, ln2_b, seq_len)
        return y.reshape(n_seq, seq_len, D_MODEL), scan[2:]

    y_ctx, fin = run(x_prompt, n_ctx, t_ctx, lambda i, tps: ctx_row, False,
                     None, None, True)
    y_lat, _ = run(x_sample, n_lat, t_lat, lambda i, tps: i // tps, True,
                   state_hgrn_fwd[:, 0], state_hgrn_bwd[:, 0], False)
    return (y_ctx, y_lat, fin[0][:, None], fin[1][:, None])
```

```python
import functools

import numpy as np
import jax
import jax.numpy as jnp
from jax import lax
from jax.experimental import pallas as pl
from jax.experimental.pallas import tpu as pltpu

F32 = jnp.float32
BF16 = jnp.bfloat16

D_MODEL = 1024
D_REC = 512
D_POOL = 512
HEADS = 4
D_HEAD = 128
D_IN = 5 * D_REC + D_POOL
D_FF = 2816
GRID_W = 64
POOL_WINDOWS = (2, 4, 8, 16)
POOL_GW = 128
ALPHA = 2.0 ** 0.25
LN_EPS = 1e-5
RMS_EPS = 1e-6

CHUNK = 64
MID_F = CHUNK // 2 - 1
MID_B = CHUNK // 2
TM_PROJ = 1024
TB_SCAN = 1024
SEQ_PER_SCAN = 4
TM_TAIL = 1024
PART = 256
ROWS = 64
FF_SPLITS = (0, 1536, D_FF)
P_COLS = 7 * D_REC
E_COLS = 4 * D_REC
VMEM_LIMIT = 56 * 1024 * 1024

_ARB = "arbitrary"


def _params(n_axes):
    return pltpu.CompilerParams(
        dimension_semantics=(_ARB,) * n_axes, vmem_limit_bytes=VMEM_LIMIT)


def _const_spec(shape):
    nd = len(shape)
    return pl.BlockSpec(shape, lambda *_: (0,) * nd, pipeline_mode=pl.Buffered(1))


def _mod_kernel(c_ref, w_ref, b_ref, o_ref):
    c = c_ref[...]
    hi, lo = _split_bf16(c * jax.nn.sigmoid(c))
    w = w_ref[...].astype(BF16)
    o_ref[...] = (jnp.dot(hi, w, preferred_element_type=F32)
                  + jnp.dot(lo, w, preferred_element_type=F32) + b_ref[...])


def _mod_call(c_all, w_mod, b_mod):
    rows = c_all.shape[0]
    n_out = w_mod.shape[1]
    blk = 1024
    return pl.pallas_call(
        _mod_kernel,
        grid=(n_out // blk,),
        in_specs=[
            pl.BlockSpec((rows, D_MODEL), lambda j: (0, 0)),
            pl.BlockSpec((D_MODEL, blk), lambda j: (0, j)),
            pl.BlockSpec((1, blk), lambda j: (0, j)),
        ],
        out_specs=pl.BlockSpec((rows, blk), lambda j: (0, j)),
        out_shape=jax.ShapeDtypeStruct((rows, n_out), F32),
        compiler_params=_params(1),
        name="mod",
    )(c_all, w_mod, b_mod)


def _lower_bound(raw_ref):
    r0 = raw_ref[0:1, :]
    r1 = raw_ref[1:2, :]
    mx = jnp.maximum(r0, r1)
    e0 = jnp.exp(r0 - mx)
    e1 = jnp.exp(r1 - mx)
    return e0 / (e0 + e1)


def _split_bf16(a):
    hi = a.astype(BF16)
    lo = (a - hi.astype(F32)).astype(BF16)
    return hi, lo


def _proj_kernel(x_ref, sh_ref, sc_ref, w_ref, lbf_ref, lbb_ref, p_ref, e_ref,
                 proj_scr, *, cps):
    tm = x_ref.shape[0]
    h = (x_ref[...] * (1.0 + sc_ref[0]) + sh_ref[0]).astype(BF16)
    proj_scr[...] = jnp.dot(h, w_ref[...], preferred_element_type=F32)

    lb_f = _lower_bound(lbf_ref)
    lb_b = _lower_bound(lbb_ref)
    ri = lax.broadcasted_iota(jnp.int32, (CHUNK, CHUNK), 0)
    ci = lax.broadcasted_iota(jnp.int32, (CHUNK, CHUNK), 1)
    tri_pre = jnp.where(ci <= ri, 1.0, 0.0).astype(BF16)
    tri_suf = jnp.where(ci >= ri, 1.0, 0.0).astype(BF16)

    def chunk_sum(tri, a):
        hi, lo = _split_bf16(a)
        return (jnp.dot(tri, hi, preferred_element_type=F32)
                + jnp.dot(tri, lo, preferred_element_type=F32))

    for c in range(tm // CHUNK):
        rows = slice(c * CHUNK, (c + 1) * CHUNK)
        q = proj_scr[rows, 0:D_REC]
        qs = q * jax.nn.sigmoid(q)

        def direction(z, lb, tri, mid, end):
            f = lb + (1.0 - lb) * jax.nn.sigmoid(z)
            k = 1.0 - f
            b = chunk_sum(tri, jnp.log(f))
            m = b[mid:mid + 1, :]
            d = b - m
            qd = qs * jnp.exp(d)
            ki = k * jnp.exp(-d)
            em = jnp.exp(m)
            r = jnp.exp(b[end:end + 1, :] - m)
            return qd, ki, em, r

        qd_f, ki_f, em_f, r_f = direction(
            proj_scr[rows, D_REC:2 * D_REC], lb_f, tri_pre, MID_F, CHUNK - 1)
        qd_b, ki_b, em_b, r_b = direction(
            proj_scr[rows, 2 * D_REC:3 * D_REC], lb_b, tri_suf, MID_B, 0)
        g = proj_scr[rows, 4 * D_REC:5 * D_REC]

        p_ref[rows, 0 * D_REC:1 * D_REC] = qd_f.astype(BF16)
        p_ref[rows, 1 * D_REC:2 * D_REC] = ki_f.astype(BF16)
        p_ref[rows, 2 * D_REC:3 * D_REC] = qd_b.astype(BF16)
        p_ref[rows, 3 * D_REC:4 * D_REC] = ki_b.astype(BF16)
        p_ref[rows, 4 * D_REC:5 * D_REC] = proj_scr[rows, 3 * D_REC:4 * D_REC].astype(BF16)
        p_ref[rows, 5 * D_REC:6 * D_REC] = (g * jax.nn.sigmoid(g)).astype(BF16)
        p_ref[rows, 6 * D_REC:7 * D_REC] = proj_scr[rows, 5 * D_REC:6 * D_REC].astype(BF16)

        s, cc = c // cps, c % cps
        e_ref[s, cc:cc + 1, 0 * D_REC:1 * D_REC] = em_f
        e_ref[s, cc:cc + 1, 1 * D_REC:2 * D_REC] = r_f
        e_ref[s, cc:cc + 1, 2 * D_REC:3 * D_REC] = em_b
        e_ref[s, cc:cc + 1, 3 * D_REC:4 * D_REC] = r_b


def _proj_call(x2, mod3, mod_row, w_in, lbf_raw, lbb_raw, n_seq, seq_len):
    n_tok = x2.shape[0]
    tm = TM_PROJ
    cpt = tm // CHUNK
    cps = min(cpt, seq_len // CHUNK)
    spt = cpt // cps
    tps = max(1, seq_len // tm)
    n_ck = seq_len // CHUNK

    if spt > 1:
        e_spec = pl.BlockSpec((spt, cps, E_COLS), lambda i: (i, 0, 0))
    else:
        e_spec = pl.BlockSpec((1, cps, E_COLS), lambda i: (i // tps, i % tps, 0))

    return pl.pallas_call(
        functools.partial(_proj_kernel, cps=cps),
        grid=(n_tok // tm,),
        in_specs=[
            pl.BlockSpec((tm, D_MODEL), lambda i: (i, 0)),
            pl.BlockSpec((1, 1, D_MODEL), lambda i: (mod_row(i, tps), 0, 0)),
            pl.BlockSpec((1, 1, D_MODEL), lambda i: (mod_row(i, tps), 0, 1)),
            _const_spec((D_MODEL, D_IN)),
            _const_spec((2, D_REC)),
            _const_spec((2, D_REC)),
        ],
        out_specs=[
            pl.BlockSpec((tm, P_COLS), lambda i: (i, 0)),
            e_spec,
        ],
        out_shape=[
            jax.ShapeDtypeStruct((n_tok, P_COLS), BF16),
            jax.ShapeDtypeStruct((n_seq, n_ck, E_COLS), F32),
        ],
        scratch_shapes=[pltpu.VMEM((tm, D_IN), F32)],
        compiler_params=_params(1),
        name="proj",
    )(x2, mod3, mod3, w_in, lbf_raw, lbb_raw)


def _scan_kernel(*refs, nck, spb, has_init, emit_final):
    it = iter(refs)
    qdf_ref, kif_ref, vf_ref, ef_ref = (next(it) for _ in range(4))
    qdb_ref, kib_ref, vb_ref, eb_ref = (next(it) for _ in range(4))
    if has_init:
        s0f_ref, s0b_ref = next(it), next(it)
    of_ref, ob_ref = next(it), next(it)
    if emit_final:
        finf_ref, finb_ref = next(it), next(it)
    sf_scr, sb_scr, uf_scr, ub_scr, scf_scr, scb_scr, tf_scr, tb_scr = (
        next(it) for _ in range(8))

    j = pl.program_id(1)
    seq_heads = [(q, h) for q in range(spb) for h in range(HEADS)]
    ri = lax.broadcasted_iota(jnp.int32, (CHUNK, CHUNK), 0)
    ci = lax.broadcasted_iota(jnp.int32, (CHUNK, CHUNK), 1)
    nt_dims = (((1,), (1,)), ((), ()))
    tn_dims = (((0,), (0,)), ((), ()))

    @pl.when(j == 0)
    def _():
        for q, h in seq_heads:
            if has_init:
                sf_scr[q, h] = s0f_ref[q, h].T
                sb_scr[q, h] = s0b_ref[q, h].T
            else:
                sf_scr[q, h] = jnp.zeros((D_HEAD, D_HEAD), F32)
                sb_scr[q, h] = jnp.zeros((D_HEAD, D_HEAD), F32)

    def tile(q, c, h):
        return (slice((q * nck + c) * CHUNK, (q * nck + c + 1) * CHUNK),
                slice(h * D_HEAD, (h + 1) * D_HEAD))

    def prepare(qd_ref, ki_ref, v_ref, keep, u_scr, sc_scr):
        for q in range(spb):
            for c in range(nck):
                for h in range(HEADS):
                    rows, cols = tile(q, c, h)
                    u_scr[q, c, h] = lax.dot_general(
                        v_ref[rows, cols], ki_ref[rows, cols], tn_dims,
                        preferred_element_type=F32)
                    sc = lax.dot_general(qd_ref[rows, cols], ki_ref[rows, cols], nt_dims,
                                         preferred_element_type=F32)
                    sc_scr[q, c, h] = jnp.where(keep, sc, 0.0).astype(BF16)

    def step(qd_ref, v_ref, e_ref, o_ref, s_scr, u_scr, sc_scr, smt_scr, c):
        for q, h in seq_heads:
            rows, cols = tile(q, c, h)
            em = e_ref[q, c:c + 1, h * D_HEAD:(h + 1) * D_HEAD]
            r = e_ref[q, c:c + 1, D_REC + h * D_HEAD:D_REC + (h + 1) * D_HEAD]
            sm = s_scr[q, h] * em
            smt_scr[q, h] = sm.astype(BF16).T
            o = jnp.dot(qd_ref[rows, cols], smt_scr[q, h], preferred_element_type=F32)
            o = o + jnp.dot(sc_scr[q, c, h], v_ref[rows, cols],
                            preferred_element_type=F32)
            o_ref[rows, cols] = o.astype(BF16)
            s_scr[q, h] = (sm + u_scr[q, c, h]) * r

    prepare(qdf_ref, kif_ref, vf_ref, ci <= ri, uf_scr, scf_scr)
    prepare(qdb_ref, kib_ref, vb_ref, ci >= ri, ub_scr, scb_scr)
    for c in range(nck):
        step(qdf_ref, vf_ref, ef_ref, of_ref, sf_scr, uf_scr, scf_scr, tf_scr, c)
        step(qdb_ref, vb_ref, eb_ref, ob_ref, sb_scr, ub_scr, scb_scr, tb_scr, nck - 1 - c)

    if emit_final:
        @pl.when(j == pl.num_programs(1) - 1)
        def _():
            for q, h in seq_heads:
                finf_ref[q, h] = sf_scr[q, h].T
                finb_ref[q, h] = sb_scr[q, h].T


def _scan_call(p, e, n_seq, seq_len, s0f=None, s0b=None, emit_final=False):
    tb = min(seq_len, TB_SCAN)
    nt = seq_len // tb
    spb = SEQ_PER_SCAN if nt == 1 else 1
    nck = tb // CHUNK
    has_init = s0f is not None
    st_blk = (spb, HEADS, D_HEAD, D_HEAD)
    rows = spb * tb
    n_tok = p.shape[0]

    def fwd(b, j):
        return b * nt + j

    def bwd(b, j):
        return b * nt + nt - 1 - j

    def p_spec(blk, col):
        return pl.BlockSpec((rows, D_REC), lambda b, j: (blk(b, j), col))

    in_specs = [
        p_spec(fwd, 0), p_spec(fwd, 1), p_spec(fwd, 4),
        pl.BlockSpec((spb, nck, 2 * D_REC), lambda b, j: (b, j, 0)),
        p_spec(bwd, 2), p_spec(bwd, 3), p_spec(bwd, 4),
        pl.BlockSpec((spb, nck, 2 * D_REC), lambda b, j: (b, nt - 1 - j, 1)),
    ]
    args = [p, p, p, e, p, p, p, e]
    if has_init:
        in_specs += [pl.BlockSpec(st_blk, lambda b, j: (b, 0, 0, 0))] * 2
        args += [s0f, s0b]
    o_shape = jax.ShapeDtypeStruct((n_tok, D_REC), BF16)
    out_specs = [
        pl.BlockSpec((rows, D_REC), lambda b, j: (fwd(b, j), 0)),
        pl.BlockSpec((rows, D_REC), lambda b, j: (bwd(b, j), 0)),
    ]
    out_shape = [o_shape, o_shape]
    if emit_final:
        out_specs += [pl.BlockSpec(st_blk, lambda b, j: (b, 0, 0, 0))] * 2
        out_shape += [jax.ShapeDtypeStruct((n_seq, HEADS, D_HEAD, D_HEAD), F32)] * 2

    return pl.pallas_call(
        functools.partial(_scan_kernel, nck=nck, spb=spb, has_init=has_init,
                          emit_final=emit_final),
        grid=(n_seq // spb, nt),
        in_specs=in_specs,
        out_specs=out_specs,
        out_shape=out_shape,
        scratch_shapes=([pltpu.VMEM((spb, HEADS, D_HEAD, D_HEAD), F32)] * 2
                        + [pltpu.VMEM((spb, nck, HEADS, D_HEAD, D_HEAD), F32)] * 2
                        + [pltpu.VMEM((spb, nck, HEADS, CHUNK, CHUNK), BF16)] * 2
                        + [pltpu.VMEM((spb, HEADS, D_HEAD, D_HEAD), BF16)] * 2),
        compiler_params=_params(2),
        name="scan",
    )(*args)


def _window(n, w):
    idx = np.arange(n)
    return np.clip(idx - w // 2, 0, n), np.clip(idx + w // 2, 0, n)


def _band(n, w):
    lo, hi = _window(n, w)
    j = np.arange(n)[None, :]
    return ((j >= lo[:, None]) & (j < hi[:, None])).astype(np.float32), (hi - lo)


def _pool_seq_kernel(u_ref, a_ref, inv_ref, pw_ref, ps_ref, o_ref, diff_scr, *,
                     n_seq_blk, seq_len):
    for gi in range(len(POOL_WINDOWS)):
        cols = slice(gi * POOL_GW, (gi + 1) * POOL_GW)
        for s in range(0, n_seq_blk, 2):
            r0 = slice(s * seq_len, (s + 1) * seq_len)
            r1 = slice((s + 1) * seq_len, (s + 2) * seq_len)
            u0, u1 = u_ref[r0, cols], u_ref[r1, cols]
            tot = jnp.dot(a_ref[gi], jnp.concatenate([u0, u1], axis=1),
                          preferred_element_type=F32)
            diff_scr[r0, cols] = (tot[:, 0:POOL_GW] * inv_ref[gi]
                                  - u0.astype(F32)).astype(BF16)
            diff_scr[r1, cols] = (tot[:, POOL_GW:2 * POOL_GW] * inv_ref[gi]
                                  - u1.astype(F32)).astype(BF16)
    _mix_groups(diff_scr, pw_ref, ps_ref, o_ref)


def _mix_groups(diff_scr, pw_ref, ps_ref, o_ref):
    for k in range(pw_ref.shape[0]):
        cols = slice(2 * k * POOL_GW, (2 * k + 2) * POOL_GW)
        y = jnp.dot(diff_scr[:, cols], pw_ref[k], preferred_element_type=F32)
        o_ref[:, cols] = (y * ps_ref[:, cols]).astype(BF16)


def _pool_grid_kernel(u_ref, a_ref, invc_ref, pw_ref, ps_ref, o_ref, col_scr, diff_scr,
                      *, n_rows):
    blk = a_ref.shape[1]
    rpb = blk // GRID_W
    for gi, w in enumerate(POOL_WINDOWS):
        cols = slice(gi * POOL_GW, (gi + 1) * POOL_GW)
        for rb in range(0, n_rows // rpb, 2):
            r0 = slice(rb * blk, (rb + 1) * blk)
            r1 = slice((rb + 1) * blk, (rb + 2) * blk)
            pair = jnp.concatenate([u_ref[r0, cols], u_ref[r1, cols]], axis=1)
            tot = jnp.dot(a_ref[gi], pair, preferred_element_type=F32)
            col_scr[gi, r0, :] = tot[:, 0:POOL_GW] * invc_ref[gi]
            col_scr[gi, r1, :] = tot[:, POOL_GW:2 * POOL_GW] * invc_ref[gi]
    for gi, w in enumerate(POOL_WINDOWS):
        cols = slice(gi * POOL_GW, (gi + 1) * POOL_GW)
        lo, hi = _window(n_rows, w)

        def grid_row(rr):
            return col_scr[gi, rr * GRID_W:(rr + 1) * GRID_W, :]

        acc = None
        for r in range(n_rows):
            if r == 0:
                acc = grid_row(int(lo[0]))
                for rr in range(int(lo[0]) + 1, int(hi[0])):
                    acc = acc + grid_row(rr)
            else:
                for rr in range(int(hi[r - 1]), int(hi[r])):
                    acc = acc + grid_row(rr)
                for rr in range(int(lo[r - 1]), int(lo[r])):
                    acc = acc - grid_row(rr)
            rows = slice(r * GRID_W, (r + 1) * GRID_W)
            mean = acc * (1.0 / float(hi[r] - lo[r]))
            diff_scr[rows, cols] = (mean - u_ref[rows, cols].astype(F32)).astype(BF16)
    _mix_groups(diff_scr, pw_ref, ps_ref, o_ref)


def _pool_call(p, pool_w, pool_scale, n_seq, seq_len, two_d):
    n_tok = p.shape[0]
    ng = len(POOL_WINDOWS)
    zero = jnp.zeros((POOL_GW, POOL_GW), pool_w.dtype)
    pw_pairs = jnp.stack([jnp.block([[pool_w[2 * k], zero], [zero, pool_w[2 * k + 1]]])
                          for k in range(ng // 2)])
    if two_d:
        n_rows = seq_len // GRID_W
        rpb = 4
        blk = rpb * GRID_W
        bands = [_band(GRID_W, w) for w in POOL_WINDOWS]
        a = np.stack([np.kron(np.eye(rpb, dtype=np.float32), b[0]) for b in bands])
        inv = np.stack([np.broadcast_to(np.tile(1.0 / b[1], rpb)[:, None], (blk, POOL_GW))
                        for b in bands]).astype(np.float32)
        kern = functools.partial(_pool_grid_kernel, n_rows=n_rows)
        tb = seq_len
        scratch = [pltpu.VMEM((ng, seq_len, POOL_GW), F32),
                   pltpu.VMEM((seq_len, D_POOL), BF16)]
    else:
        blk = seq_len
        bands = [_band(seq_len, w) for w in POOL_WINDOWS]
        a = np.stack([b[0] for b in bands])
        inv = np.stack([np.broadcast_to((1.0 / b[1])[:, None], (blk, POOL_GW))
                        for b in bands]).astype(np.float32)
        n_seq_blk = 8
        kern = functools.partial(_pool_seq_kernel, n_seq_blk=n_seq_blk, seq_len=seq_len)
        tb = n_seq_blk * seq_len
        scratch = [pltpu.VMEM((tb, D_POOL), BF16)]
    return pl.pallas_call(
        kern,
        grid=(n_tok // tb,),
        in_specs=[
            pl.BlockSpec((tb, D_POOL), lambda i: (i, 6)),
            _const_spec((ng, blk, blk)),
            _const_spec((ng, blk, POOL_GW)),
            _const_spec((ng // 2, 2 * POOL_GW, 2 * POOL_GW)),
            _const_spec((1, D_POOL)),
        ],
        out_specs=pl.BlockSpec((tb, D_POOL), lambda i: (i, 0)),
        out_shape=jax.ShapeDtypeStruct((n_tok, D_POOL), BF16),
        scratch_shapes=scratch,
        compiler_params=_params(1),
        name="pool",
    )(p, jnp.asarray(a, BF16), jnp.asarray(inv, F32), pw_pairs, pool_scale)


def _layer_norm(u, g, b):
    mu = jnp.mean(u, axis=-1, keepdims=True)
    d = u - mu
    var = jnp.mean(d * d, axis=-1, keepdims=True)
    return d * lax.rsqrt(var + LN_EPS) * g + b


def _tail_kernel(x_ref, of_ref, ob_ref, gs_ref, yp_ref, g1_ref, sh2_ref, sc2_ref,
                 g2_ref, gn_ref, wo_ref, l1g_ref, l1b_ref, wg_ref, wu_ref, wd_ref,
                 l2g_ref, l2b_ref, o_ref, yr_scr, h2_scr):
    tm = x_ref.shape[0]
    gn = gn_ref[...]
    n_part = tm // PART
    n_rb = PART // ROWS
    ff_slices = [slice(lo, hi) for lo, hi in zip(FF_SPLITS[:-1], FF_SPLITS[1:])]

    def block(p, m):
        return slice(p * PART + m * ROWS, p * PART + (m + 1) * ROWS)

    def gated_norm_block(p, m):
        rows = block(p, m)
        for h in range(HEADS):
            cols = slice(h * D_HEAD, (h + 1) * D_HEAD)
            o = of_ref[rows, cols].astype(F32) + ob_ref[rows, cols].astype(F32)
            ms = jnp.mean(o * o, axis=-1, keepdims=True)
            on = o * lax.rsqrt(ms + RMS_EPS) * gn
            yr_scr[rows, cols] = (on * gs_ref[rows, cols].astype(F32)).astype(BF16)

    def norm1_block(p, m):
        rows = block(p, m)
        x1 = _layer_norm(o_ref[rows, :], l1g_ref[...], l1b_ref[...])
        o_ref[rows, :] = x1
        h2_scr[rows, :] = (x1 * (1.0 + sc2_ref[0]) + sh2_ref[0]).astype(BF16)

    def norm2_block(p, m):
        rows = block(p, m)
        o_ref[rows, :] = _layer_norm(o_ref[rows, :], l2g_ref[...], l2b_ref[...])

    def out_proj(p, fill):
        hr = slice(p * PART, (p + 1) * PART)
        y = jnp.dot(yr_scr[hr, :], wo_ref[0:D_REC, :], preferred_element_type=F32)
        fill(0)
        y = y + jnp.dot(yp_ref[hr, :], wo_ref[D_REC:2 * D_REC, :],
                        preferred_element_type=F32)
        fill(1)
        o_ref[hr, :] = ALPHA * x_ref[hr, :] + g1_ref[0] * y

    def ffn(p, fill):
        hr = slice(p * PART, (p + 1) * PART)
        h2 = h2_scr[hr, :]
        acc = None
        j = 0
        for cs in ff_slices:
            a = jnp.dot(h2, wg_ref[:, cs], preferred_element_type=F32)
            fill(j); j += 1
            b = jnp.dot(h2, wu_ref[:, cs], preferred_element_type=F32)
            fill(j); j += 1
            act = (a * jax.nn.sigmoid(a) * b).astype(BF16)
            part = jnp.dot(act, wd_ref[cs, :], preferred_element_type=F32)
            acc = part if acc is None else acc + part
        o_ref[hr, :] = ALPHA * o_ref[hr, :] + g2_ref[0] * acc

    def spread(tasks, n_slots):
        blocks = [(fn, p, m) for fn, p in tasks for m in range(n_rb)]

        def emit(j):
            lo = j * len(blocks) // n_slots
            hi = (j + 1) * len(blocks) // n_slots
            for fn, p, m in blocks[lo:hi]:
                fn(p, m)
        return emit

    n_ffn_slots = 2 * len(ff_slices)
    gn_, n1_, n2_ = gated_norm_block, norm1_block, norm2_block
    if n_part == 2:
        plan = [("op", 0, [(gn_, 1)]), ("op", 1, [(n1_, 0)]),
                ("ffn", 0, [(n1_, 1)]), ("ffn", 1, [(n2_, 0)])]
    else:
        assert n_part == 4
        plan = [("op", 0, [(gn_, 1)]), ("op", 1, [(n1_, 0)]),
                ("ffn", 0, [(gn_, 2), (n1_, 1)]), ("op", 2, [(gn_, 3)]),
                ("ffn", 1, [(n2_, 0)]), ("op", 3, [(n1_, 2)]),
                ("ffn", 2, [(n2_, 1), (n1_, 3)]), ("ffn", 3, [(n2_, 2)])]
    for m in range(n_rb):
        gated_norm_block(0, m)
    for kind, p, tasks in plan:
        if kind == "op":
            out_proj(p, spread(tasks, 2))
        else:
            ffn(p, spread(tasks, n_ffn_slots))
    for m in range(n_rb):
        norm2_block(n_part - 1, m)


def _tail_call(x2, o_f, o_b, p, yp, mod3, mod_row, gn, wo, l1g, l1b, wg, wu, wd,
               l2g, l2b, seq_len):
    n_tok = x2.shape[0]
    tm = TM_TAIL if seq_len >= TM_TAIL else TM_TAIL // 2
    tps = max(1, seq_len // tm)

    def mod_spec(col):
        return pl.BlockSpec((1, 1, D_MODEL), lambda i: (mod_row(i, tps), 0, col))

    rec_spec = pl.BlockSpec((tm, D_REC), lambda i: (i, 0))
    return pl.pallas_call(
        _tail_kernel,
        grid=(n_tok // tm,),
        in_specs=[
            pl.BlockSpec((tm, D_MODEL), lambda i: (i, 0)),
            rec_spec, rec_spec,
            pl.BlockSpec((tm, D_REC), lambda i: (i, 5)),
            pl.BlockSpec((tm, D_POOL), lambda i: (i, 0)),
            mod_spec(2), mod_spec(3), mod_spec(4), mod_spec(5),
            _const_spec((1, D_HEAD)),
            _const_spec((D_MODEL, D_MODEL)),
            _const_spec((1, D_MODEL)), _const_spec((1, D_MODEL)),
            _const_spec((D_MODEL, D_FF)), _const_spec((D_MODEL, D_FF)),
            _const_spec((D_FF, D_MODEL)),
            _const_spec((1, D_MODEL)), _const_spec((1, D_MODEL)),
        ],
        out_specs=pl.BlockSpec((tm, D_MODEL), lambda i: (i, 0)),
        out_shape=jax.ShapeDtypeStruct((n_tok, D_MODEL), F32),
        scratch_shapes=[
            pltpu.VMEM((tm, D_REC), BF16),
            pltpu.VMEM((tm, D_MODEL), BF16),
        ],
        compiler_params=_params(1),
        name="tail",
    )(x2, o_f, o_b, p, yp, mod3, mod3, mod3, mod3, gn, wo, l1g, l1b, wg, wu, wd,
      l2g, l2b)


def kernel(x_prompt, x_sample, state_hgrn_fwd, state_hgrn_bwd, c, c_ctx, w_mod, b_mod,
           w_in, lb_fwd_raw, lb_bwd_raw, hgrn_norm_g, pool_w, pool_scale, w_out, ln1_g,
           ln1_b, w_ffn_gate, w_ffn_up, w_ffn_down, ln2_g, ln2_b):
    n_ctx, t_ctx, _ = x_prompt.shape
    n_lat, t_lat, _ = x_sample.shape
    ctx_row = n_lat

    c_all = jnp.concatenate(
        [c, c_ctx[None, :], jnp.zeros((16 - n_lat - 1, D_MODEL), F32)], axis=0)
    mod = _mod_call(c_all, w_mod[0], b_mod)
    mod3 = mod.reshape(16, 1, 6 * D_MODEL)

    w_in_b = w_in[0].astype(BF16)
    wo_b = w_out[0].astype(BF16)
    wg_b = w_ffn_gate[0].astype(BF16)
    wu_b = w_ffn_up[0].astype(BF16)
    wd_b = w_ffn_down[0].astype(BF16)
    pw_b = pool_w[0].astype(BF16)

    def run(x, n_seq, seq_len, mod_row, two_d, s0f, s0b, emit_final):
        x2 = x.reshape(n_seq * seq_len, D_MODEL)
        p, e = _proj_call(x2, mod3, mod_row, w_in_b, lb_fwd_raw, lb_bwd_raw, n_seq, seq_len)
        scan = _scan_call(p, e, n_seq, seq_len, s0f, s0b, emit_final)
        yp = _pool_call(p, pw_b, pool_scale, n_seq, seq_len, two_d)
        y = _tail_call(x2, scan[0], scan[1], p, yp, mod3, mod_row, hgrn_norm_g, wo_b,
                       ln1_g, ln1_b, wg_b, wu_b, wd_b, ln2_g, ln2_b, seq_len)
        return y.reshape(n_seq, seq_len, D_MODEL), scan[2:]

    y_ctx, fin = run(x_prompt, n_ctx, t_ctx, lambda i, tps: ctx_row, False,
                     None, None, True)
    y_lat, _ = run(x_sample, n_lat, t_lat, lambda i, tps: i // tps, True,
                   state_hgrn_fwd[:, 0], state_hgrn_bwd[:, 0], False)
    return (y_ctx, y_lat, fin[0][:, None], fin[1][:, None])
```
